```python
import math
import jax, jax.numpy as jnp
from jax import lax
import numpy as np

D_MODEL = 1024
BATCH = 8
SEQ = 2048
DEPTH = 1
DEC_BATCH = 128
DEC_SEQ = 1
PAST_LEN = 16384
PAGE_SIZE = 128

DN_HEADS = 8
DN_DK = 128
DN_DV = 128
DN_CONV = 4
DN_CHUNK = 64
RET_HEADS = 4
RET_DK = 256
RET_DV = 512
RET_CHUNK = 128
ROPE_BASE = 10000.0
N_EXPERTS = 64
TOP_K = 8
N_GROUPS = 8
TOPK_GROUPS = 4
D_EXPERT = 256
D_SHARED = 256
ROUTED_SCALE = 2.5
MOE_BLOCK = 128
MOD_CHUNKS = 6
EPS = 1e-6

DN_QK = DN_HEADS * DN_DK
DN_VW = DN_HEADS * DN_DV
DN_CONV_W = 2 * DN_QK + DN_VW
RET_QK = RET_HEADS * RET_DK
RET_VW = RET_HEADS * RET_DV
IN_SPLITS = (DN_CONV_W, DN_HEADS, DN_HEADS, DN_VW, RET_QK, RET_QK, RET_VW, RET_VW, D_MODEL, D_MODEL)
D_IN_PROJ = sum(IN_SPLITS)

kernel_name = "hybrid_deltanet_retention_moe_adaln_step"


def _rms_norm(x, w):
    xf = x.astype(jnp.float32)
    y = xf * lax.rsqrt(jnp.mean(xf * xf, axis=-1, keepdims=True) + EPS)
    return (y * w.astype(jnp.float32)).astype(x.dtype)


def _l2norm(x):
    xf = x.astype(jnp.float32)
    return xf * lax.rsqrt(jnp.sum(xf * xf, axis=-1, keepdims=True) + EPS)


def _heads(t, n_heads):
    b, l, w = t.shape
    return t.reshape(b, l, n_heads, w // n_heads).transpose(0, 2, 1, 3)


def _to_chunks(t, c):
    b, h, l = t.shape[:3]
    t = t.reshape(b, h, l // c, c, *t.shape[3:])
    return jnp.moveaxis(t, 2, 0)


def _causal_conv(u, buf, w):
    l = u.shape[1]
    ext = jnp.concatenate([buf.astype(u.dtype), u], axis=1)
    out = ext[:, 0:l] * w[0]
    for i in range(1, DN_CONV):
        out = out + ext[:, i:i + l] * w[i]
    return jax.nn.silu(out), ext[:, l:]


def _rotary(x, pos):
    half = x.shape[-1] // 2
    inv = ROPE_BASE ** (-jnp.arange(half, dtype=jnp.float32) / half)
    ang = pos.astype(jnp.float32)[:, None] * inv[None, :]
    cos, sin = jnp.cos(ang), jnp.sin(ang)
    xf = x.astype(jnp.float32)
    x1, x2 = xf[..., :half], xf[..., half:]
    return jnp.concatenate([x1 * cos - x2 * sin, x1 * sin + x2 * cos], axis=-1)


def _gated_delta_rule(q, k, v, g, beta, s0):
    b, h, l, dk = q.shape
    dv = v.shape[-1]
    c = DN_CHUNK if l % DN_CHUNK == 0 else l
    q, k, v, g, beta = (_to_chunks(t.astype(jnp.float32), c) for t in (q, k, v, g, beta))
    gc = jnp.cumsum(g, axis=-1)
    idx = jnp.arange(c)
    causal = idx[:, None] >= idx[None, :]
    strict = idx[:, None] > idx[None, :]
    diff = gc[..., :, None] - gc[..., None, :]
    decay = jnp.where(causal, jnp.exp(jnp.where(causal, diff, 0.0)), 0.0)
    kb = k * beta[..., None]
    a_mat = jnp.where(strict, jnp.einsum('nbhik,nbhjk->nbhij', kb, k) * decay, 0.0)
    lhs = a_mat + jnp.eye(c, dtype=jnp.float32)
    rhs = jnp.concatenate([v * beta[..., None], kb * jnp.exp(gc)[..., None]], axis=-1)
    sol = lax.linalg.triangular_solve(lhs, rhs, left_side=True, lower=True, transpose_a=False, unit_diagonal=True)
    u, w = sol[..., :dv], sol[..., dv:]
    qk = jnp.where(causal, jnp.einsum('nbhik,nbhjk->nbhij', q, k) * decay, 0.0)
    q_dec = q * jnp.exp(gc)[..., None]
    g_last = gc[..., -1]
    k_end = k * jnp.exp(g_last[..., None] - gc)[..., None]

    def step(s, xs):
        q_n, w_n, u_n, qk_n, k_n, gl_n = xs
        v_new = u_n - jnp.einsum('bhck,bhkv->bhcv', w_n, s)
        o = jnp.einsum('bhck,bhkv->bhcv', q_n, s) + jnp.einsum('bhij,bhjv->bhiv', qk_n, v_new)
        s = s * jnp.exp(gl_n)[..., None, None] + jnp.einsum('bhck,bhcv->bhkv', k_n, v_new)
        return s, o

    s_fin, o = lax.scan(step, s0.astype(jnp.float32), (q_dec, w, u, qk, k_end, g_last))
    o = jnp.moveaxis(o, 0, 2).reshape(b, h, l, dv)
    return o, s_fin


def _retention(q, k, v, s0):
    b, h, l, dk = q.shape
    dv = v.shape[-1]
    c = RET_CHUNK if l % RET_CHUNK == 0 else l
    log_gamma = jnp.log(1.0 - 2.0 ** (-5.0 - jnp.arange(h, dtype=jnp.float32)))
    q, k, v = (_to_chunks(t.astype(jnp.float32), c) for t in (q, k, v))
    idx = jnp.arange(c, dtype=jnp.float32)
    rel = idx[:, None] - idx[None, :]
    d_mat = jnp.where(rel >= 0, jnp.exp(jnp.maximum(rel, 0.0) * log_gamma[:, None, None]), 0.0)
    intra = jnp.einsum('nbhij,nbhjv->nbhiv', jnp.einsum('nbhik,nbhjk->nbhij', q, k) * d_mat, v)
    q_dec = q * jnp.exp((idx[None, :] + 1.0) * log_gamma[:, None])[:, :, None]
    k_dec = k * jnp.exp((c - 1.0 - idx[None, :]) * log_gamma[:, None])[:, :, None]
    chunk_decay = jnp.exp(c * log_gamma)[:, None, None]

    def step(s, xs):
        q_n, k_n, v_n = xs
        o = jnp.einsum('bhck,bhkv->bhcv', q_n, s)
        s = s * chunk_decay + jnp.einsum('bhck,bhcv->bhkv', k_n, v_n)
        return s, o

    s_fin, cross = lax.scan(step, s0.astype(jnp.float32), (q_dec, k_dec, v))
    o = jnp.moveaxis(intra + cross, 0, 2).reshape(b, h, l, dv)
    return o, s_fin


def _token_mixers(h, pos, conv_buf, s_delta, s_ret, w_in, conv_w, a_log, dt_bias, dn_norm_w,
                  ret_gn_w, ret_gn_b, w_down_a, w_down_b, w_out):
    b, l, _ = h.shape
    points = np.cumsum(IN_SPLITS)[:-1].tolist()
    qkv_a, b_a, a_a, z_a, q_b, k_b, v_b, g_b, gate_a, gate_b = jnp.split(h @ w_in, points, axis=-1)
    qkv_a, conv_new = _causal_conv(qkv_a, conv_buf, conv_w)
    q_a, k_a, v_a = jnp.split(qkv_a, [DN_QK, 2 * DN_QK], axis=-1)
    q_a = _l2norm(_heads(q_a, DN_HEADS)) * (DN_DK ** -0.5)
    k_a = _l2norm(_heads(k_a, DN_HEADS))
    v_a = _heads(v_a, DN_HEADS)
    beta = jax.nn.sigmoid(b_a.astype(jnp.float32)).transpose(0, 2, 1)
    g = (-jnp.exp(a_log.astype(jnp.float32))
         * jax.nn.softplus(a_a.astype(jnp.float32) + dt_bias.astype(jnp.float32))).transpose(0, 2, 1)
    o_a, s_delta_new = _gated_delta_rule(q_a, k_a, v_a, g, beta, s_delta)
    o_a = o_a.transpose(0, 2, 1, 3).astype(h.dtype)
    z = z_a.reshape(b, l, DN_HEADS, DN_DV)
    o_a = (_rms_norm(o_a, dn_norm_w) * jax.nn.silu(z)).reshape(b, l, DN_VW)
    y_a = o_a @ w_down_a
    q_b = _rotary(_heads(q_b, RET_HEADS), pos)
    k_b = _rotary(_heads(k_b, RET_HEADS), pos) * (RET_DK ** -0.5)
    v_b = _heads(v_b, RET_HEADS)
    o_b, s_ret_new = _retention(q_b, k_b, v_b, s_ret)
    o_b = o_b.transpose(0, 2, 1, 3)
    mu = jnp.mean(o_b, axis=-1, keepdims=True)
    var = jnp.mean(jnp.square(o_b - mu), axis=-1, keepdims=True)
    o_b = (((o_b - mu) * lax.rsqrt(var + EPS)).reshape(b, l, RET_VW) * ret_gn_w.astype(jnp.float32)
           + ret_gn_b.astype(jnp.float32))
    o_b = o_b.astype(h.dtype) * jax.nn.silu(g_b)
    y_b = o_b @ w_down_b
    merged = jax.nn.sigmoid(gate_a) * y_a + jax.nn.sigmoid(gate_b) * y_b
    return (merged @ w_out, conv_new, s_delta_new.astype(s_delta.dtype), s_ret_new.astype(s_ret.dtype))


def _moe(h, router_w, router_bias, w_gate, w_up, w_down, ws_gate, ws_up, ws_down):
    b, l, d = h.shape
    x = h.reshape(b * l, d)
    t = x.shape[0]
    scores = jax.nn.sigmoid(x.astype(jnp.float32) @ router_w.astype(jnp.float32).T)
    biased = scores + router_bias.astype(jnp.float32)
    grp_score = lax.top_k(biased.reshape(t, N_GROUPS, N_EXPERTS // N_GROUPS), 2)[0].sum(-1)
    _, top_g = lax.top_k(grp_score, TOPK_GROUPS)
    gmask = jnp.any(top_g[:, :, None] == jnp.arange(N_GROUPS)[None, None, :], axis=1)
    emask = jnp.repeat(gmask, N_EXPERTS // N_GROUPS, axis=-1)
    _, top_e = lax.top_k(jnp.where(emask, biased, -jnp.inf), TOP_K)
    wts = jnp.take_along_axis(scores, top_e, axis=-1)
    wts = wts / jnp.sum(wts, axis=-1, keepdims=True) * ROUTED_SCALE
    n = t * TOP_K
    flat_e = top_e.reshape(n)
    flat_t = jnp.arange(n, dtype=jnp.int32) // TOP_K
    flat_w = wts.reshape(n)
    order = jnp.argsort(flat_e)
    se, st, sw = flat_e[order], flat_t[order], flat_w[order]
    counts = jnp.bincount(flat_e, length=N_EXPERTS)
    padded = ((counts + MOE_BLOCK - 1) // MOE_BLOCK) * MOE_BLOCK
    pad_end = jnp.cumsum(padded)
    pad_start = pad_end - padded
    start = jnp.cumsum(counts) - counts
    dest = pad_start[se] + (jnp.arange(n) - start[se])
    n_blocks = -(-n // MOE_BLOCK) + N_EXPERTS
    p = n_blocks * MOE_BLOCK
    row_tok = jnp.full((p,), t, dtype=jnp.int32).at[dest].set(st)
    row_w = jnp.zeros((p,), jnp.float32).at[dest].set(sw)
    block_e = jnp.minimum(jnp.searchsorted(pad_end, jnp.arange(n_blocks) * MOE_BLOCK, side='right'),
                          N_EXPERTS - 1)
    x_pad = jnp.concatenate([x, jnp.zeros((1, d), x.dtype)], axis=0)
    xb = x_pad[row_tok].reshape(n_blocks, MOE_BLOCK, d)

    def expert_block(args):
        xi, e = args
        hid = jax.nn.silu(xi @ w_gate[e]) * (xi @ w_up[e])
        return hid @ w_down[e]

    yb = lax.map(expert_block, (xb, block_e)).reshape(p, d)
    routed = jax.ops.segment_sum(yb * row_w[:, None].astype(yb.dtype), row_tok, num_segments=t + 1)[:t]
    shared = (jax.nn.silu(x @ ws_gate) * (x @ ws_up)) @ ws_down
    return (routed + shared).reshape(b, l, d)


def _trunk(x, c, pos, conv_bufs, s_deltas, s_rets, params):
    (w_mod, b_mod, norm_mix_w, w_in, conv_w, a_log, dt_bias, dn_norm_w, ret_gn_w, ret_gn_b,
     w_down_a, w_down_b, w_out, norm_ffn_w, router_w, router_bias, w_gate, w_up, w_down,
     ws_gate, ws_up, ws_down, final_norm_w) = params
    new_conv, new_delta, new_ret = [], [], []
    for layer in range(DEPTH):
        mod = jax.nn.silu(c) @ w_mod[layer] + b_mod[layer]
        sh1, sc1, g1, sh2, sc2, g2 = jnp.split(mod[:, None, :], MOD_CHUNKS, axis=-1)
        hmix = _rms_norm(x, norm_mix_w[layer]) * (1.0 + sc1) + sh1
        mix, cb, sd, sr = _token_mixers(hmix, pos, conv_bufs[layer], s_deltas[layer], s_rets[layer],
                                        w_in[layer], conv_w[layer], a_log[layer], dt_bias[layer],
                                        dn_norm_w[layer], ret_gn_w[layer], ret_gn_b[layer],
                                        w_down_a[layer], w_down_b[layer], w_out[layer])
        x = x + g1 * mix
        hffn = _rms_norm(x, norm_ffn_w[layer]) * (1.0 + sc2) + sh2
        x = x + g2 * _moe(hffn, router_w[layer], router_bias[layer], w_gate[layer], w_up[layer],
                          w_down[layer], ws_gate[layer], ws_up[layer], ws_down[layer])
        new_conv.append(cb)
        new_delta.append(sd)
        new_ret.append(sr)
    return (_rms_norm(x, final_norm_w), jnp.stack(new_conv), jnp.stack(new_delta), jnp.stack(new_ret))


def setup_inputs(seed: int = 0) -> dict:
    key = jax.random.key(seed)
    ks = jax.random.split(key, 32)
    f32 = jnp.float32

    def nrm(k, shape, scale):
        return jax.random.normal(k, shape, f32) * scale

    dt = jnp.exp(jax.random.uniform(ks[12], (DEPTH, DN_HEADS), f32, math.log(1e-3), math.log(1e-1)))
    return {
        "x_prompt": nrm(ks[0], (BATCH, SEQ, D_MODEL), 1.0),
        "x_sample": nrm(ks[1], (DEC_BATCH, DEC_SEQ, D_MODEL), 1.0),
        "state_conv": nrm(ks[2], (DEPTH, DEC_BATCH, DN_CONV - 1, DN_CONV_W), 1.0),
        "state_delta": nrm(ks[3], (DEPTH, DEC_BATCH, DN_HEADS, DN_DK, DN_DV), 0.1),
        "state_ret": nrm(ks[4], (DEPTH, DEC_BATCH, RET_HEADS, RET_DK, RET_DV), 0.1),
        "c_prompt": nrm(ks[5], (BATCH, D_MODEL), 1.0),
        "c_sample": nrm(ks[6], (DEC_BATCH, D_MODEL), 1.0),
        "w_mod": nrm(ks[7], (DEPTH, D_MODEL, MOD_CHUNKS * D_MODEL), 0.2 * D_MODEL ** -0.5),
        "b_mod": nrm(ks[8], (DEPTH, MOD_CHUNKS * D_MODEL), 0.02),
        "norm_mix_w": 1.0 + nrm(ks[9], (DEPTH, D_MODEL), 0.02),
        "w_in": nrm(ks[10], (DEPTH, D_MODEL, D_IN_PROJ), D_MODEL ** -0.5),
        "conv_w": nrm(ks[11], (DEPTH, DN_CONV, DN_CONV_W), DN_CONV ** -0.5),
        "a_log": jnp.log(jax.random.uniform(ks[13], (DEPTH, DN_HEADS), f32, 1.0, 16.0)),
        "dt_bias": dt + jnp.log(-jnp.expm1(-dt)),
        "dn_norm_w": 1.0 + nrm(ks[14], (DEPTH, DN_DV), 0.02),
        "ret_gn_w": 1.0 + nrm(ks[15], (DEPTH, RET_VW), 0.02),
        "ret_gn_b": nrm(ks[16], (DEPTH, RET_VW), 0.02),
        "w_down_a": nrm(ks[17], (DEPTH, DN_VW, D_MODEL), DN_VW ** -0.5),
        "w_down_b": nrm(ks[18], (DEPTH, RET_VW, D_MODEL), RET_VW ** -0.5),
        "w_out": nrm(ks[19], (DEPTH, D_MODEL, D_MODEL), D_MODEL ** -0.5),
        "norm_ffn_w": 1.0 + nrm(ks[20], (DEPTH, D_MODEL), 0.02),
        "router_w": nrm(ks[21], (DEPTH, N_EXPERTS, D_MODEL), D_MODEL ** -0.5),
        "router_bias": nrm(ks[22], (DEPTH, N_EXPERTS), 0.01),
        "w_gate": nrm(ks[23], (DEPTH, N_EXPERTS, D_MODEL, D_EXPERT), D_MODEL ** -0.5),
        "w_up": nrm(ks[24], (DEPTH, N_EXPERTS, D_MODEL, D_EXPERT), D_MODEL ** -0.5),
        "w_down": nrm(ks[25], (DEPTH, N_EXPERTS, D_EXPERT, D_MODEL), D_EXPERT ** -0.5),
        "ws_gate": nrm(ks[26], (DEPTH, D_MODEL, D_SHARED), D_MODEL ** -0.5),
        "ws_up": nrm(ks[27], (DEPTH, D_MODEL, D_SHARED), D_MODEL ** -0.5),
        "ws_down": nrm(ks[28], (DEPTH, D_SHARED, D_MODEL), D_SHARED ** -0.5),
        "final_norm_w": 1.0 + nrm(ks[29], (D_MODEL,), 0.02),
    }


def reference(x_prompt, x_sample, state_conv, state_delta, state_ret, c_prompt, c_sample,
              w_mod, b_mod, norm_mix_w, w_in, conv_w, a_log, dt_bias, dn_norm_w, ret_gn_w, ret_gn_b,
              w_down_a, w_down_b, w_out, norm_ffn_w, router_w, router_bias, w_gate, w_up, w_down,
              ws_gate, ws_up, ws_down, final_norm_w):
    params = (w_mod, b_mod, norm_mix_w, w_in, conv_w, a_log, dt_bias, dn_norm_w, ret_gn_w, ret_gn_b,
              w_down_a, w_down_b, w_out, norm_ffn_w, router_w, router_bias, w_gate, w_up, w_down,
              ws_gate, ws_up, ws_down, final_norm_w)
    bp, lp = x_prompt.shape[0], x_prompt.shape[1]
    zero_conv = jnp.zeros((DEPTH, bp, DN_CONV - 1, DN_CONV_W), x_prompt.dtype)
    zero_delta = jnp.zeros((DEPTH, bp, DN_HEADS, DN_DK, DN_DV), state_delta.dtype)
    zero_ret = jnp.zeros((DEPTH, bp, RET_HEADS, RET_DK, RET_DV), state_ret.dtype)
    pos_prompt = jnp.arange(lp, dtype=jnp.int32)
    pos_sample = PAST_LEN + jnp.arange(x_sample.shape[1], dtype=jnp.int32)
    y_prompt, conv_p, delta_p, ret_p = _trunk(x_prompt, c_prompt, pos_prompt, zero_conv, zero_delta,
                                              zero_ret, params)
    y_sample, conv_s, delta_s, ret_s = _trunk(x_sample, c_sample, pos_sample, state_conv, state_delta,
                                              state_ret, params)
    return (y_prompt, y_sample, conv_p, delta_p, ret_p, conv_s, delta_s, ret_s)
```

```python
import functools
import math

import jax
import jax.numpy as jnp
from jax import lax
from jax.experimental import pallas as pl
from jax.experimental.pallas import tpu as pltpu

F32 = jnp.float32
BF16 = jnp.bfloat16

D_MODEL = 1024
DN_HEADS, DN_DK, DN_DV, DN_CONV = 8, 128, 128, 4
DN_QK = DN_HEADS * DN_DK
DN_VW = DN_HEADS * DN_DV
DN_CONV_W = 2 * DN_QK + DN_VW
RET_HEADS, RET_DK, RET_DV = 4, 256, 512
RET_QK = RET_HEADS * RET_DK
RET_VW = RET_HEADS * RET_DV
ROPE_BASE = 10000.0
PAST_LEN = 16384
N_EXPERTS, TOP_K, N_GROUPS, TOPK_GROUPS = 64, 8, 8, 4
GROUP_SIZE = N_EXPERTS // N_GROUPS
D_EXPERT, D_SHARED = 256, 256
ROUTED_SCALE = 2.5
MOD_CHUNKS = 6
EPS = 1e-6

PROJ_W = 12 * D_MODEL
COL_Q_A, COL_K_A, COL_V_A, COL_Z_A = 0, 1, 2, 3
COL_Q_B, COL_K_B = 4, 5
COL_V_B, COL_G_B = 3, 4
COL_GATE_A, COL_GATE_B = 10, 11
BA_W = 128
CHUNK = 128
CONV_HALO = 8
VMEM_LIMIT = 56 * 1024 * 1024

NN = (((1,), (0,)), ((), ()))
NT = (((1,), (1,)), ((), ()))
TN = (((0,), (0,)), ((), ()))


def _mm(a, b, dims=NN):
    return lax.dot_general(a.astype(BF16), b.astype(BF16), dims, preferred_element_type=F32)


def _mm_hi(a, b, dims=NN):
    return lax.dot_general(a.astype(F32), b.astype(F32), dims, precision=lax.Precision.HIGHEST,
                           preferred_element_type=F32)


def _sigmoid(x):
    return jax.nn.sigmoid(x)


def _silu(x):
    return x * _sigmoid(x)


def _softplus(x):
    return jnp.maximum(x, 0.0) + jnp.log1p(jnp.exp(-jnp.abs(x)))


def _rms(x, w):
    return x * lax.rsqrt(jnp.mean(x * x, axis=-1, keepdims=True) + EPS) * w


def _params(*sem):
    return pltpu.CompilerParams(dimension_semantics=sem, vmem_limit_bytes=VMEM_LIMIT)


def _mod_kernel(c_ref, w_ref, b_ref, o_ref):
    o_ref[...] = _mm_hi(_silu(c_ref[...]), w_ref[...]) + b_ref[...]


def _modulation(c, w_mod, b_mod):
    n = c.shape[0]
    tn = 1536
    return pl.pallas_call(
        _mod_kernel,
        out_shape=jax.ShapeDtypeStruct((n, MOD_CHUNKS * D_MODEL), F32),
        grid=(MOD_CHUNKS * D_MODEL // tn,),
        in_specs=[pl.BlockSpec((n, D_MODEL), lambda j: (0, 0)),
                  pl.BlockSpec((D_MODEL, tn), lambda j: (0, j)),
                  pl.BlockSpec((1, tn), lambda j: (0, j))],
        out_specs=pl.BlockSpec((n, tn), lambda j: (0, j)),
        compiler_params=_params("parallel"),
        name="modulation",
    )(c, w_mod, b_mod.reshape(1, -1))


def _mod_spec(per_token, tm, col):
    if per_token:
        return pl.BlockSpec((None, tm, D_MODEL), lambda b, i, *_: (b, i, col))
    return pl.BlockSpec((None, 1, D_MODEL), lambda b, i, *_: (b, 0, col))


def _inproj_kernel(x_ref, sh_ref, sc_ref, nw_ref, w_ref, wba_ref, wbat_ref, out_ref, ba_ref, bat_ref, h_scr):
    @pl.when(pl.program_id(2) == 0)
    def _():
        h = _rms(x_ref[...], nw_ref[...]) * (1.0 + sc_ref[...]) + sh_ref[...]
        hb = h.astype(BF16)
        h_scr[...] = hb
        ba_ref[...] = _mm(hb, wba_ref[...])
        bat_ref[...] = _mm(wbat_ref[...], hb, NT)

    out_ref[...] = _mm(h_scr[...], w_ref[...]).astype(BF16)


def _in_projection(x, mod, per_token, norm_w, w_main, w_ba, w_bat, tm, tn):
    nb, l, _ = x.shape
    return pl.pallas_call(
        _inproj_kernel,
        out_shape=(jax.ShapeDtypeStruct((nb, l, PROJ_W), BF16),
                   jax.ShapeDtypeStruct((nb, l, BA_W), F32),
                   jax.ShapeDtypeStruct((nb, 2 * DN_HEADS, l), F32)),
        grid=(nb, l // tm, PROJ_W // tn),
        in_specs=[pl.BlockSpec((None, tm, D_MODEL), lambda b, i, j: (b, i, 0)),
                  _mod_spec(per_token, tm, 0),
                  _mod_spec(per_token, tm, 1),
                  pl.BlockSpec((1, D_MODEL), lambda b, i, j: (0, 0)),
                  pl.BlockSpec((D_MODEL, tn), lambda b, i, j: (0, j)),
                  pl.BlockSpec((D_MODEL, BA_W), lambda b, i, j: (0, 0)),
                  pl.BlockSpec((2 * DN_HEADS, D_MODEL), lambda b, i, j: (0, 0))],
        out_specs=(pl.BlockSpec((None, tm, tn), lambda b, i, j: (b, i, j)),
                   pl.BlockSpec((None, tm, BA_W), lambda b, i, j: (b, i, 0)),
                   pl.BlockSpec((None, 2 * DN_HEADS, tm), lambda b, i, j: (b, 0, i))),
        scratch_shapes=[pltpu.VMEM((tm, D_MODEL), BF16)],
        compiler_params=_params("parallel", "parallel", "arbitrary"),
        name="in_projection",
    )(x, mod, mod, norm_w, w_main, w_ba, w_bat)


def _delta_prefill_kernel(q_ref, k_ref, v_ref, z_ref, ba_ref, bat_ref, cw_ref, alog_l_ref, dtb_l_ref,
                          alog_c_ref, dtb_c_ref, nw_ref, s0_ref, c0_ref,
                          o_ref, sout_ref, cout_ref, ext_scr, s_scr):
    step = pl.program_id(1)
    c = CHUNK
    lo = CONV_HALO - (DN_CONV - 1)

    @pl.when(step == 0)
    def _():
        s_scr[...] = s0_ref[...]
        ext_scr[lo:CONV_HALO, :] = c0_ref[...]

    ext_scr[CONV_HALO:, 0:DN_QK] = q_ref[...].astype(F32)
    ext_scr[CONV_HALO:, DN_QK:2 * DN_QK] = k_ref[...].astype(F32)
    ext_scr[CONV_HALO:, 2 * DN_QK:] = v_ref[...].astype(F32)

    row = lax.broadcasted_iota(jnp.int32, (c, c), 0)
    col = lax.broadcasted_iota(jnp.int32, (c, c), 1)
    causal = row >= col
    strict = row > col
    eye = (row == col).astype(F32)
    ltri = causal.astype(F32)
    utri = (row <= col).astype(F32)

    ba = ba_ref[...]
    beta_tok = _sigmoid(ba)
    g_tok = -jnp.exp(alog_l_ref[...]) * _softplus(ba + dtb_l_ref[...])
    gc_tok = _mm_hi(ltri, g_tok)
    bat = bat_ref[...]
    g_t = -jnp.exp(alog_c_ref[...]) * _softplus(bat[DN_HEADS:, :] + dtb_c_ref[...])
    gc_t = _mm_hi(g_t, utri)

    def conv(c0):
        acc = ext_scr[lo:lo + c, c0:c0 + DN_DK] * cw_ref[0:1, c0:c0 + DN_DK]
        for i in range(1, DN_CONV):
            acc = acc + ext_scr[lo + i:lo + i + c, c0:c0 + DN_DK] * cw_ref[i:i + 1, c0:c0 + DN_DK]
        return _silu(acc)

    for h in range(DN_HEADS):
        q = conv(h * DN_DK)
        k = conv(DN_QK + h * DN_DK)
        v = conv(2 * DN_QK + h * DN_DV)
        q = q * lax.rsqrt(jnp.sum(q * q, axis=-1, keepdims=True) + EPS) * (DN_DK ** -0.5)
        k = k * lax.rsqrt(jnp.sum(k * k, axis=-1, keepdims=True) + EPS)
        beta = beta_tok[:, h:h + 1]
        gc = gc_tok[:, DN_HEADS + h:DN_HEADS + h + 1]
        gc_row = gc_t[h:h + 1, :]
        decay = jnp.where(causal, jnp.exp(jnp.where(causal, gc - gc_row, 0.0)), 0.0)
        kb = k * beta
        a_mat = jnp.where(strict, _mm(kb, k, NT) * decay, 0.0)
        inv = eye - a_mat
        a_pow = a_mat
        for _ in range(int(math.log2(c)) - 1):
            a_pow = _mm_hi(a_pow, a_pow)
            inv = inv + _mm_hi(inv, a_pow)
        egc = jnp.exp(gc)
        sol = _mm_hi(inv, jnp.concatenate([v * beta, kb * egc], axis=-1))
        u, w = sol[:, :DN_DV], sol[:, DN_DV:]
        qk = jnp.where(causal, _mm(q, k, NT) * decay, 0.0)
        g_last = gc[c - 1:c, :]
        k_end = k * jnp.exp(g_last - gc)
        s = s_scr[h]
        v_new = u - _mm(w, s)
        o = _mm(q * egc, s) + _mm(qk, v_new)
        s_scr[h] = s * jnp.exp(g_last) + _mm(k_end, v_new, TN)
        z = z_ref[:, h * DN_DV:(h + 1) * DN_DV].astype(F32)
        o_ref[:, h * DN_DV:(h + 1) * DN_DV] = (_rms(o, nw_ref[...]) * _silu(z)).astype(BF16)

    ext_scr[0:CONV_HALO, :] = ext_scr[c:c + CONV_HALO, :]

    @pl.when(step == pl.num_programs(1) - 1)
    def _():
        sout_ref[...] = s_scr[...]
        cout_ref[...] = ext_scr[lo:CONV_HALO, :]


def _lane_pad(v, offset):
    return jnp.zeros((1, BA_W), F32).at[0, offset:offset + v.shape[0]].set(v.astype(F32))


def _delta_prefill(proj, ba, bat, conv_w, a_log, dt_bias, dn_norm_w, s0, c0):
    nb, l, _ = proj.shape
    c = CHUNK
    col_spec = lambda j: pl.BlockSpec((None, c, D_MODEL), lambda b, i: (b, i, j))
    full = lambda shape: pl.BlockSpec(shape, lambda b, i: (0,) * len(shape))
    return pl.pallas_call(
        _delta_prefill_kernel,
        out_shape=(jax.ShapeDtypeStruct((nb, l, DN_VW), BF16),
                   jax.ShapeDtypeStruct((nb, DN_HEADS, DN_DK, DN_DV), F32),
                   jax.ShapeDtypeStruct((nb, DN_CONV - 1, DN_CONV_W), F32)),
        grid=(nb, l // c),
        in_specs=[col_spec(COL_Q_A), col_spec(COL_K_A), col_spec(COL_V_A), col_spec(COL_Z_A),
                  pl.BlockSpec((None, c, BA_W), lambda b, i: (b, i, 0)),
                  pl.BlockSpec((None, 2 * DN_HEADS, c), lambda b, i: (b, 0, i)),
                  full((DN_CONV, DN_CONV_W)), full((1, BA_W)), full((1, BA_W)),
                  full((DN_HEADS, 1)), full((DN_HEADS, 1)), full((1, DN_DV)),
                  pl.BlockSpec((None, DN_HEADS, DN_DK, DN_DV), lambda b, i: (b, 0, 0, 0)),
                  pl.BlockSpec((None, DN_CONV - 1, DN_CONV_W), lambda b, i: (b, 0, 0))],
        out_specs=(pl.BlockSpec((None, c, DN_VW), lambda b, i: (b, i, 0)),
                   pl.BlockSpec((None, DN_HEADS, DN_DK, DN_DV), lambda b, i: (b, 0, 0, 0)),
                   pl.BlockSpec((None, DN_CONV - 1, DN_CONV_W), lambda b, i: (b, 0, 0))),
        scratch_shapes=[pltpu.VMEM((c + CONV_HALO, DN_CONV_W), F32),
                        pltpu.VMEM((DN_HEADS, DN_DK, DN_DV), F32)],
        compiler_params=_params("parallel", "arbitrary"),
        name="delta_prefill",
    )(proj, proj, proj, proj, ba, bat, conv_w, _lane_pad(a_log, DN_HEADS), _lane_pad(dt_bias, DN_HEADS),
      a_log.reshape(DN_HEADS, 1), dt_bias.reshape(DN_HEADS, 1), dn_norm_w.reshape(1, DN_DV), s0, c0)


def _log_gamma(h):
    return math.log(1.0 - 2.0 ** (-5.0 - h))


def _rotate(x, cos, sin):
    half = x.shape[-1] // 2
    x1, x2 = x[:, :half], x[:, half:]
    return jnp.concatenate([x1 * cos - x2 * sin, x1 * sin + x2 * cos], axis=-1)


def _group_norm_gate(o, gw, gb, gate):
    mu = jnp.mean(o, axis=-1, keepdims=True)
    var = jnp.mean(jnp.square(o - mu), axis=-1, keepdims=True)
    return ((o - mu) * lax.rsqrt(var + EPS) * gw + gb) * _silu(gate)


def _ret_prefill_kernel(q_ref, k_ref, v_ref, g_ref, inv_ref, gw_ref, gb_ref, s0_ref, o_ref, sout_ref, s_scr):
    step = pl.program_id(1)
    c = CHUNK

    @pl.when(step == 0)
    def _():
        s_scr[...] = s0_ref[...]

    idx_c = lax.broadcasted_iota(jnp.int32, (c, 1), 0)
    pos = (step * c + idx_c).astype(F32)
    ang = pos * inv_ref[...]
    cos, sin = jnp.cos(ang), jnp.sin(ang)
    rel = (lax.broadcasted_iota(jnp.int32, (c, c), 0) - lax.broadcasted_iota(jnp.int32, (c, c), 1)).astype(F32)
    idx = idx_c.astype(F32)

    for h in range(RET_HEADS):
        lg = _log_gamma(h)
        q = _rotate(q_ref[:, h * RET_DK:(h + 1) * RET_DK].astype(F32), cos, sin)
        k = _rotate(k_ref[:, h * RET_DK:(h + 1) * RET_DK].astype(F32), cos, sin) * (RET_DK ** -0.5)
        v = v_ref[:, h * RET_DV:(h + 1) * RET_DV]
        d_mat = jnp.where(rel >= 0, jnp.exp(jnp.maximum(rel, 0.0) * lg), 0.0)
        intra = _mm(_mm(q, k, NT) * d_mat, v)
        s = s_scr[h]
        cross = _mm(q * jnp.exp((idx + 1.0) * lg), s)
        s_scr[h] = s * math.exp(c * lg) + _mm(k * jnp.exp((c - 1.0 - idx) * lg), v, TN)
        sl = slice(h * RET_DV, (h + 1) * RET_DV)
        o_ref[:, sl] = _group_norm_gate(intra + cross, gw_ref[:, sl], gb_ref[:, sl],
                                        g_ref[:, sl].astype(F32)).astype(BF16)

    @pl.when(step == pl.num_programs(1) - 1)
    def _():
        sout_ref[...] = s_scr[...]


def _ret_prefill(proj, inv_freq, gn_w, gn_b, s0):
    nb, l, _ = proj.shape
    c = CHUNK
    full = lambda shape: pl.BlockSpec(shape, lambda b, i: (0,) * len(shape))
    state_spec = pl.BlockSpec((None, RET_HEADS, RET_DK, RET_DV), lambda b, i: (b, 0, 0, 0))
    return pl.pallas_call(
        _ret_prefill_kernel,
        out_shape=(jax.ShapeDtypeStruct((nb, l, RET_VW), BF16),
                   jax.ShapeDtypeStruct((nb, RET_HEADS, RET_DK, RET_DV), F32)),
        grid=(nb, l // c),
        in_specs=[pl.BlockSpec((None, c, RET_QK), lambda b, i: (b, i, COL_Q_B)),
                  pl.BlockSpec((None, c, RET_QK), lambda b, i: (b, i, COL_K_B)),
                  pl.BlockSpec((None, c, RET_VW), lambda b, i: (b, i, COL_V_B)),
                  pl.BlockSpec((None, c, RET_VW), lambda b, i: (b, i, COL_G_B)),
                  full((1, RET_DK // 2)), full((1, RET_VW)), full((1, RET_VW)), state_spec],
        out_specs=(pl.BlockSpec((None, c, RET_VW), lambda b, i: (b, i, 0)), state_spec),
        scratch_shapes=[pltpu.VMEM((RET_HEADS, RET_DK, RET_DV), F32)],
        compiler_params=_params("parallel", "arbitrary"),
        name="retention_prefill",
    )(proj, proj, proj, proj, inv_freq, gn_w.reshape(1, RET_VW), gn_b.reshape(1, RET_VW), s0)


ROWS = 8


def _row_mm(a, s):
    return _mm(jnp.broadcast_to(a, (ROWS, a.shape[-1])), s)[0:1, :]


def _outer(a, b):
    first = lax.broadcasted_iota(jnp.int32, (ROWS, a.shape[-1]), 0) == 0
    a8 = jnp.where(first, jnp.broadcast_to(a, (ROWS, a.shape[-1])), 0.0)
    return _mm(a8, jnp.broadcast_to(b, (ROWS, b.shape[-1])), TN)


def _delta_step_kernel(qkv_ref, z_ref, ba_ref, cw_ref, alog_ref, dtb_ref, nw_ref, s_ref, c_ref,
                       o_ref, sout_ref, cout_ref):
    u_new = qkv_ref[...].astype(F32)
    buf = c_ref[...]
    acc = u_new * cw_ref[DN_CONV - 1:DN_CONV, :]
    for i in range(DN_CONV - 1):
        acc = acc + buf[i:i + 1, :] * cw_ref[i:i + 1, :]
    qkv = _silu(acc)
    cout_ref[0:DN_CONV - 2, :] = buf[1:, :]
    cout_ref[DN_CONV - 2:, :] = u_new
    ba = ba_ref[...]
    beta_all = _sigmoid(ba)
    g_all = -jnp.exp(alog_ref[...]) * _softplus(ba + dtb_ref[...])
    for h in range(DN_HEADS):
        q = qkv[:, h * DN_DK:(h + 1) * DN_DK]
        k = qkv[:, DN_QK + h * DN_DK:DN_QK + (h + 1) * DN_DK]
        v = qkv[:, 2 * DN_QK + h * DN_DV:2 * DN_QK + (h + 1) * DN_DV]
        q = q * lax.rsqrt(jnp.sum(q * q, axis=-1, keepdims=True) + EPS) * (DN_DK ** -0.5)
        k = k * lax.rsqrt(jnp.sum(k * k, axis=-1, keepdims=True) + EPS)
        beta = beta_all[:, h:h + 1]
        eg = jnp.exp(g_all[:, DN_HEADS + h:DN_HEADS + h + 1])
        s = s_ref[h]
        kb = k * beta
        v_new = v * beta - _row_mm(kb * eg, s)
        o = _row_mm(q * eg, s) + jnp.sum(q * k, axis=-1, keepdims=True) * v_new
        sout_ref[h] = s * eg + _outer(k, v_new)
        z = z_ref[:, h * DN_DV:(h + 1) * DN_DV].astype(F32)
        o_ref[:, h * DN_DV:(h + 1) * DN_DV] = (_rms(o, nw_ref[...]) * _silu(z)).astype(BF16)


def _delta_step(proj, ba, conv_w, a_log, dt_bias, dn_norm_w, s0, c0):
    n = proj.shape[0]
    full = lambda shape: pl.BlockSpec(shape, lambda b: (0,) * len(shape))
    state_spec = pl.BlockSpec((None, DN_HEADS, DN_DK, DN_DV), lambda b: (b, 0, 0, 0))
    conv_spec = pl.BlockSpec((None, DN_CONV - 1, DN_CONV_W), lambda b: (b, 0, 0))
    return pl.pallas_call(
        _delta_step_kernel,
        out_shape=(jax.ShapeDtypeStruct((n, 1, DN_VW), BF16),
                   jax.ShapeDtypeStruct((n, DN_HEADS, DN_DK, DN_DV), F32),
                   jax.ShapeDtypeStruct((n, DN_CONV - 1, DN_CONV_W), F32)),
        grid=(n,),
        in_specs=[pl.BlockSpec((None, 1, DN_CONV_W), lambda b: (b, 0, 0)),
                  pl.BlockSpec((None, 1, DN_VW), lambda b: (b, 0, COL_Z_A)),
                  pl.BlockSpec((None, 1, BA_W), lambda b: (b, 0, 0)),
                  full((DN_CONV, DN_CONV_W)), full((1, BA_W)), full((1, BA_W)), full((1, DN_DV)),
                  state_spec, conv_spec],
        out_specs=(pl.BlockSpec((None, 1, DN_VW), lambda b: (b, 0, 0)), state_spec, conv_spec),
        compiler_params=_params("parallel"),
        name="delta_step",
    )(proj, proj, ba, conv_w, _lane_pad(a_log, DN_HEADS), _lane_pad(dt_bias, DN_HEADS),
      dn_norm_w.reshape(1, DN_DV), s0, c0)


def _ret_step_kernel(q_ref, k_ref, v_ref, g_ref, inv_ref, gw_ref, gb_ref, s_ref, o_ref, sout_ref):
    ang = float(PAST_LEN) * inv_ref[...]
    cos, sin = jnp.cos(ang), jnp.sin(ang)
    for h in range(RET_HEADS):
        gamma = math.exp(_log_gamma(h))
        q = _rotate(q_ref[:, h * RET_DK:(h + 1) * RET_DK].astype(F32), cos, sin)
        k = _rotate(k_ref[:, h * RET_DK:(h + 1) * RET_DK].astype(F32), cos, sin) * (RET_DK ** -0.5)
        v = v_ref[:, h * RET_DV:(h + 1) * RET_DV].astype(F32)
        s = s_ref[h]
        o = jnp.sum(q * k, axis=-1, keepdims=True) * v + _row_mm(q * gamma, s)
        sout_ref[h] = s * gamma + _outer(k, v)
        sl = slice(h * RET_DV, (h + 1) * RET_DV)
        o_ref[:, sl] = _group_norm_gate(o, gw_ref[:, sl], gb_ref[:, sl], g_ref[:, sl].astype(F32)).astype(BF16)


def _ret_step(proj, inv_freq, gn_w, gn_b, s0):
    n = proj.shape[0]
    full = lambda shape: pl.BlockSpec(shape, lambda b: (0,) * len(shape))
    state_spec = pl.BlockSpec((None, RET_HEADS, RET_DK, RET_DV), lambda b: (b, 0, 0, 0))
    return pl.pallas_call(
        _ret_step_kernel,
        out_shape=(jax.ShapeDtypeStruct((n, 1, RET_VW), BF16),
                   jax.ShapeDtypeStruct((n, RET_HEADS, RET_DK, RET_DV), F32)),
        grid=(n,),
        in_specs=[pl.BlockSpec((None, 1, RET_QK), lambda b: (b, 0, COL_Q_B)),
                  pl.BlockSpec((None, 1, RET_QK), lambda b: (b, 0, COL_K_B)),
                  pl.BlockSpec((None, 1, RET_VW), lambda b: (b, 0, COL_V_B)),
                  pl.BlockSpec((None, 1, RET_VW), lambda b: (b, 0, COL_G_B)),
                  full((1, RET_DK // 2)), full((1, RET_VW)), full((1, RET_VW)), state_spec],
        out_specs=(pl.BlockSpec((None, 1, RET_VW), lambda b: (b, 0, 0)), state_spec),
        compiler_params=_params("parallel"),
        name="retention_step",
    )(proj, proj, proj, proj, inv_freq, gn_w.reshape(1, RET_VW), gn_b.reshape(1, RET_VW), s0)


def _post_mixer_kernel(oa_ref, ob_ref, ga_ref, gb_ref, x_ref, g1_ref, sh2_ref, sc2_ref, wda_ref, wdb_ref,
                       wout_ref, nw_ref, rw_ref, x1_ref, h_ref, logit_ref):
    y_a = _mm(oa_ref[...], wda_ref[...])
    y_b = _mm(ob_ref[...], wdb_ref[...])
    merged = _sigmoid(ga_ref[...].astype(F32)) * y_a + _sigmoid(gb_ref[...].astype(F32)) * y_b
    x1 = x_ref[...] + g1_ref[...] * _mm(merged, wout_ref[...])
    x1_ref[...] = x1
    hf = _rms(x1, nw_ref[...]) * (1.0 + sc2_ref[...]) + sh2_ref[...]
    h_ref[...] = hf.astype(BF16)
    logit_ref[...] = _mm_hi(rw_ref[...], hf, NT)


def _post_mixer(o_a, o_b, proj, x, mod, per_token, w_down_a, w_down_b, w_out, norm_w, router_w, tm):
    nb, l, _ = x.shape
    tok = lambda w, j=0: pl.BlockSpec((None, tm, w), lambda b, i: (b, i, j))
    full = lambda shape: pl.BlockSpec(shape, lambda b, i: (0,) * len(shape))
    return pl.pallas_call(
        _post_mixer_kernel,
        out_shape=(jax.ShapeDtypeStruct((nb, l, D_MODEL), F32),
                   jax.ShapeDtypeStruct((nb, l, D_MODEL), BF16),
                   jax.ShapeDtypeStruct((nb, N_EXPERTS, l), F32)),
        grid=(nb, l // tm),
        in_specs=[tok(DN_VW), tok(RET_VW), tok(D_MODEL, COL_GATE_A), tok(D_MODEL, COL_GATE_B), tok(D_MODEL),
                  _mod_spec(per_token, tm, 2), _mod_spec(per_token, tm, 3), _mod_spec(per_token, tm, 4),
                  full((DN_VW, D_MODEL)), full((RET_VW, D_MODEL)), full((D_MODEL, D_MODEL)),
                  full((1, D_MODEL)), full((N_EXPERTS, D_MODEL))],
        out_specs=(tok(D_MODEL), tok(D_MODEL), pl.BlockSpec((None, N_EXPERTS, tm), lambda b, i: (b, 0, i))),
        compiler_params=_params("parallel", "parallel"),
        name="post_mixer",
    )(o_a, o_b, proj, proj, x, mod, mod, mod, w_down_a, w_down_b, w_out, norm_w, router_w)


def _first_max(x, axis, n):
    m = jnp.max(x, axis=axis, keepdims=True)
    ids = lax.broadcasted_iota(jnp.int32, x.shape, axis)
    first = jnp.min(jnp.where(x == m, ids, n), axis=axis, keepdims=True)
    return m, ids == first


def _route_kernel(logit_ref, bias_ref, gw_ref, gwt_ref):
    t = logit_ref.shape[-1]
    scores = _sigmoid(logit_ref[...])
    biased = scores + bias_ref[...]
    neg = jnp.float32(-jnp.inf)
    b3 = biased.reshape(N_GROUPS, GROUP_SIZE, t)
    m1, hit = _first_max(b3, 1, GROUP_SIZE)
    m2 = jnp.max(jnp.where(hit, neg, b3), axis=1, keepdims=True)
    gscore = (m1 + m2).reshape(N_GROUPS, t)
    gmask = jnp.zeros((N_GROUPS, t), jnp.bool_)
    for _ in range(TOPK_GROUPS):
        _, hit = _first_max(gscore, 0, N_GROUPS)
        gmask = jnp.logical_or(gmask, hit)
        gscore = jnp.where(hit, neg, gscore)
    emask = jnp.broadcast_to(gmask.reshape(N_GROUPS, 1, t), (N_GROUPS, GROUP_SIZE, t)).reshape(N_EXPERTS, t)
    masked = jnp.where(emask, biased, neg)
    sel = jnp.zeros((N_EXPERTS, t), jnp.bool_)
    for _ in range(TOP_K):
        _, hit = _first_max(masked, 0, N_EXPERTS)
        sel = jnp.logical_or(sel, hit)
        masked = jnp.where(hit, neg, masked)
    wts = jnp.where(sel, scores, 0.0)
    gw = wts / jnp.sum(wts, axis=0, keepdims=True) * ROUTED_SCALE
    gw_ref[...] = gw
    r = lax.broadcasted_iota(jnp.int32, (N_EXPERTS, N_EXPERTS), 0)
    c = lax.broadcasted_iota(jnp.int32, (N_EXPERTS, N_EXPERTS), 1)
    gwt_ref[...] = _mm_hi(gw, (r == c).astype(F32), TN)


def _route(logits_t, router_bias, tr):
    nb, _, l = logits_t.shape
    return pl.pallas_call(
        _route_kernel,
        out_shape=(jax.ShapeDtypeStruct((nb, N_EXPERTS, l), F32),
                   jax.ShapeDtypeStruct((nb, l, N_EXPERTS), F32)),
        grid=(nb, l // tr),
        in_specs=[pl.BlockSpec((None, N_EXPERTS, tr), lambda b, i: (b, 0, i)),
                  pl.BlockSpec((N_EXPERTS, 1), lambda b, i: (0, 0))],
        out_specs=(pl.BlockSpec((None, N_EXPERTS, tr), lambda b, i: (b, 0, i)),
                   pl.BlockSpec((None, tr, N_EXPERTS), lambda b, i: (b, i, 0))),
        compiler_params=_params("parallel", "parallel"),
        name="route",
    )(logits_t, router_bias.reshape(N_EXPERTS, 1))


def _moe_kernel(h_ref, gwt_ref, x1_ref, g2_ref, wg_ref, wu_ref, wd_ref, wsg_ref, wsu_ref, wsd_ref, fw_ref,
                y_ref, acc_scr):
    e = pl.program_id(2)
    hb = h_ref[...]

    @pl.when(e == 0)
    def _():
        acc_scr[...] = _mm(_silu(_mm(hb, wsg_ref[...])) * _mm(hb, wsu_ref[...]), wsd_ref[...])

    onehot = (lax.broadcasted_iota(jnp.int32, (N_EXPERTS, D_EXPERT), 0) == e).astype(F32)
    gate = _mm_hi(gwt_ref[...], onehot)
    hid = _silu(_mm(hb, wg_ref[...])) * _mm(hb, wu_ref[...])
    acc_scr[...] += _mm(hid, wd_ref[...]) * jnp.concatenate([gate] * (D_MODEL // D_EXPERT), axis=-1)

    @pl.when(e == pl.num_programs(2) - 1)
    def _():
        y_ref[...] = _rms(x1_ref[...] + g2_ref[...] * acc_scr[...], fw_ref[...])


def _moe(hffn, gwt, x1, mod, per_token, w_gate, w_up, w_down, ws_gate, ws_up, ws_down, final_w, tm):
    nb, l, _ = x1.shape
    tok = lambda w: pl.BlockSpec((None, tm, w), lambda b, i, e: (b, i, 0))
    full = lambda shape: pl.BlockSpec(shape, lambda b, i, e: (0,) * len(shape))
    return pl.pallas_call(
        _moe_kernel,
        out_shape=jax.ShapeDtypeStruct((nb, l, D_MODEL), F32),
        grid=(nb, l // tm, N_EXPERTS),
        in_specs=[tok(D_MODEL), tok(N_EXPERTS), tok(D_MODEL), _mod_spec(per_token, tm, 5),
                  pl.BlockSpec((None, D_MODEL, D_EXPERT), lambda b, i, e: (e, 0, 0)),
                  pl.BlockSpec((None, D_MODEL, D_EXPERT), lambda b, i, e: (e, 0, 0)),
                  pl.BlockSpec((None, D_EXPERT, D_MODEL), lambda b, i, e: (e, 0, 0)),
                  full((D_MODEL, D_SHARED)), full((D_MODEL, D_SHARED)), full((D_SHARED, D_MODEL)),
                  full((1, D_MODEL))],
        out_specs=tok(D_MODEL),
        scratch_shapes=[pltpu.VMEM((tm, D_MODEL), F32)],
        compiler_params=_params("parallel", "parallel", "arbitrary"),
        name="moe",
    )(hffn, gwt, x1, mod, w_gate, w_up, w_down, ws_gate, ws_up, ws_down, final_w)


def _trunk(x, mod, per_token, states, w):
    nb, l, _ = x.shape
    tm_proj, tn_proj, tm_post, tr, tm_moe = min(l, 1024), 1024, min(l, 512), min(l, 512), min(l, 1024)
    proj, ba, bat = _in_projection(x, mod, per_token, w["norm_mix"], w["w_main"], w["w_ba"], w["w_bat"],
                                   tm_proj, tn_proj)
    if states is None:
        c0 = jnp.zeros((nb, DN_CONV - 1, DN_CONV_W), F32)
        sd0 = jnp.zeros((nb, DN_HEADS, DN_DK, DN_DV), F32)
        sr0 = jnp.zeros((nb, RET_HEADS, RET_DK, RET_DV), F32)
        o_a, sd, cb = _delta_prefill(proj, ba, bat, w["conv_w"], w["a_log"], w["dt_bias"], w["dn_norm"], sd0, c0)
        o_b, sr = _ret_prefill(proj, w["inv_freq"], w["gn_w"], w["gn_b"], sr0)
    else:
        c0, sd0, sr0 = states
        n = nb * l
        o_a, sd, cb = _delta_step(proj.reshape(n, 1, PROJ_W), ba.reshape(n, 1, BA_W), w["conv_w"], w["a_log"],
                                  w["dt_bias"], w["dn_norm"], sd0, c0)
        o_b, sr = _ret_step(proj.reshape(n, 1, PROJ_W), w["inv_freq"], w["gn_w"], w["gn_b"], sr0)
        o_a = o_a.reshape(nb, l, DN_VW)
        o_b = o_b.reshape(nb, l, RET_VW)
    x1, hffn, logits_t = _post_mixer(o_a, o_b, proj, x, mod, per_token, w["w_down_a"], w["w_down_b"], w["w_out"],
                                     w["norm_ffn"], w["router_w"], tm_post)
    _, gwt = _route(logits_t, w["router_bias"], tr)
    y = _moe(hffn, gwt, x1, mod, per_token, w["w_gate"], w["w_up"], w["w_down"], w["ws_gate"], w["ws_up"],
             w["ws_down"], w["final_norm"], tm_moe)
    return y, cb, sd, sr


def kernel(x_prompt, x_sample, state_conv, state_delta, state_ret, c_prompt, c_sample, w_mod, b_mod, norm_mix_w, w_in, conv_w, a_log, dt_bias, dn_norm_w, ret_gn_w, ret_gn_b, w_down_a, w_down_b, w_out, norm_ffn_w, router_w, router_bias, w_gate, w_up, w_down, ws_gate, ws_up, ws_down, final_norm_w):
    bp, lp, _ = x_prompt.shape
    bs = x_sample.shape[0]
    half = RET_DK // 2
    w_in0 = w_in[0]
    c0, c1 = DN_CONV_W, DN_CONV_W + 2 * DN_HEADS
    w_ba = jnp.pad(w_in0[:, c0:c1], ((0, 0), (0, BA_W - 2 * DN_HEADS))).astype(BF16)
    w = {
        "norm_mix": norm_mix_w[0].reshape(1, D_MODEL),
        "w_main": jnp.concatenate([w_in0[:, :c0], w_in0[:, c1:]], axis=1).astype(BF16),
        "w_ba": w_ba,
        "w_bat": w_in0[:, c0:c1].T.astype(BF16),
        "conv_w": conv_w[0], "a_log": a_log[0], "dt_bias": dt_bias[0], "dn_norm": dn_norm_w[0],
        "inv_freq": (ROPE_BASE ** (-jnp.arange(half, dtype=F32) / half)).reshape(1, half),
        "gn_w": ret_gn_w[0], "gn_b": ret_gn_b[0],
        "w_down_a": w_down_a[0].astype(BF16), "w_down_b": w_down_b[0].astype(BF16), "w_out": w_out[0].astype(BF16),
        "norm_ffn": norm_ffn_w[0].reshape(1, D_MODEL), "router_w": router_w[0], "router_bias": router_bias[0],
        "w_gate": w_gate[0].astype(BF16), "w_up": w_up[0].astype(BF16), "w_down": w_down[0].astype(BF16),
        "ws_gate": ws_gate[0].astype(BF16), "ws_up": ws_up[0].astype(BF16), "ws_down": ws_down[0].astype(BF16),
        "final_norm": final_norm_w.reshape(1, D_MODEL),
    }
    mod = _modulation(jnp.concatenate([c_prompt, c_sample], axis=0), w_mod[0], b_mod[0])
    mod_p = mod[:bp].reshape(bp, 1, MOD_CHUNKS * D_MODEL)
    mod_s = mod[bp:].reshape(1, bs, MOD_CHUNKS * D_MODEL)

    y_p, conv_p, delta_p, ret_p = _trunk(x_prompt, mod_p, False, None, w)
    y_s, conv_s, delta_s, ret_s = _trunk(x_sample.reshape(1, bs, D_MODEL), mod_s, True,
                                         (state_conv[0], state_delta[0], state_ret[0]), w)
    return (y_p, y_s.reshape(bs, 1, D_MODEL), conv_p[None], delta_p[None], ret_p[None],
            conv_s[None], delta_s[None], ret_s[None])
```

```python
import functools
import math

import jax
import jax.numpy as jnp
from jax import lax
from jax.experimental import pallas as pl
from jax.experimental.pallas import tpu as pltpu

F32 = jnp.float32
BF16 = jnp.bfloat16

D_MODEL = 1024
DN_HEADS, DN_DK, DN_DV, DN_CONV = 8, 128, 128, 4
DN_QK = DN_HEADS * DN_DK
DN_VW = DN_HEADS * DN_DV
DN_CONV_W = 2 * DN_QK + DN_VW
RET_HEADS, RET_DK, RET_DV = 4, 256, 512
RET_QK = RET_HEADS * RET_DK
RET_VW = RET_HEADS * RET_DV
ROPE_BASE = 10000.0
PAST_LEN = 16384
N_EXPERTS, TOP_K, N_GROUPS, TOPK_GROUPS = 64, 8, 8, 4
GROUP_SIZE = N_EXPERTS // N_GROUPS
D_EXPERT, D_SHARED = 256, 256
ROUTED_SCALE = 2.5
MOD_CHUNKS = 6
EPS = 1e-6

PROJ_W = 12 * D_MODEL
COL_Q_A, COL_K_A, COL_V_A, COL_Z_A = 0, 1, 2, 3
COL_Q_B, COL_K_B = 4, 5
COL_V_B, COL_G_B = 3, 4
COL_GATE_A, COL_GATE_B = 10, 11
BA_W = 128
CHUNK = 128
CONV_HALO = 8
VMEM_LIMIT = 56 * 1024 * 1024

NN = (((1,), (0,)), ((), ()))
NT = (((1,), (1,)), ((), ()))
TN = (((0,), (0,)), ((), ()))


def _mm(a, b, dims=NN):
    return lax.dot_general(a.astype(BF16), b.astype(BF16), dims, preferred_element_type=F32)


def _mm_hi(a, b, dims=NN):
    return lax.dot_general(a.astype(F32), b.astype(F32), dims, precision=lax.Precision.HIGHEST,
                           preferred_element_type=F32)


def _hi_lo(x):
    hi = x.astype(BF16)
    return hi, (x - hi.astype(F32)).astype(BF16)


def _split_lhs(a):
    hi, lo = _hi_lo(a)
    return jnp.concatenate([hi, lo, hi], axis=1)


def _split_rhs(b):
    hi, lo = _hi_lo(b)
    return jnp.concatenate([hi, hi, lo], axis=0)


def _mm_split(a3, b3):
    return lax.dot_general(a3, b3, NN, preferred_element_type=F32)


def _sigmoid(x):
    return jax.nn.sigmoid(x)


def _silu(x):
    return x * _sigmoid(x)


def _softplus(x):
    return jnp.maximum(x, 0.0) + jnp.log1p(jnp.exp(-jnp.abs(x)))


def _rms(x, w):
    return x * lax.rsqrt(jnp.mean(x * x, axis=-1, keepdims=True) + EPS) * w


def _params(*sem):
    return pltpu.CompilerParams(dimension_semantics=sem, vmem_limit_bytes=VMEM_LIMIT)


def _mod_kernel(c_ref, w_ref, b_ref, o_ref):
    o_ref[...] = _mm_hi(_silu(c_ref[...]), w_ref[...]) + b_ref[...]


def _modulation(c, w_mod, b_mod):
    n = c.shape[0]
    tn = 1536
    return pl.pallas_call(
        _mod_kernel,
        out_shape=jax.ShapeDtypeStruct((n, MOD_CHUNKS * D_MODEL), F32),
        grid=(MOD_CHUNKS * D_MODEL // tn,),
        in_specs=[pl.BlockSpec((n, D_MODEL), lambda j: (0, 0)),
                  pl.BlockSpec((D_MODEL, tn), lambda j: (0, j)),
                  pl.BlockSpec((1, tn), lambda j: (0, j))],
        out_specs=pl.BlockSpec((n, tn), lambda j: (0, j)),
        compiler_params=_params("parallel"),
        name="modulation",
    )(c, w_mod, b_mod.reshape(1, -1))


def _mod_spec(per_token, tm, col):
    if per_token:
        return pl.BlockSpec((None, tm, D_MODEL), lambda b, i, *_: (b, i, col))
    return pl.BlockSpec((None, 1, D_MODEL), lambda b, i, *_: (b, 0, col))


def _inproj_kernel(x_ref, sh_ref, sc_ref, nw_ref, w_ref, wba_ref, wbat_ref, out_ref, ba_ref, bat_ref, h_scr):
    @pl.when(pl.program_id(2) == 0)
    def _():
        h = _rms(x_ref[...], nw_ref[...]) * (1.0 + sc_ref[...]) + sh_ref[...]
        hb = h.astype(BF16)
        h_scr[...] = hb
        ba_ref[...] = _mm(hb, wba_ref[...])
        bat_ref[...] = _mm(wbat_ref[...], hb, NT)

    out_ref[...] = _mm(h_scr[...], w_ref[...]).astype(BF16)


def _in_projection(x, mod, per_token, norm_w, w_main, w_ba, w_bat, tm, tn):
    nb, l, _ = x.shape
    return pl.pallas_call(
        _inproj_kernel,
        out_shape=(jax.ShapeDtypeStruct((nb, l, PROJ_W), BF16),
                   jax.ShapeDtypeStruct((nb, l, BA_W), F32),
                   jax.ShapeDtypeStruct((nb, 2 * DN_HEADS, l), F32)),
        grid=(nb, l // tm, PROJ_W // tn),
        in_specs=[pl.BlockSpec((None, tm, D_MODEL), lambda b, i, j: (b, i, 0)),
                  _mod_spec(per_token, tm, 0),
                  _mod_spec(per_token, tm, 1),
                  pl.BlockSpec((1, D_MODEL), lambda b, i, j: (0, 0)),
                  pl.BlockSpec((D_MODEL, tn), lambda b, i, j: (0, j)),
                  pl.BlockSpec((D_MODEL, BA_W), lambda b, i, j: (0, 0)),
                  pl.BlockSpec((2 * DN_HEADS, D_MODEL), lambda b, i, j: (0, 0))],
        out_specs=(pl.BlockSpec((None, tm, tn), lambda b, i, j: (b, i, j)),
                   pl.BlockSpec((None, tm, BA_W), lambda b, i, j: (b, i, 0)),
                   pl.BlockSpec((None, 2 * DN_HEADS, tm), lambda b, i, j: (b, 0, i))),
        scratch_shapes=[pltpu.VMEM((tm, D_MODEL), BF16)],
        compiler_params=_params("parallel", "parallel", "arbitrary"),
        name="in_projection",
    )(x, mod, mod, norm_w, w_main, w_ba, w_bat)


def _delta_prefill_kernel(q_ref, k_ref, v_ref, z_ref, ba_ref, bat_ref, cw_ref, alog_l_ref, dtb_l_ref,
                          alog_c_ref, dtb_c_ref, nw_ref, s0_ref, c0_ref,
                          o_ref, sout_ref, cout_ref, ext_scr, s_scr, qkv_scr, pow_scr, inv_scr, uw_scr):
    step = pl.program_id(1)
    c = CHUNK
    lo = CONV_HALO - (DN_CONV - 1)

    @pl.when(step == 0)
    def _():
        s_scr[...] = s0_ref[...]
        ext_scr[lo:CONV_HALO, :] = c0_ref[...]

    ext_scr[CONV_HALO:, 0:DN_QK] = q_ref[...].astype(F32)
    ext_scr[CONV_HALO:, DN_QK:2 * DN_QK] = k_ref[...].astype(F32)
    ext_scr[CONV_HALO:, 2 * DN_QK:] = v_ref[...].astype(F32)

    row = lax.broadcasted_iota(jnp.int32, (c, c), 0)
    col = lax.broadcasted_iota(jnp.int32, (c, c), 1)
    causal = row >= col
    strict = row > col
    eye = (row == col).astype(F32)
    ltri = causal.astype(F32)
    utri = (row <= col).astype(F32)

    ba = ba_ref[...]
    beta_tok = _sigmoid(ba)
    g_tok = -jnp.exp(alog_l_ref[...]) * _softplus(ba + dtb_l_ref[...])
    gc_tok = _mm_hi(ltri, g_tok)
    bat = bat_ref[...]
    g_t = -jnp.exp(alog_c_ref[...]) * _softplus(bat[DN_HEADS:, :] + dtb_c_ref[...])
    gc_t = _mm_hi(g_t, utri)

    def conv(c0):
        acc = ext_scr[lo:lo + c, c0:c0 + DN_DK] * cw_ref[0:1, c0:c0 + DN_DK]
        for i in range(1, DN_CONV):
            acc = acc + ext_scr[lo + i:lo + i + c, c0:c0 + DN_DK] * cw_ref[i:i + 1, c0:c0 + DN_DK]
        return _silu(acc)

    def decay_of(h):
        gc = gc_tok[:, DN_HEADS + h:DN_HEADS + h + 1]
        return gc, jnp.where(causal, jnp.exp(jnp.where(causal, gc - gc_t[h:h + 1, :], 0.0)), 0.0)

    for h in range(DN_HEADS):
        q = conv(h * DN_DK)
        k = conv(DN_QK + h * DN_DK)
        q = q * lax.rsqrt(jnp.sum(q * q, axis=-1, keepdims=True) + EPS) * (DN_DK ** -0.5)
        k = k * lax.rsqrt(jnp.sum(k * k, axis=-1, keepdims=True) + EPS)
        qkv_scr[:, h * DN_DK:(h + 1) * DN_DK] = q
        qkv_scr[:, DN_QK + h * DN_DK:DN_QK + (h + 1) * DN_DK] = k
        qkv_scr[:, 2 * DN_QK + h * DN_DV:2 * DN_QK + (h + 1) * DN_DV] = conv(2 * DN_QK + h * DN_DV)
        _, decay = decay_of(h)
        a_mat = jnp.where(strict, _mm(k * beta_tok[:, h:h + 1], k, NT) * decay, 0.0)
        pow_scr[h] = a_mat
        inv_scr[h] = eye - a_mat

    for _ in range(int(math.log2(c)) - 1):
        for h in range(DN_HEADS):
            a_pow = pow_scr[h]
            pow_scr[h] = _mm_split(_split_lhs(a_pow), _split_rhs(a_pow))
        for h in range(DN_HEADS):
            inv = inv_scr[h]
            inv_scr[h] = inv + _mm_split(_split_lhs(inv), _split_rhs(pow_scr[h]))

    def qkv_of(h):
        return (qkv_scr[:, h * DN_DK:(h + 1) * DN_DK], qkv_scr[:, DN_QK + h * DN_DK:DN_QK + (h + 1) * DN_DK],
                qkv_scr[:, 2 * DN_QK + h * DN_DV:2 * DN_QK + (h + 1) * DN_DV])

    for h in range(DN_HEADS):
        q, k, v = qkv_of(h)
        beta = beta_tok[:, h:h + 1]
        gc, decay = decay_of(h)
        rhs = jnp.concatenate([v * beta, k * beta * jnp.exp(gc)], axis=-1)
        uw_scr[h] = _mm_split(_split_lhs(inv_scr[h]), _split_rhs(rhs))
        pow_scr[h] = jnp.where(causal, _mm(q, k, NT) * decay, 0.0)
    for h in range(DN_HEADS):
        inv_scr[h] = uw_scr[h, :, 0:DN_DV] - _mm(uw_scr[h, :, DN_DV:], s_scr[h])
    for h in range(DN_HEADS):
        q, k, _ = qkv_of(h)
        gc = gc_tok[:, DN_HEADS + h:DN_HEADS + h + 1]
        g_last = gc[c - 1:c, :]
        s = s_scr[h]
        v_new = inv_scr[h]
        o = _mm(q * jnp.exp(gc), s) + _mm(pow_scr[h], v_new)
        s_scr[h] = s * jnp.exp(g_last) + _mm(k * jnp.exp(g_last - gc), v_new, TN)
        z = z_ref[:, h * DN_DV:(h + 1) * DN_DV].astype(F32)
        o_ref[:, h * DN_DV:(h + 1) * DN_DV] = (_rms(o, nw_ref[...]) * _silu(z)).astype(BF16)

    ext_scr[0:CONV_HALO, :] = ext_scr[c:c + CONV_HALO, :]

    @pl.when(step == pl.num_programs(1) - 1)
    def _():
        sout_ref[...] = s_scr[...]
        cout_ref[...] = ext_scr[lo:CONV_HALO, :]


def _lane_pad(v, offset):
    return jnp.zeros((1, BA_W), F32).at[0, offset:offset + v.shape[0]].set(v.astype(F32))


def _delta_prefill(proj, ba, bat, conv_w, a_log, dt_bias, dn_norm_w, s0, c0):
    nb, l, _ = proj.shape
    c = CHUNK
    col_spec = lambda j: pl.BlockSpec((None, c, D_MODEL), lambda b, i: (b, i, j))
    full = lambda shape: pl.BlockSpec(shape, lambda b, i: (0,) * len(shape))
    return pl.pallas_call(
        _delta_prefill_kernel,
        out_shape=(jax.ShapeDtypeStruct((nb, l, DN_VW), BF16),
                   jax.ShapeDtypeStruct((nb, DN_HEADS, DN_DK, DN_DV), F32),
                   jax.ShapeDtypeStruct((nb, DN_CONV - 1, DN_CONV_W), F32)),
        grid=(nb, l // c),
        in_specs=[col_spec(COL_Q_A), col_spec(COL_K_A), col_spec(COL_V_A), col_spec(COL_Z_A),
                  pl.BlockSpec((None, c, BA_W), lambda b, i: (b, i, 0)),
                  pl.BlockSpec((None, 2 * DN_HEADS, c), lambda b, i: (b, 0, i)),
                  full((DN_CONV, DN_CONV_W)), full((1, BA_W)), full((1, BA_W)),
                  full((DN_HEADS, 1)), full((DN_HEADS, 1)), full((1, DN_DV)),
                  pl.BlockSpec((None, DN_HEADS, DN_DK, DN_DV), lambda b, i: (b, 0, 0, 0)),
                  pl.BlockSpec((None, DN_CONV - 1, DN_CONV_W), lambda b, i: (b, 0, 0))],
        out_specs=(pl.BlockSpec((None, c, DN_VW), lambda b, i: (b, i, 0)),
                   pl.BlockSpec((None, DN_HEADS, DN_DK, DN_DV), lambda b, i: (b, 0, 0, 0)),
                   pl.BlockSpec((None, DN_CONV - 1, DN_CONV_W), lambda b, i: (b, 0, 0))),
        scratch_shapes=[pltpu.VMEM((c + CONV_HALO, DN_CONV_W), F32),
                        pltpu.VMEM((DN_HEADS, DN_DK, DN_DV), F32),
                        pltpu.VMEM((c, DN_CONV_W), F32),
                        pltpu.VMEM((DN_HEADS, c, c), F32),
                        pltpu.VMEM((DN_HEADS, c, c), F32),
                        pltpu.VMEM((DN_HEADS, c, DN_DV + DN_DK), F32)],
        compiler_params=_params("parallel", "arbitrary"),
        name="delta_prefill",
    )(proj, proj, proj, proj, ba, bat, conv_w, _lane_pad(a_log, DN_HEADS), _lane_pad(dt_bias, DN_HEADS),
      a_log.reshape(DN_HEADS, 1), dt_bias.reshape(DN_HEADS, 1), dn_norm_w.reshape(1, DN_DV), s0, c0)


def _log_gamma(h):
    return math.log(1.0 - 2.0 ** (-5.0 - h))


def _rotate(x, cos, sin):
    half = x.shape[-1] // 2
    x1, x2 = x[:, :half], x[:, half:]
    return jnp.concatenate([x1 * cos - x2 * sin, x1 * sin + x2 * cos], axis=-1)


def _group_norm_gate(o, gw, gb, gate):
    mu = jnp.mean(o, axis=-1, keepdims=True)
    var = jnp.mean(jnp.square(o - mu), axis=-1, keepdims=True)
    return ((o - mu) * lax.rsqrt(var + EPS) * gw + gb) * _silu(gate)


def _ret_prefill_kernel(q_ref, k_ref, v_ref, g_ref, inv_ref, gw_ref, gb_ref, s0_ref, o_ref, sout_ref, s_scr):
    step = pl.program_id(1)
    c = CHUNK

    @pl.when(step == 0)
    def _():
        s_scr[...] = s0_ref[...]

    idx_c = lax.broadcasted_iota(jnp.int32, (c, 1), 0)
    pos = (step * c + idx_c).astype(F32)
    ang = pos * inv_ref[...]
    cos, sin = jnp.cos(ang), jnp.sin(ang)
    rel = (lax.broadcasted_iota(jnp.int32, (c, c), 0) - lax.broadcasted_iota(jnp.int32, (c, c), 1)).astype(F32)
    idx = idx_c.astype(F32)

    for h in range(RET_HEADS):
        lg = _log_gamma(h)
        q = _rotate(q_ref[:, h * RET_DK:(h + 1) * RET_DK].astype(F32), cos, sin)
        k = _rotate(k_ref[:, h * RET_DK:(h + 1) * RET_DK].astype(F32), cos, sin) * (RET_DK ** -0.5)
        v = v_ref[:, h * RET_DV:(h + 1) * RET_DV]
        d_mat = jnp.where(rel >= 0, jnp.exp(jnp.maximum(rel, 0.0) * lg), 0.0)
        intra = _mm(_mm(q, k, NT) * d_mat, v)
        s = s_scr[h]
        cross = _mm(q * jnp.exp((idx + 1.0) * lg), s)
        s_scr[h] = s * math.exp(c * lg) + _mm(k * jnp.exp((c - 1.0 - idx) * lg), v, TN)
        sl = slice(h * RET_DV, (h + 1) * RET_DV)
        o_ref[:, sl] = _group_norm_gate(intra + cross, gw_ref[:, sl], gb_ref[:, sl],
                                        g_ref[:, sl].astype(F32)).astype(BF16)

    @pl.when(step == pl.num_programs(1) - 1)
    def _():
        sout_ref[...] = s_scr[...]


def _ret_prefill(proj, inv_freq, gn_w, gn_b, s0):
    nb, l, _ = proj.shape
    c = CHUNK
    full = lambda shape: pl.BlockSpec(shape, lambda b, i: (0,) * len(shape))
    state_spec = pl.BlockSpec((None, RET_HEADS, RET_DK, RET_DV), lambda b, i: (b, 0, 0, 0))
    return pl.pallas_call(
        _ret_prefill_kernel,
        out_shape=(jax.ShapeDtypeStruct((nb, l, RET_VW), BF16),
                   jax.ShapeDtypeStruct((nb, RET_HEADS, RET_DK, RET_DV), F32)),
        grid=(nb, l // c),
        in_specs=[pl.BlockSpec((None, c, RET_QK), lambda b, i: (b, i, COL_Q_B)),
                  pl.BlockSpec((None, c, RET_QK), lambda b, i: (b, i, COL_K_B)),
                  pl.BlockSpec((None, c, RET_VW), lambda b, i: (b, i, COL_V_B)),
                  pl.BlockSpec((None, c, RET_VW), lambda b, i: (b, i, COL_G_B)),
                  full((1, RET_DK // 2)), full((1, RET_VW)), full((1, RET_VW)), state_spec],
        out_specs=(pl.BlockSpec((None, c, RET_VW), lambda b, i: (b, i, 0)), state_spec),
        scratch_shapes=[pltpu.VMEM((RET_HEADS, RET_DK, RET_DV), F32)],
        compiler_params=_params("parallel", "arbitrary"),
        name="retention_prefill",
    )(proj, proj, proj, proj, inv_freq, gn_w.reshape(1, RET_VW), gn_b.reshape(1, RET_VW), s0)


ROWS = 8


def _row_mm(a, s):
    return _mm(jnp.broadcast_to(a, (ROWS, a.shape[-1])), s)[0:1, :]


def _outer(a, b):
    first = lax.broadcasted_iota(jnp.int32, (ROWS, a.shape[-1]), 0) == 0
    a8 = jnp.where(first, jnp.broadcast_to(a, (ROWS, a.shape[-1])), 0.0)
    return _mm(a8, jnp.broadcast_to(b, (ROWS, b.shape[-1])), TN)


def _delta_step_kernel(qkv_ref, z_ref, ba_ref, cw_ref, alog_ref, dtb_ref, nw_ref, s_ref, c_ref,
                       o_ref, sout_ref, cout_ref):
    u_new = qkv_ref[...].astype(F32)
    buf = c_ref[...]
    acc = u_new * cw_ref[DN_CONV - 1:DN_CONV, :]
    for i in range(DN_CONV - 1):
        acc = acc + buf[i:i + 1, :] * cw_ref[i:i + 1, :]
    qkv = _silu(acc)
    cout_ref[0:DN_CONV - 2, :] = buf[1:, :]
    cout_ref[DN_CONV - 2:, :] = u_new
    ba = ba_ref[...]
    beta_all = _sigmoid(ba)
    g_all = -jnp.exp(alog_ref[...]) * _softplus(ba + dtb_ref[...])
    for h in range(DN_HEADS):
        q = qkv[:, h * DN_DK:(h + 1) * DN_DK]
        k = qkv[:, DN_QK + h * DN_DK:DN_QK + (h + 1) * DN_DK]
        v = qkv[:, 2 * DN_QK + h * DN_DV:2 * DN_QK + (h + 1) * DN_DV]
        q = q * lax.rsqrt(jnp.sum(q * q, axis=-1, keepdims=True) + EPS) * (DN_DK ** -0.5)
        k = k * lax.rsqrt(jnp.sum(k * k, axis=-1, keepdims=True) + EPS)
        beta = beta_all[:, h:h + 1]
        eg = jnp.exp(g_all[:, DN_HEADS + h:DN_HEADS + h + 1])
        s = s_ref[h]
        kb = k * beta
        v_new = v * beta - _row_mm(kb * eg, s)
        o = _row_mm(q * eg, s) + jnp.sum(q * k, axis=-1, keepdims=True) * v_new
        sout_ref[h] = s * eg + _outer(k, v_new)
        z = z_ref[:, h * DN_DV:(h + 1) * DN_DV].astype(F32)
        o_ref[:, h * DN_DV:(h + 1) * DN_DV] = (_rms(o, nw_ref[...]) * _silu(z)).astype(BF16)


def _delta_step(proj, ba, conv_w, a_log, dt_bias, dn_norm_w, s0, c0):
    n = proj.shape[0]
    full = lambda shape: pl.BlockSpec(shape, lambda b: (0,) * len(shape))
    state_spec = pl.BlockSpec((None, DN_HEADS, DN_DK, DN_DV), lambda b: (b, 0, 0, 0))
    conv_spec = pl.BlockSpec((None, DN_CONV - 1, DN_CONV_W), lambda b: (b, 0, 0))
    return pl.pallas_call(
        _delta_step_kernel,
        out_shape=(jax.ShapeDtypeStruct((n, 1, DN_VW), BF16),
                   jax.ShapeDtypeStruct((n, DN_HEADS, DN_DK, DN_DV), F32),
                   jax.ShapeDtypeStruct((n, DN_CONV - 1, DN_CONV_W), F32)),
        grid=(n,),
        in_specs=[pl.BlockSpec((None, 1, DN_CONV_W), lambda b: (b, 0, 0)),
                  pl.BlockSpec((None, 1, DN_VW), lambda b: (b, 0, COL_Z_A)),
                  pl.BlockSpec((None, 1, BA_W), lambda b: (b, 0, 0)),
                  full((DN_CONV, DN_CONV_W)), full((1, BA_W)), full((1, BA_W)), full((1, DN_DV)),
                  state_spec, conv_spec],
        out_specs=(pl.BlockSpec((None, 1, DN_VW), lambda b: (b, 0, 0)), state_spec, conv_spec),
        compiler_params=_params("parallel"),
        name="delta_step",
    )(proj, proj, ba, conv_w, _lane_pad(a_log, DN_HEADS), _lane_pad(dt_bias, DN_HEADS),
      dn_norm_w.reshape(1, DN_DV), s0, c0)


def _ret_step_kernel(q_ref, k_ref, v_ref, g_ref, inv_ref, gw_ref, gb_ref, s_ref, o_ref, sout_ref):
    ang = float(PAST_LEN) * inv_ref[...]
    cos, sin = jnp.cos(ang), jnp.sin(ang)
    for h in range(RET_HEADS):
        gamma = math.exp(_log_gamma(h))
        q = _rotate(q_ref[:, h * RET_DK:(h + 1) * RET_DK].astype(F32), cos, sin)
        k = _rotate(k_ref[:, h * RET_DK:(h + 1) * RET_DK].astype(F32), cos, sin) * (RET_DK ** -0.5)
        v = v_ref[:, h * RET_DV:(h + 1) * RET_DV].astype(F32)
        s = s_ref[h]
        o = jnp.sum(q * k, axis=-1, keepdims=True) * v + _row_mm(q * gamma, s)
        sout_ref[h] = s * gamma + _outer(k, v)
        sl = slice(h * RET_DV, (h + 1) * RET_DV)
        o_ref[:, sl] = _group_norm_gate(o, gw_ref[:, sl], gb_ref[:, sl], g_ref[:, sl].astype(F32)).astype(BF16)


def _ret_step(proj, inv_freq, gn_w, gn_b, s0):
    n = proj.shape[0]
    full = lambda shape: pl.BlockSpec(shape, lambda b: (0,) * len(shape))
    state_spec = pl.BlockSpec((None, RET_HEADS, RET_DK, RET_DV), lambda b: (b, 0, 0, 0))
    return pl.pallas_call(
        _ret_step_kernel,
        out_shape=(jax.ShapeDtypeStruct((n, 1, RET_VW), BF16),
                   jax.ShapeDtypeStruct((n, RET_HEADS, RET_DK, RET_DV), F32)),
        grid=(n,),
        in_specs=[pl.BlockSpec((None, 1, RET_QK), lambda b: (b, 0, COL_Q_B)),
                  pl.BlockSpec((None, 1, RET_QK), lambda b: (b, 0, COL_K_B)),
                  pl.BlockSpec((None, 1, RET_VW), lambda b: (b, 0, COL_V_B)),
                  pl.BlockSpec((None, 1, RET_VW), lambda b: (b, 0, COL_G_B)),
                  full((1, RET_DK // 2)), full((1, RET_VW)), full((1, RET_VW)), state_spec],
        out_specs=(pl.BlockSpec((None, 1, RET_VW), lambda b: (b, 0, 0)), state_spec),
        compiler_params=_params("parallel"),
        name="retention_step",
    )(proj, proj, proj, proj, inv_freq, gn_w.reshape(1, RET_VW), gn_b.reshape(1, RET_VW), s0)


def _post_mixer_kernel(oa_ref, ob_ref, ga_ref, gb_ref, x_ref, g1_ref, sh2_ref, sc2_ref, wda_ref, wdb_ref,
                       wout_ref, nw_ref, rw_ref, x1_ref, h_ref, logit_ref):
    y_a = _mm(oa_ref[...], wda_ref[...])
    y_b = _mm(ob_ref[...], wdb_ref[...])
    merged = _sigmoid(ga_ref[...].astype(F32)) * y_a + _sigmoid(gb_ref[...].astype(F32)) * y_b
    x1 = x_ref[...] + g1_ref[...] * _mm(merged, wout_ref[...])
    x1_ref[...] = x1
    hf = _rms(x1, nw_ref[...]) * (1.0 + sc2_ref[...]) + sh2_ref[...]
    h_ref[...] = hf.astype(BF16)
    logit_ref[...] = _mm_hi(rw_ref[...], hf, NT)


def _post_mixer(o_a, o_b, proj, x, mod, per_token, w_down_a, w_down_b, w_out, norm_w, router_w, tm):
    nb, l, _ = x.shape
    tok = lambda w, j=0: pl.BlockSpec((None, tm, w), lambda b, i: (b, i, j))
    full = lambda shape: pl.BlockSpec(shape, lambda b, i: (0,) * len(shape))
    return pl.pallas_call(
        _post_mixer_kernel,
        out_shape=(jax.ShapeDtypeStruct((nb, l, D_MODEL), F32),
                   jax.ShapeDtypeStruct((nb, l, D_MODEL), BF16),
                   jax.ShapeDtypeStruct((nb, N_EXPERTS, l), F32)),
        grid=(nb, l // tm),
        in_specs=[tok(DN_VW), tok(RET_VW), tok(D_MODEL, COL_GATE_A), tok(D_MODEL, COL_GATE_B), tok(D_MODEL),
                  _mod_spec(per_token, tm, 2), _mod_spec(per_token, tm, 3), _mod_spec(per_token, tm, 4),
                  full((DN_VW, D_MODEL)), full((RET_VW, D_MODEL)), full((D_MODEL, D_MODEL)),
                  full((1, D_MODEL)), full((N_EXPERTS, D_MODEL))],
        out_specs=(tok(D_MODEL), tok(D_MODEL), pl.BlockSpec((None, N_EXPERTS, tm), lambda b, i: (b, 0, i))),
        compiler_params=_params("parallel", "parallel"),
        name="post_mixer",
    )(o_a, o_b, proj, proj, x, mod, mod, mod, w_down_a, w_down_b, w_out, norm_w, router_w)


def _first_max(x, axis, n):
    m = jnp.max(x, axis=axis, keepdims=True)
    ids = lax.broadcasted_iota(jnp.int32, x.shape, axis)
    first = jnp.min(jnp.where(x == m, ids, n), axis=axis, keepdims=True)
    return m, ids == first


def _route_kernel(logit_ref, bias_ref, gw_ref, gwt_ref):
    t = logit_ref.shape[-1]
    scores = _sigmoid(logit_ref[...])
    biased = scores + bias_ref[...]
    neg = jnp.float32(-jnp.inf)
    b3 = biased.reshape(N_GROUPS, GROUP_SIZE, t)
    m1, hit = _first_max(b3, 1, GROUP_SIZE)
    m2 = jnp.max(jnp.where(hit, neg, b3), axis=1, keepdims=True)
    gscore = (m1 + m2).reshape(N_GROUPS, t)
    gmask = jnp.zeros((N_GROUPS, t), jnp.bool_)
    for _ in range(TOPK_GROUPS):
        _, hit = _first_max(gscore, 0, N_GROUPS)
        gmask = jnp.logical_or(gmask, hit)
        gscore = jnp.where(hit, neg, gscore)
    emask = jnp.broadcast_to(gmask.reshape(N_GROUPS, 1, t), (N_GROUPS, GROUP_SIZE, t)).reshape(N_EXPERTS, t)
    masked = jnp.where(emask, biased, neg)
    sel = jnp.zeros((N_EXPERTS, t), jnp.bool_)
    for _ in range(TOP_K):
        _, hit = _first_max(masked, 0, N_EXPERTS)
        sel = jnp.logical_or(sel, hit)
        masked = jnp.where(hit, neg, masked)
    wts = jnp.where(sel, scores, 0.0)
    gw = wts / jnp.sum(wts, axis=0, keepdims=True) * ROUTED_SCALE
    gw_ref[...] = gw
    r = lax.broadcasted_iota(jnp.int32, (N_EXPERTS, N_EXPERTS), 0)
    c = lax.broadcasted_iota(jnp.int32, (N_EXPERTS, N_EXPERTS), 1)
    gwt_ref[...] = _mm_hi(gw, (r == c).astype(F32), TN)


def _route(logits_t, router_bias, tr):
    nb, _, l = logits_t.shape
    return pl.pallas_call(
        _route_kernel,
        out_shape=(jax.ShapeDtypeStruct((nb, N_EXPERTS, l), F32),
                   jax.ShapeDtypeStruct((nb, l, N_EXPERTS), F32)),
        grid=(nb, l // tr),
        in_specs=[pl.BlockSpec((None, N_EXPERTS, tr), lambda b, i: (b, 0, i)),
                  pl.BlockSpec((N_EXPERTS, 1), lambda b, i: (0, 0))],
        out_specs=(pl.BlockSpec((None, N_EXPERTS, tr), lambda b, i: (b, 0, i)),
                   pl.BlockSpec((None, tr, N_EXPERTS), lambda b, i: (b, i, 0))),
        compiler_params=_params("parallel", "parallel"),
        name="route",
    )(logits_t, router_bias.reshape(N_EXPERTS, 1))


def _moe_kernel(h_ref, gwt_ref, x1_ref, g2_ref, wg_ref, wu_ref, wd_ref, wsg_ref, wsu_ref, wsd_ref, fw_ref,
                y_ref, acc_scr):
    e = pl.program_id(2)
    hb = h_ref[...]

    @pl.when(e == 0)
    def _():
        acc_scr[...] = _mm(_silu(_mm(hb, wsg_ref[...])) * _mm(hb, wsu_ref[...]), wsd_ref[...])

    onehot = (lax.broadcasted_iota(jnp.int32, (N_EXPERTS, D_EXPERT), 0) == e).astype(F32)
    gate = _mm_hi(gwt_ref[...], onehot)
    hid = _silu(_mm(hb, wg_ref[...])) * _mm(hb, wu_ref[...])
    acc_scr[...] += _mm(hid, wd_ref[...]) * jnp.concatenate([gate] * (D_MODEL // D_EXPERT), axis=-1)

    @pl.when(e == pl.num_programs(2) - 1)
    def _():
        y_ref[...] = _rms(x1_ref[...] + g2_ref[...] * acc_scr[...], fw_ref[...])


def _moe(hffn, gwt, x1, mod, per_token, w_gate, w_up, w_down, ws_gate, ws_up, ws_down, final_w, tm):
    nb, l, _ = x1.shape
    tok = lambda w: pl.BlockSpec((None, tm, w), lambda b, i, e: (b, i, 0))
    full = lambda shape: pl.BlockSpec(shape, lambda b, i, e: (0,) * len(shape))
    return pl.pallas_call(
        _moe_kernel,
        out_shape=jax.ShapeDtypeStruct((nb, l, D_MODEL), F32),
        grid=(nb, l // tm, N_EXPERTS),
        in_specs=[tok(D_MODEL), tok(N_EXPERTS), tok(D_MODEL), _mod_spec(per_token, tm, 5),
                  pl.BlockSpec((None, D_MODEL, D_EXPERT), lambda b, i, e: (e, 0, 0)),
                  pl.BlockSpec((None, D_MODEL, D_EXPERT), lambda b, i, e: (e, 0, 0)),
                  pl.BlockSpec((None, D_EXPERT, D_MODEL), lambda b, i, e: (e, 0, 0)),
                  full((D_MODEL, D_SHARED)), full((D_MODEL, D_SHARED)), full((D_SHARED, D_MODEL)),
                  full((1, D_MODEL))],
        out_specs=tok(D_MODEL),
        scratch_shapes=[pltpu.VMEM((tm, D_MODEL), F32)],
        compiler_params=_params("parallel", "parallel", "arbitrary"),
        name="moe",
    )(hffn, gwt, x1, mod, w_gate, w_up, w_down, ws_gate, ws_up, ws_down, final_w)


def _trunk(x, mod, per_token, states, w):
    nb, l, _ = x.shape
    tm_proj, tn_proj, tm_post, tr, tm_moe = min(l, 1024), 1024, min(l, 512), min(l, 512), min(l, 1024)
    proj, ba, bat = _in_projection(x, mod, per_token, w["norm_mix"], w["w_main"], w["w_ba"], w["w_bat"],
                                   tm_proj, tn_proj)
    if states is None:
        c0 = jnp.zeros((nb, DN_CONV - 1, DN_CONV_W), F32)
        sd0 = jnp.zeros((nb, DN_HEADS, DN_DK, DN_DV), F32)
        sr0 = jnp.zeros((nb, RET_HEADS, RET_DK, RET_DV), F32)
        o_a, sd, cb = _delta_prefill(proj, ba, bat, w["conv_w"], w["a_log"], w["dt_bias"], w["dn_norm"], sd0, c0)
        o_b, sr = _ret_prefill(proj, w["inv_freq"], w["gn_w"], w["gn_b"], sr0)
    else:
        c0, sd0, sr0 = states
        n = nb * l
        o_a, sd, cb = _delta_step(proj.reshape(n, 1, PROJ_W), ba.reshape(n, 1, BA_W), w["conv_w"], w["a_log"],
                                  w["dt_bias"], w["dn_norm"], sd0, c0)
        o_b, sr = _ret_step(proj.reshape(n, 1, PROJ_W), w["inv_freq"], w["gn_w"], w["gn_b"], sr0)
        o_a = o_a.reshape(nb, l, DN_VW)
        o_b = o_b.reshape(nb, l, RET_VW)
    x1, hffn, logits_t = _post_mixer(o_a, o_b, proj, x, mod, per_token, w["w_down_a"], w["w_down_b"], w["w_out"],
                                     w["norm_ffn"], w["router_w"], tm_post)
    _, gwt = _route(logits_t, w["router_bias"], tr)
    y = _moe(hffn, gwt, x1, mod, per_token, w["w_gate"], w["w_up"], w["w_down"], w["ws_gate"], w["ws_up"],
             w["ws_down"], w["final_norm"], tm_moe)
    return y, cb, sd, sr


def kernel(x_prompt, x_sample, state_conv, state_delta, state_ret, c_prompt, c_sample, w_mod, b_mod, norm_mix_w, w_in, conv_w, a_log, dt_bias, dn_norm_w, ret_gn_w, ret_gn_b, w_down_a, w_down_b, w_out, norm_ffn_w, router_w, router_bias, w_gate, w_up, w_down, ws_gate, ws_up, ws_down, final_norm_w):
    bp, lp, _ = x_prompt.shape
    bs = x_sample.shape[0]
    half = RET_DK // 2
    w_in0 = w_in[0]
    c0, c1 = DN_CONV_W, DN_CONV_W + 2 * DN_HEADS
    w_ba = jnp.pad(w_in0[:, c0:c1], ((0, 0), (0, BA_W - 2 * DN_HEADS))).astype(BF16)
    w = {
        "norm_mix": norm_mix_w[0].reshape(1, D_MODEL),
        "w_main": jnp.concatenate([w_in0[:, :c0], w_in0[:, c1:]], axis=1).astype(BF16),
        "w_ba": w_ba,
        "w_bat": w_in0[:, c0:c1].T.astype(BF16),
        "conv_w": conv_w[0], "a_log": a_log[0], "dt_bias": dt_bias[0], "dn_norm": dn_norm_w[0],
        "inv_freq": (ROPE_BASE ** (-jnp.arange(half, dtype=F32) / half)).reshape(1, half),
        "gn_w": ret_gn_w[0], "gn_b": ret_gn_b[0],
        "w_down_a": w_down_a[0].astype(BF16), "w_down_b": w_down_b[0].astype(BF16), "w_out": w_out[0].astype(BF16),
        "norm_ffn": norm_ffn_w[0].reshape(1, D_MODEL), "router_w": router_w[0], "router_bias": router_bias[0],
        "w_gate": w_gate[0].astype(BF16), "w_up": w_up[0].astype(BF16), "w_down": w_down[0].astype(BF16),
        "ws_gate": ws_gate[0].astype(BF16), "ws_up": ws_up[0].astype(BF16), "ws_down": ws_down[0].astype(BF16),
        "final_norm": final_norm_w.reshape(1, D_MODEL),
    }
    mod = _modulation(jnp.concatenate([c_prompt, c_sample], axis=0), w_mod[0], b_mod[0])
    mod_p = mod[:bp].reshape(bp, 1, MOD_CHUNKS * D_MODEL)
    mod_s = mod[bp:].reshape(1, bs, MOD_CHUNKS * D_MODEL)

    y_p, conv_p, delta_p, ret_p = _trunk(x_prompt, mod_p, False, None, w)
    y_s, conv_s, delta_s, ret_s = _trunk(x_sample.reshape(1, bs, D_MODEL), mod_s, True,
                                         (state_conv[0], state_delta[0], state_ret[0]), w)
    return (y_p, y_s.reshape(bs, 1, D_MODEL), conv_p[None], delta_p[None], ret_p[None],
            conv_s[None], delta_s[None], ret_s[None])
```

```python
import functools
import math

import jax
import jax.numpy as jnp
from jax import lax
from jax.experimental import pallas as pl
from jax.experimental.pallas import tpu as pltpu

F32 = jnp.float32
BF16 = jnp.bfloat16

D_MODEL = 1024
DN_HEADS, DN_DK, DN_DV, DN_CONV = 8, 128, 128, 4
DN_QK = DN_HEADS * DN_DK
DN_VW = DN_HEADS * DN_DV
DN_CONV_W = 2 * DN_QK + DN_VW
RET_HEADS, RET_DK, RET_DV = 4, 256, 512
RET_QK = RET_HEADS * RET_DK
RET_VW = RET_HEADS * RET_DV
ROPE_BASE = 10000.0
PAST_LEN = 16384
N_EXPERTS, TOP_K, N_GROUPS, TOPK_GROUPS = 64, 8, 8, 4
GROUP_SIZE = N_EXPERTS // N_GROUPS
D_EXPERT, D_SHARED = 256, 256
ROUTED_SCALE = 2.5
MOD_CHUNKS = 6
EPS = 1e-6

PROJ_W = 12 * D_MODEL
COL_Q_A, COL_K_A, COL_V_A, COL_Z_A = 0, 1, 2, 3
COL_Q_B, COL_K_B = 4, 5
COL_V_B, COL_G_B = 3, 4
COL_GATE_A, COL_GATE_B = 10, 11
BA_W = 128
CHUNK = 128
CONV_HALO = 8
VMEM_LIMIT = 56 * 1024 * 1024
LANES = 128
MOE_BLOCK = 256
TOKEN_TILE = (D_MODEL // LANES, LANES)

NN = (((1,), (0,)), ((), ()))
NT = (((1,), (1,)), ((), ()))
TN = (((0,), (0,)), ((), ()))


def _mm(a, b, dims=NN):
    return lax.dot_general(a.astype(BF16), b.astype(BF16), dims, preferred_element_type=F32)


def _mm_hi(a, b, dims=NN):
    return lax.dot_general(a.astype(F32), b.astype(F32), dims, precision=lax.Precision.HIGHEST,
                           preferred_element_type=F32)


def _hi_lo(x):
    hi = x.astype(BF16)
    return hi, (x - hi.astype(F32)).astype(BF16)


def _split_lhs(a):
    hi, lo = _hi_lo(a)
    return jnp.concatenate([hi, lo, hi], axis=1)


def _split_rhs(b):
    hi, lo = _hi_lo(b)
    return jnp.concatenate([hi, hi, lo], axis=0)


def _mm_split(a3, b3):
    return lax.dot_general(a3, b3, NN, preferred_element_type=F32)


def _sigmoid(x):
    return jax.nn.sigmoid(x)


def _silu(x):
    return x * _sigmoid(x)


def _softplus(x):
    return jnp.maximum(x, 0.0) + jnp.log1p(jnp.exp(-jnp.abs(x)))


def _rms(x, w):
    return x * lax.rsqrt(jnp.mean(x * x, axis=-1, keepdims=True) + EPS) * w


def _params(*sem):
    return pltpu.CompilerParams(dimension_semantics=sem, vmem_limit_bytes=VMEM_LIMIT)


def _mod_kernel(c_ref, w_ref, b_ref, o_ref):
    o_ref[...] = _mm_hi(_silu(c_ref[...]), w_ref[...]) + b_ref[...]


def _modulation(c, w_mod, b_mod):
    n = c.shape[0]
    tn = 1536
    return pl.pallas_call(
        _mod_kernel,
        out_shape=jax.ShapeDtypeStruct((n, MOD_CHUNKS * D_MODEL), F32),
        grid=(MOD_CHUNKS * D_MODEL // tn,),
        in_specs=[pl.BlockSpec((n, D_MODEL), lambda j: (0, 0)),
                  pl.BlockSpec((D_MODEL, tn), lambda j: (0, j)),
                  pl.BlockSpec((1, tn), lambda j: (0, j))],
        out_specs=pl.BlockSpec((n, tn), lambda j: (0, j)),
        compiler_params=_params("parallel"),
        name="modulation",
    )(c, w_mod, b_mod.reshape(1, -1))


def _mod_spec(per_token, tm, col):
    if per_token:
        return pl.BlockSpec((None, tm, D_MODEL), lambda b, i, *_: (b, i, col))
    return pl.BlockSpec((None, 1, D_MODEL), lambda b, i, *_: (b, 0, col))


def _inproj_kernel(x_ref, sh_ref, sc_ref, nw_ref, w_ref, wba_ref, wbat_ref, out_ref, ba_ref, bat_ref, h_scr):
    @pl.when(pl.program_id(2) == 0)
    def _():
        h = _rms(x_ref[...], nw_ref[...]) * (1.0 + sc_ref[...]) + sh_ref[...]
        hb = h.astype(BF16)
        h_scr[...] = hb
        ba_ref[...] = _mm(hb, wba_ref[...])
        bat_ref[...] = _mm(wbat_ref[...], hb, NT)

    out_ref[...] = _mm(h_scr[...], w_ref[...]).astype(BF16)


def _in_projection(x, mod, per_token, norm_w, w_main, w_ba, w_bat, tm, tn):
    nb, l, _ = x.shape
    return pl.pallas_call(
        _inproj_kernel,
        out_shape=(jax.ShapeDtypeStruct((nb, l, PROJ_W), BF16),
                   jax.ShapeDtypeStruct((nb, l, BA_W), F32),
                   jax.ShapeDtypeStruct((nb, 2 * DN_HEADS, l), F32)),
        grid=(nb, l // tm, PROJ_W // tn),
        in_specs=[pl.BlockSpec((None, tm, D_MODEL), lambda b, i, j: (b, i, 0)),
                  _mod_spec(per_token, tm, 0),
                  _mod_spec(per_token, tm, 1),
                  pl.BlockSpec((1, D_MODEL), lambda b, i, j: (0, 0)),
                  pl.BlockSpec((D_MODEL, tn), lambda b, i, j: (0, j)),
                  pl.BlockSpec((D_MODEL, BA_W), lambda b, i, j: (0, 0)),
                  pl.BlockSpec((2 * DN_HEADS, D_MODEL), lambda b, i, j: (0, 0))],
        out_specs=(pl.BlockSpec((None, tm, tn), lambda b, i, j: (b, i, j)),
                   pl.BlockSpec((None, tm, BA_W), lambda b, i, j: (b, i, 0)),
                   pl.BlockSpec((None, 2 * DN_HEADS, tm), lambda b, i, j: (b, 0, i))),
        scratch_shapes=[pltpu.VMEM((tm, D_MODEL), BF16)],
        compiler_params=_params("parallel", "parallel", "arbitrary"),
        name="in_projection",
    )(x, mod, mod, norm_w, w_main, w_ba, w_bat)


def _delta_prefill_kernel(q_ref, k_ref, v_ref, z_ref, ba_ref, bat_ref, cw_ref, alog_l_ref, dtb_l_ref,
                          alog_c_ref, dtb_c_ref, nw_ref, s0_ref, c0_ref,
                          o_ref, sout_ref, cout_ref, ext_scr, s_scr, qkv_scr, pow_scr, inv_scr, uw_scr):
    step = pl.program_id(1)
    c = CHUNK
    lo = CONV_HALO - (DN_CONV - 1)

    @pl.when(step == 0)
    def _():
        s_scr[...] = s0_ref[...]
        ext_scr[lo:CONV_HALO, :] = c0_ref[...]

    ext_scr[CONV_HALO:, 0:DN_QK] = q_ref[...].astype(F32)
    ext_scr[CONV_HALO:, DN_QK:2 * DN_QK] = k_ref[...].astype(F32)
    ext_scr[CONV_HALO:, 2 * DN_QK:] = v_ref[...].astype(F32)

    row = lax.broadcasted_iota(jnp.int32, (c, c), 0)
    col = lax.broadcasted_iota(jnp.int32, (c, c), 1)
    causal = row >= col
    strict = row > col
    eye = (row == col).astype(F32)
    ltri = causal.astype(F32)
    utri = (row <= col).astype(F32)

    ba = ba_ref[...]
    beta_tok = _sigmoid(ba)
    g_tok = -jnp.exp(alog_l_ref[...]) * _softplus(ba + dtb_l_ref[...])
    gc_tok = _mm_hi(ltri, g_tok)
    bat = bat_ref[...]
    g_t = -jnp.exp(alog_c_ref[...]) * _softplus(bat[DN_HEADS:, :] + dtb_c_ref[...])
    gc_t = _mm_hi(g_t, utri)

    def conv(c0):
        acc = ext_scr[lo:lo + c, c0:c0 + DN_DK] * cw_ref[0:1, c0:c0 + DN_DK]
        for i in range(1, DN_CONV):
            acc = acc + ext_scr[lo + i:lo + i + c, c0:c0 + DN_DK] * cw_ref[i:i + 1, c0:c0 + DN_DK]
        return _silu(acc)

    def decay_of(h):
        gc = gc_tok[:, DN_HEADS + h:DN_HEADS + h + 1]
        return gc, jnp.where(causal, jnp.exp(jnp.where(causal, gc - gc_t[h:h + 1, :], 0.0)), 0.0)

    for h in range(DN_HEADS):
        q = conv(h * DN_DK)
        k = conv(DN_QK + h * DN_DK)
        q = q * lax.rsqrt(jnp.sum(q * q, axis=-1, keepdims=True) + EPS) * (DN_DK ** -0.5)
        k = k * lax.rsqrt(jnp.sum(k * k, axis=-1, keepdims=True) + EPS)
        qkv_scr[:, h * DN_DK:(h + 1) * DN_DK] = q
        qkv_scr[:, DN_QK + h * DN_DK:DN_QK + (h + 1) * DN_DK] = k
        qkv_scr[:, 2 * DN_QK + h * DN_DV:2 * DN_QK + (h + 1) * DN_DV] = conv(2 * DN_QK + h * DN_DV)
        _, decay = decay_of(h)
        a_mat = jnp.where(strict, _mm(k * beta_tok[:, h:h + 1], k, NT) * decay, 0.0)
        pow_scr[h] = a_mat
        inv_scr[h] = eye - a_mat

    for _ in range(int(math.log2(c)) - 1):
        for h in range(DN_HEADS):
            a_pow = pow_scr[h]
            pow_scr[h] = _mm_split(_split_lhs(a_pow), _split_rhs(a_pow))
        for h in range(DN_HEADS):
            inv = inv_scr[h]
            inv_scr[h] = inv + _mm_split(_split_lhs(inv), _split_rhs(pow_scr[h]))

    def qkv_of(h):
        return (qkv_scr[:, h * DN_DK:(h + 1) * DN_DK], qkv_scr[:, DN_QK + h * DN_DK:DN_QK + (h + 1) * DN_DK],
                qkv_scr[:, 2 * DN_QK + h * DN_DV:2 * DN_QK + (h + 1) * DN_DV])

    for h in range(DN_HEADS):
        q, k, v = qkv_of(h)
        beta = beta_tok[:, h:h + 1]
        gc, decay = decay_of(h)
        rhs = jnp.concatenate([v * beta, k * beta * jnp.exp(gc)], axis=-1)
        uw_scr[h] = _mm_split(_split_lhs(inv_scr[h]), _split_rhs(rhs))
        pow_scr[h] = jnp.where(causal, _mm(q, k, NT) * decay, 0.0)
    for h in range(DN_HEADS):
        inv_scr[h] = uw_scr[h, :, 0:DN_DV] - _mm(uw_scr[h, :, DN_DV:], s_scr[h])
    for h in range(DN_HEADS):
        q, k, _ = qkv_of(h)
        gc = gc_tok[:, DN_HEADS + h:DN_HEADS + h + 1]
        g_last = gc[c - 1:c, :]
        s = s_scr[h]
        v_new = inv_scr[h]
        o = _mm(q * jnp.exp(gc), s) + _mm(pow_scr[h], v_new)
        s_scr[h] = s * jnp.exp(g_last) + _mm(k * jnp.exp(g_last - gc), v_new, TN)
        z = z_ref[:, h * DN_DV:(h + 1) * DN_DV].astype(F32)
        o_ref[:, h * DN_DV:(h + 1) * DN_DV] = (_rms(o, nw_ref[...]) * _silu(z)).astype(BF16)

    ext_scr[0:CONV_HALO, :] = ext_scr[c:c + CONV_HALO, :]

    @pl.when(step == pl.num_programs(1) - 1)
    def _():
        sout_ref[...] = s_scr[...]
        cout_ref[...] = ext_scr[lo:CONV_HALO, :]


def _lane_pad(v, offset):
    return jnp.zeros((1, BA_W), F32).at[0, offset:offset + v.shape[0]].set(v.astype(F32))


def _delta_prefill(proj, ba, bat, conv_w, a_log, dt_bias, dn_norm_w, s0, c0):
    nb, l, _ = proj.shape
    c = CHUNK
    col_spec = lambda j: pl.BlockSpec((None, c, D_MODEL), lambda b, i: (b, i, j))
    full = lambda shape: pl.BlockSpec(shape, lambda b, i: (0,) * len(shape))
    return pl.pallas_call(
        _delta_prefill_kernel,
        out_shape=(jax.ShapeDtypeStruct((nb, l, DN_VW), BF16),
                   jax.ShapeDtypeStruct((nb, DN_HEADS, DN_DK, DN_DV), F32),
                   jax.ShapeDtypeStruct((nb, DN_CONV - 1, DN_CONV_W), F32)),
        grid=(nb, l // c),
        in_specs=[col_spec(COL_Q_A), col_spec(COL_K_A), col_spec(COL_V_A), col_spec(COL_Z_A),
                  pl.BlockSpec((None, c, BA_W), lambda b, i: (b, i, 0)),
                  pl.BlockSpec((None, 2 * DN_HEADS, c), lambda b, i: (b, 0, i)),
                  full((DN_CONV, DN_CONV_W)), full((1, BA_W)), full((1, BA_W)),
                  full((DN_HEADS, 1)), full((DN_HEADS, 1)), full((1, DN_DV)),
                  pl.BlockSpec((None, DN_HEADS, DN_DK, DN_DV), lambda b, i: (b, 0, 0, 0)),
                  pl.BlockSpec((None, DN_CONV - 1, DN_CONV_W), lambda b, i: (b, 0, 0))],
        out_specs=(pl.BlockSpec((None, c, DN_VW), lambda b, i: (b, i, 0)),
                   pl.BlockSpec((None, DN_HEADS, DN_DK, DN_DV), lambda b, i: (b, 0, 0, 0)),
                   pl.BlockSpec((None, DN_CONV - 1, DN_CONV_W), lambda b, i: (b, 0, 0))),
        scratch_shapes=[pltpu.VMEM((c + CONV_HALO, DN_CONV_W), F32),
                        pltpu.VMEM((DN_HEADS, DN_DK, DN_DV), F32),
                        pltpu.VMEM((c, DN_CONV_W), F32),
                        pltpu.VMEM((DN_HEADS, c, c), F32),
                        pltpu.VMEM((DN_HEADS, c, c), F32),
                        pltpu.VMEM((DN_HEADS, c, DN_DV + DN_DK), F32)],
        compiler_params=_params("parallel", "arbitrary"),
        name="delta_prefill",
    )(proj, proj, proj, proj, ba, bat, conv_w, _lane_pad(a_log, DN_HEADS), _lane_pad(dt_bias, DN_HEADS),
      a_log.reshape(DN_HEADS, 1), dt_bias.reshape(DN_HEADS, 1), dn_norm_w.reshape(1, DN_DV), s0, c0)


def _log_gamma(h):
    return math.log(1.0 - 2.0 ** (-5.0 - h))


def _rotate(x, cos, sin):
    half = x.shape[-1] // 2
    x1, x2 = x[:, :half], x[:, half:]
    return jnp.concatenate([x1 * cos - x2 * sin, x1 * sin + x2 * cos], axis=-1)


def _group_norm_gate(o, gw, gb, gate):
    mu = jnp.mean(o, axis=-1, keepdims=True)
    var = jnp.mean(jnp.square(o - mu), axis=-1, keepdims=True)
    return ((o - mu) * lax.rsqrt(var + EPS) * gw + gb) * _silu(gate)


def _ret_prefill_kernel(q_ref, k_ref, v_ref, g_ref, inv_ref, gw_ref, gb_ref, s0_ref, o_ref, sout_ref, s_scr):
    step = pl.program_id(1)
    c = CHUNK

    @pl.when(step == 0)
    def _():
        s_scr[...] = s0_ref[...]

    idx_c = lax.broadcasted_iota(jnp.int32, (c, 1), 0)
    pos = (step * c + idx_c).astype(F32)
    ang = pos * inv_ref[...]
    cos, sin = jnp.cos(ang), jnp.sin(ang)
    rel = (lax.broadcasted_iota(jnp.int32, (c, c), 0) - lax.broadcasted_iota(jnp.int32, (c, c), 1)).astype(F32)
    idx = idx_c.astype(F32)

    for h in range(RET_HEADS):
        lg = _log_gamma(h)
        q = _rotate(q_ref[:, h * RET_DK:(h + 1) * RET_DK].astype(F32), cos, sin)
        k = _rotate(k_ref[:, h * RET_DK:(h + 1) * RET_DK].astype(F32), cos, sin) * (RET_DK ** -0.5)
        v = v_ref[:, h * RET_DV:(h + 1) * RET_DV]
        d_mat = jnp.where(rel >= 0, jnp.exp(jnp.maximum(rel, 0.0) * lg), 0.0)
        intra = _mm(_mm(q, k, NT) * d_mat, v)
        s = s_scr[h]
        cross = _mm(q * jnp.exp((idx + 1.0) * lg), s)
        s_scr[h] = s * math.exp(c * lg) + _mm(k * jnp.exp((c - 1.0 - idx) * lg), v, TN)
        sl = slice(h * RET_DV, (h + 1) * RET_DV)
        o_ref[:, sl] = _group_norm_gate(intra + cross, gw_ref[:, sl], gb_ref[:, sl],
                                        g_ref[:, sl].astype(F32)).astype(BF16)

    @pl.when(step == pl.num_programs(1) - 1)
    def _():
        sout_ref[...] = s_scr[...]


def _ret_prefill(proj, inv_freq, gn_w, gn_b, s0):
    nb, l, _ = proj.shape
    c = CHUNK
    full = lambda shape: pl.BlockSpec(shape, lambda b, i: (0,) * len(shape))
    state_spec = pl.BlockSpec((None, RET_HEADS, RET_DK, RET_DV), lambda b, i: (b, 0, 0, 0))
    return pl.pallas_call(
        _ret_prefill_kernel,
        out_shape=(jax.ShapeDtypeStruct((nb, l, RET_VW), BF16),
                   jax.ShapeDtypeStruct((nb, RET_HEADS, RET_DK, RET_DV), F32)),
        grid=(nb, l // c),
        in_specs=[pl.BlockSpec((None, c, RET_QK), lambda b, i: (b, i, COL_Q_B)),
                  pl.BlockSpec((None, c, RET_QK), lambda b, i: (b, i, COL_K_B)),
                  pl.BlockSpec((None, c, RET_VW), lambda b, i: (b, i, COL_V_B)),
                  pl.BlockSpec((None, c, RET_VW), lambda b, i: (b, i, COL_G_B)),
                  full((1, RET_DK // 2)), full((1, RET_VW)), full((1, RET_VW)), state_spec],
        out_specs=(pl.BlockSpec((None, c, RET_VW), lambda b, i: (b, i, 0)), state_spec),
        scratch_shapes=[pltpu.VMEM((RET_HEADS, RET_DK, RET_DV), F32)],
        compiler_params=_params("parallel", "arbitrary"),
        name="retention_prefill",
    )(proj, proj, proj, proj, inv_freq, gn_w.reshape(1, RET_VW), gn_b.reshape(1, RET_VW), s0)


ROWS = 8


def _row_mm(a, s):
    return _mm(jnp.broadcast_to(a, (ROWS, a.shape[-1])), s)[0:1, :]


def _outer(a, b):
    first = lax.broadcasted_iota(jnp.int32, (ROWS, a.shape[-1]), 0) == 0
    a8 = jnp.where(first, jnp.broadcast_to(a, (ROWS, a.shape[-1])), 0.0)
    return _mm(a8, jnp.broadcast_to(b, (ROWS, b.shape[-1])), TN)


def _delta_step_kernel(qkv_ref, z_ref, ba_ref, cw_ref, alog_ref, dtb_ref, nw_ref, s_ref, c_ref,
                       o_ref, sout_ref, cout_ref):
    u_new = qkv_ref[...].astype(F32)
    buf = c_ref[...]
    acc = u_new * cw_ref[DN_CONV - 1:DN_CONV, :]
    for i in range(DN_CONV - 1):
        acc = acc + buf[i:i + 1, :] * cw_ref[i:i + 1, :]
    qkv = _silu(acc)
    cout_ref[0:DN_CONV - 2, :] = buf[1:, :]
    cout_ref[DN_CONV - 2:, :] = u_new
    ba = ba_ref[...]
    beta_all = _sigmoid(ba)
    g_all = -jnp.exp(alog_ref[...]) * _softplus(ba + dtb_ref[...])
    for h in range(DN_HEADS):
        q = qkv[:, h * DN_DK:(h + 1) * DN_DK]
        k = qkv[:, DN_QK + h * DN_DK:DN_QK + (h + 1) * DN_DK]
        v = qkv[:, 2 * DN_QK + h * DN_DV:2 * DN_QK + (h + 1) * DN_DV]
        q = q * lax.rsqrt(jnp.sum(q * q, axis=-1, keepdims=True) + EPS) * (DN_DK ** -0.5)
        k = k * lax.rsqrt(jnp.sum(k * k, axis=-1, keepdims=True) + EPS)
        beta = beta_all[:, h:h + 1]
        eg = jnp.exp(g_all[:, DN_HEADS + h:DN_HEADS + h + 1])
        s = s_ref[h]
        kb = k * beta
        v_new = v * beta - _row_mm(kb * eg, s)
        o = _row_mm(q * eg, s) + jnp.sum(q * k, axis=-1, keepdims=True) * v_new
        sout_ref[h] = s * eg + _outer(k, v_new)
        z = z_ref[:, h * DN_DV:(h + 1) * DN_DV].astype(F32)
        o_ref[:, h * DN_DV:(h + 1) * DN_DV] = (_rms(o, nw_ref[...]) * _silu(z)).astype(BF16)


def _delta_step(proj, ba, conv_w, a_log, dt_bias, dn_norm_w, s0, c0):
    n = proj.shape[0]
    full = lambda shape: pl.BlockSpec(shape, lambda b: (0,) * len(shape))
    state_spec = pl.BlockSpec((None, DN_HEADS, DN_DK, DN_DV), lambda b: (b, 0, 0, 0))
    conv_spec = pl.BlockSpec((None, DN_CONV - 1, DN_CONV_W), lambda b: (b, 0, 0))
    return pl.pallas_call(
        _delta_step_kernel,
        out_shape=(jax.ShapeDtypeStruct((n, 1, DN_VW), BF16),
                   jax.ShapeDtypeStruct((n, DN_HEADS, DN_DK, DN_DV), F32),
                   jax.ShapeDtypeStruct((n, DN_CONV - 1, DN_CONV_W), F32)),
        grid=(n,),
        in_specs=[pl.BlockSpec((None, 1, DN_CONV_W), lambda b: (b, 0, 0)),
                  pl.BlockSpec((None, 1, DN_VW), lambda b: (b, 0, COL_Z_A)),
                  pl.BlockSpec((None, 1, BA_W), lambda b: (b, 0, 0)),
                  full((DN_CONV, DN_CONV_W)), full((1, BA_W)), full((1, BA_W)), full((1, DN_DV)),
                  state_spec, conv_spec],
        out_specs=(pl.BlockSpec((None, 1, DN_VW), lambda b: (b, 0, 0)), state_spec, conv_spec),
        compiler_params=_params("parallel"),
        name="delta_step",
    )(proj, proj, ba, conv_w, _lane_pad(a_log, DN_HEADS), _lane_pad(dt_bias, DN_HEADS),
      dn_norm_w.reshape(1, DN_DV), s0, c0)


def _ret_step_kernel(q_ref, k_ref, v_ref, g_ref, inv_ref, gw_ref, gb_ref, s_ref, o_ref, sout_ref):
    ang = float(PAST_LEN) * inv_ref[...]
    cos, sin = jnp.cos(ang), jnp.sin(ang)
    for h in range(RET_HEADS):
        gamma = math.exp(_log_gamma(h))
        q = _rotate(q_ref[:, h * RET_DK:(h + 1) * RET_DK].astype(F32), cos, sin)
        k = _rotate(k_ref[:, h * RET_DK:(h + 1) * RET_DK].astype(F32), cos, sin) * (RET_DK ** -0.5)
        v = v_ref[:, h * RET_DV:(h + 1) * RET_DV].astype(F32)
        s = s_ref[h]
        o = jnp.sum(q * k, axis=-1, keepdims=True) * v + _row_mm(q * gamma, s)
        sout_ref[h] = s * gamma + _outer(k, v)
        sl = slice(h * RET_DV, (h + 1) * RET_DV)
        o_ref[:, sl] = _group_norm_gate(o, gw_ref[:, sl], gb_ref[:, sl], g_ref[:, sl].astype(F32)).astype(BF16)


def _ret_step(proj, inv_freq, gn_w, gn_b, s0):
    n = proj.shape[0]
    full = lambda shape: pl.BlockSpec(shape, lambda b: (0,) * len(shape))
    state_spec = pl.BlockSpec((None, RET_HEADS, RET_DK, RET_DV), lambda b: (b, 0, 0, 0))
    return pl.pallas_call(
        _ret_step_kernel,
        out_shape=(jax.ShapeDtypeStruct((n, 1, RET_VW), BF16),
                   jax.ShapeDtypeStruct((n, RET_HEADS, RET_DK, RET_DV), F32)),
        grid=(n,),
        in_specs=[pl.BlockSpec((None, 1, RET_QK), lambda b: (b, 0, COL_Q_B)),
                  pl.BlockSpec((None, 1, RET_QK), lambda b: (b, 0, COL_K_B)),
                  pl.BlockSpec((None, 1, RET_VW), lambda b: (b, 0, COL_V_B)),
                  pl.BlockSpec((None, 1, RET_VW), lambda b: (b, 0, COL_G_B)),
                  full((1, RET_DK // 2)), full((1, RET_VW)), full((1, RET_VW)), state_spec],
        out_specs=(pl.BlockSpec((None, 1, RET_VW), lambda b: (b, 0, 0)), state_spec),
        compiler_params=_params("parallel"),
        name="retention_step",
    )(proj, proj, proj, proj, inv_freq, gn_w.reshape(1, RET_VW), gn_b.reshape(1, RET_VW), s0)


def _post_mixer_kernel(oa_ref, ob_ref, ga_ref, gb_ref, x_ref, g1_ref, sh2_ref, sc2_ref, wda_ref, wdb_ref,
                       wout_ref, nw_ref, rw_ref, x1_ref, h_ref, hf_ref, logit_ref):
    y_a = _mm(oa_ref[...], wda_ref[...])
    y_b = _mm(ob_ref[...], wdb_ref[...])
    merged = _sigmoid(ga_ref[...].astype(F32)) * y_a + _sigmoid(gb_ref[...].astype(F32)) * y_b
    x1 = x_ref[...] + g1_ref[...] * _mm(merged, wout_ref[...])
    x1_ref[...] = x1
    hf = _rms(x1, nw_ref[...]) * (1.0 + sc2_ref[...]) + sh2_ref[...]
    h_ref[...] = hf.astype(BF16)
    hf_ref[...] = hf.reshape(hf_ref.shape)
    logit_ref[...] = _mm_hi(rw_ref[...], hf, NT)


def _post_mixer(o_a, o_b, proj, x, mod, per_token, w_down_a, w_down_b, w_out, norm_w, router_w, tm):
    nb, l, _ = x.shape
    tok = lambda w, j=0: pl.BlockSpec((None, tm, w), lambda b, i: (b, i, j))
    full = lambda shape: pl.BlockSpec(shape, lambda b, i: (0,) * len(shape))
    return pl.pallas_call(
        _post_mixer_kernel,
        out_shape=(jax.ShapeDtypeStruct((nb, l, D_MODEL), F32),
                   jax.ShapeDtypeStruct((nb, l, D_MODEL), BF16),
                   jax.ShapeDtypeStruct((nb, l) + TOKEN_TILE, F32),
                   jax.ShapeDtypeStruct((nb, N_EXPERTS, l), F32)),
        grid=(nb, l // tm),
        in_specs=[tok(DN_VW), tok(RET_VW), tok(D_MODEL, COL_GATE_A), tok(D_MODEL, COL_GATE_B), tok(D_MODEL),
                  _mod_spec(per_token, tm, 2), _mod_spec(per_token, tm, 3), _mod_spec(per_token, tm, 4),
                  full((DN_VW, D_MODEL)), full((RET_VW, D_MODEL)), full((D_MODEL, D_MODEL)),
                  full((1, D_MODEL)), full((N_EXPERTS, D_MODEL))],
        out_specs=(tok(D_MODEL), tok(D_MODEL),
                   pl.BlockSpec((None, tm) + TOKEN_TILE, lambda b, i: (b, i, 0, 0)),
                   pl.BlockSpec((None, N_EXPERTS, tm), lambda b, i: (b, 0, i))),
        compiler_params=_params("parallel", "parallel"),
        name="post_mixer",
    )(o_a, o_b, proj, proj, x, mod, mod, mod, w_down_a, w_down_b, w_out, norm_w, router_w)


def _first_max(x, axis, n):
    m = jnp.max(x, axis=axis, keepdims=True)
    ids = lax.broadcasted_iota(jnp.int32, x.shape, axis)
    first = jnp.min(jnp.where(x == m, ids, n), axis=axis, keepdims=True)
    return m, ids == first


def _eye(n):
    return (lax.broadcasted_iota(jnp.int32, (n, n), 0) == lax.broadcasted_iota(jnp.int32, (n, n), 1)).astype(F32)


def _route_kernel(logit_ref, bias_ref, ek_ref, rk_ref, wkt_ref, cnt_ref, run_scr):
    @pl.when(jnp.logical_and(pl.program_id(0) == 0, pl.program_id(1) == 0))
    def _():
        run_scr[...] = jnp.zeros_like(run_scr)

    t = logit_ref.shape[-1]
    scores = _sigmoid(logit_ref[...])
    biased = scores + bias_ref[...]
    neg = jnp.float32(-jnp.inf)
    b3 = biased.reshape(N_GROUPS, GROUP_SIZE, t)
    m1, hit = _first_max(b3, 1, GROUP_SIZE)
    m2 = jnp.max(jnp.where(hit, neg, b3), axis=1, keepdims=True)
    gscore = (m1 + m2).reshape(N_GROUPS, t)
    gmask = jnp.zeros((N_GROUPS, t), jnp.bool_)
    for _ in range(TOPK_GROUPS):
        _, hit = _first_max(gscore, 0, N_GROUPS)
        gmask = jnp.logical_or(gmask, hit)
        gscore = jnp.where(hit, neg, gscore)
    emask = jnp.broadcast_to(gmask.reshape(N_GROUPS, 1, t), (N_GROUPS, GROUP_SIZE, t)).reshape(N_EXPERTS, t)
    masked = jnp.where(emask, biased, neg)
    eid = lax.broadcasted_iota(jnp.int32, (N_EXPERTS, t), 0).astype(F32)
    hits, e_rows, w_rows = [], [], []
    for _ in range(TOP_K):
        _, hit = _first_max(masked, 0, N_EXPERTS)
        hits.append(hit)
        e_rows.append(jnp.sum(jnp.where(hit, eid, 0.0), axis=0, keepdims=True))
        w_rows.append(jnp.sum(jnp.where(hit, scores, 0.0), axis=0, keepdims=True))
        masked = jnp.where(hit, neg, masked)
    sel = sum(hit.astype(F32) for hit in hits)
    before = lax.broadcasted_iota(jnp.int32, (t, t), 0) < lax.broadcasted_iota(jnp.int32, (t, t), 1)
    rank = run_scr[:, 0:1] + _mm(sel, before.astype(F32))
    rk_rows = [jnp.sum(jnp.where(hit, rank, 0.0), axis=0, keepdims=True) for hit in hits]
    run_scr[...] = run_scr[...] + jnp.sum(sel, axis=1, keepdims=True)
    wk = jnp.concatenate(w_rows, axis=0)
    wk = wk / jnp.sum(wk, axis=0, keepdims=True) * ROUTED_SCALE
    ek_ref[...] = jnp.concatenate(e_rows, axis=0).astype(jnp.int32)
    rk_ref[...] = jnp.concatenate(rk_rows, axis=0).astype(jnp.int32)
    wkt_ref[...] = _mm_hi(wk, _eye(TOP_K), TN)
    cnt_ref[...] = run_scr[...]


def _route(logits_t, router_bias, tr):
    nb, _, l = logits_t.shape
    pick = pl.BlockSpec((None, TOP_K, tr), lambda b, i: (b, 0, i))
    return pl.pallas_call(
        _route_kernel,
        out_shape=(jax.ShapeDtypeStruct((nb, TOP_K, l), jnp.int32),
                   jax.ShapeDtypeStruct((nb, TOP_K, l), jnp.int32),
                   jax.ShapeDtypeStruct((nb, l, TOP_K), F32),
                   jax.ShapeDtypeStruct((N_EXPERTS, LANES), F32)),
        grid=(nb, l // tr),
        in_specs=[pl.BlockSpec((None, N_EXPERTS, tr), lambda b, i: (b, 0, i)),
                  pl.BlockSpec((N_EXPERTS, 1), lambda b, i: (0, 0))],
        out_specs=(pick, pick, pl.BlockSpec((None, tr, TOP_K), lambda b, i: (b, i, 0)),
                   pl.BlockSpec((N_EXPERTS, LANES), lambda b, i: (0, 0))),
        scratch_shapes=[pltpu.VMEM((N_EXPERTS, LANES), F32)],
        compiler_params=_params("arbitrary", "arbitrary"),
        name="route",
    )(logits_t, router_bias.reshape(N_EXPERTS, 1))


def _plan_kernel(cnt_ref, ek_ref, rk_ref, pos_ref, be_ref, pe_ref, nu_ref, start_scr):
    @pl.when(jnp.logical_and(pl.program_id(0) == 0, pl.program_id(1) == 0))
    def _():
        padded = jnp.ceil(cnt_ref[...] * (1.0 / MOE_BLOCK)) * MOE_BLOCK
        r = lax.broadcasted_iota(jnp.int32, (N_EXPERTS, N_EXPERTS), 0)
        c = lax.broadcasted_iota(jnp.int32, (N_EXPERTS, N_EXPERTS), 1)
        pad_end = _mm_hi((r >= c).astype(F32), padded)
        start_scr[...] = pad_end - padded
        pe_ref[...] = pad_end.astype(jnp.int32)
        nu_ref[...] = (pad_end[N_EXPERTS - 1:, :] * (1.0 / MOE_BLOCK)).astype(jnp.int32)
        nbp = be_ref.shape[-1]
        first_row = lax.broadcasted_iota(jnp.int32, (1, nbp), 1).astype(F32) * MOE_BLOCK
        owner = jnp.sum((pad_end[:, 0:1] <= first_row).astype(F32), axis=0, keepdims=True)
        be_ref[...] = jnp.minimum(owner, N_EXPERTS - 1.0).astype(jnp.int32)

    t = ek_ref.shape[-1]
    eid = lax.broadcasted_iota(jnp.int32, (N_EXPERTS, t), 0)
    start = start_scr[:, 0:1]
    rows = [jnp.sum(jnp.where(eid == ek_ref[k:k + 1, :], start, 0.0), axis=0, keepdims=True)
            for k in range(TOP_K)]
    pos_ref[...] = jnp.concatenate(rows, axis=0).astype(jnp.int32) + rk_ref[...]


def _plan(counts, ek, rk, tr, n_blocks):
    nb, _, l = ek.shape
    nbp = -(-n_blocks // LANES) * LANES
    pick = pl.BlockSpec((None, TOP_K, tr), lambda b, i: (b, 0, i))
    const = lambda shape: pl.BlockSpec(shape, lambda b, i: (0, 0))
    return pl.pallas_call(
        _plan_kernel,
        out_shape=(jax.ShapeDtypeStruct((nb, TOP_K, l), jnp.int32),
                   jax.ShapeDtypeStruct((1, nbp), jnp.int32),
                   jax.ShapeDtypeStruct((N_EXPERTS, LANES), jnp.int32),
                   jax.ShapeDtypeStruct((1, LANES), jnp.int32)),
        grid=(nb, l // tr),
        in_specs=[const((N_EXPERTS, LANES)), pick, pick],
        out_specs=(pick, const((1, nbp)), const((N_EXPERTS, LANES)), const((1, LANES))),
        scratch_shapes=[pltpu.VMEM((N_EXPERTS, LANES), F32)],
        compiler_params=_params("arbitrary", "arbitrary"),
        name="plan",
    )(counts, ek, rk)


def _row_copy(src_hbm, src_row, dst_ref, dst_row, sem):
    return pltpu.make_async_copy(src_hbm.at[src_row], dst_ref.at[dst_row], sem)


def _dispatch_kernel(pe_ref, pos_ref, x_hbm, xs_hbm, zero_scr, sem):
    b, i = pl.program_id(0), pl.program_id(1)
    tm = pos_ref.shape[-1]
    base = (b * pl.num_programs(1) + i) * tm

    @pl.when(jnp.logical_and(b == 0, i == 0))
    def _():
        zero_scr[...] = jnp.zeros_like(zero_scr)

        def tail_copy(e):
            return pltpu.make_async_copy(zero_scr, xs_hbm.at[pl.ds(pe_ref[e] - MOE_BLOCK, MOE_BLOCK)], sem)

        def nonempty(e):
            return pe_ref[e] > jnp.where(e > 0, pe_ref[jnp.maximum(e - 1, 0)], 0)

        @pl.loop(0, N_EXPERTS)
        def _(e):
            @pl.when(nonempty(e))
            def _():
                tail_copy(e).start()

        @pl.loop(0, N_EXPERTS)
        def _(e):
            @pl.when(nonempty(e))
            def _():
                tail_copy(e).wait()

    @pl.loop(0, tm)
    def _(t):
        for k in range(TOP_K):
            _row_copy(x_hbm, base + t, xs_hbm, pos_ref[k, t], sem).start()

    @pl.loop(0, tm)
    def _(t):
        for k in range(TOP_K):
            _row_copy(x_hbm, base + t, xs_hbm, pos_ref[k, t], sem).wait()


def _dispatch(x, pos, pad_end, n_rows, tm):
    nb, l = x.shape[:2]
    return pl.pallas_call(
        _dispatch_kernel,
        out_shape=jax.ShapeDtypeStruct((n_rows,) + TOKEN_TILE, F32),
        grid_spec=pltpu.PrefetchScalarGridSpec(
            num_scalar_prefetch=1,
            grid=(nb, l // tm),
            in_specs=[pl.BlockSpec((None, TOP_K, tm), lambda b, i, pe: (b, 0, i), memory_space=pltpu.SMEM),
                      pl.BlockSpec(memory_space=pl.ANY)],
            out_specs=pl.BlockSpec(memory_space=pl.ANY),
            scratch_shapes=[pltpu.VMEM((MOE_BLOCK,) + TOKEN_TILE, F32), pltpu.SemaphoreType.DMA(())]),
        compiler_params=_params("arbitrary", "arbitrary"),
        name="dispatch",
    )(pad_end, pos, x.reshape((nb * l,) + TOKEN_TILE))


def _experts_kernel(be_ref, nu_ref, x_ref, wg_ref, wu_ref, wd_ref, y_ref):
    @pl.when(pl.program_id(0) < nu_ref[0])
    def _():
        xb = x_ref[...].reshape(MOE_BLOCK, D_MODEL).astype(BF16)
        y = _mm(_silu(_mm(xb, wg_ref[...])) * _mm(xb, wu_ref[...]), wd_ref[...])
        y_ref[...] = y.reshape(y_ref.shape)


def _experts(xs, block_expert, n_used, w_gate, w_up, w_down):
    n_rows, d = xs.shape[0], D_MODEL
    used = lambda i, nu: jnp.minimum(i, nu[0] - 1)
    rows = pl.BlockSpec((MOE_BLOCK,) + TOKEN_TILE, lambda i, be, nu: (used(i, nu), 0, 0))
    return pl.pallas_call(
        _experts_kernel,
        out_shape=jax.ShapeDtypeStruct((n_rows,) + TOKEN_TILE, F32),
        grid_spec=pltpu.PrefetchScalarGridSpec(
            num_scalar_prefetch=2,
            grid=(n_rows // MOE_BLOCK,),
            in_specs=[rows,
                      pl.BlockSpec((None, d, D_EXPERT), lambda i, be, nu: (be[used(i, nu)], 0, 0)),
                      pl.BlockSpec((None, d, D_EXPERT), lambda i, be, nu: (be[used(i, nu)], 0, 0)),
                      pl.BlockSpec((None, D_EXPERT, d), lambda i, be, nu: (be[used(i, nu)], 0, 0))],
            out_specs=rows),
        compiler_params=_params("arbitrary"),
        name="experts",
    )(block_expert, n_used, xs, w_gate, w_up, w_down)


def _combine_kernel(pos_ref, wkt_ref, ys_hbm, h_ref, x1_ref, g2_ref, wsg_ref, wsu_ref, wsd_ref, fw_ref,
                    y_ref, rows_scr, sem):
    tm = pos_ref.shape[-1]

    def row(k, t):
        return pltpu.make_async_copy(ys_hbm.at[pos_ref[k, t]], rows_scr.at[k, t], sem)

    @pl.loop(0, tm)
    def _(t):
        for k in range(TOP_K):
            row(k, t).start()

    hb = h_ref[...]
    acc = _mm(_silu(_mm(hb, wsg_ref[...])) * _mm(hb, wsu_ref[...]), wsd_ref[...])

    @pl.loop(0, tm)
    def _(t):
        for k in range(TOP_K):
            row(k, t).wait()

    wkt = wkt_ref[...]
    for k in range(TOP_K):
        acc = acc + rows_scr[k].reshape(tm, D_MODEL) * wkt[:, k:k + 1]
    y_ref[...] = _rms(x1_ref[...] + g2_ref[...] * acc, fw_ref[...])


def _combine(ys, pos, wkt, hffn, x1, mod, per_token, ws_gate, ws_up, ws_down, final_w, tm):
    nb, l, d = x1.shape
    tok = lambda w: pl.BlockSpec((None, tm, w), lambda b, i: (b, i, 0))
    full = lambda shape: pl.BlockSpec(shape, lambda b, i: (0,) * len(shape))
    return pl.pallas_call(
        _combine_kernel,
        out_shape=jax.ShapeDtypeStruct((nb, l, d), F32),
        grid=(nb, l // tm),
        in_specs=[pl.BlockSpec((None, TOP_K, tm), lambda b, i: (b, 0, i), memory_space=pltpu.SMEM),
                  tok(TOP_K), pl.BlockSpec(memory_space=pl.ANY), tok(d), tok(d), _mod_spec(per_token, tm, 5),
                  full((d, D_SHARED)), full((d, D_SHARED)), full((D_SHARED, d)), full((1, d))],
        out_specs=tok(d),
        scratch_shapes=[pltpu.VMEM((TOP_K, tm) + TOKEN_TILE, F32), pltpu.SemaphoreType.DMA(())],
        compiler_params=_params("parallel", "parallel"),
        name="combine",
    )(pos, wkt, ys, hffn, x1, mod, ws_gate, ws_up, ws_down, final_w)


def _moe(hffn, hf32, logits_t, x1, mod, per_token, w, tr, tm_dispatch, tm_combine):
    nb, l, _ = x1.shape
    n_blocks = nb * l * TOP_K // MOE_BLOCK + N_EXPERTS
    ek, rk, wkt, counts = _route(logits_t, w["router_bias"], tr)
    pos, block_expert, pad_end, n_used = _plan(counts, ek, rk, tr, n_blocks)
    xs = _dispatch(hf32, pos, pad_end[:, 0], n_blocks * MOE_BLOCK, tm_dispatch)
    ys = _experts(xs, block_expert[0, :n_blocks], n_used[0, :1], w["w_gate"], w["w_up"], w["w_down"])
    return _combine(ys, pos, wkt, hffn, x1, mod, per_token, w["ws_gate"], w["ws_up"], w["ws_down"],
                    w["final_norm"], tm_combine)


def _trunk(x, mod, per_token, states, w):
    nb, l, _ = x.shape
    tm_proj, tn_proj, tm_post, tr, tm_moe = min(l, 1024), 1024, min(l, 512), min(l, 512), min(l, 1024)
    proj, ba, bat = _in_projection(x, mod, per_token, w["norm_mix"], w["w_main"], w["w_ba"], w["w_bat"],
                                   tm_proj, tn_proj)
    if states is None:
        c0 = jnp.zeros((nb, DN_CONV - 1, DN_CONV_W), F32)
        sd0 = jnp.zeros((nb, DN_HEADS, DN_DK, DN_DV), F32)
        sr0 = jnp.zeros((nb, RET_HEADS, RET_DK, RET_DV), F32)
        o_a, sd, cb = _delta_prefill(proj, ba, bat, w["conv_w"], w["a_log"], w["dt_bias"], w["dn_norm"], sd0, c0)
        o_b, sr = _ret_prefill(proj, w["inv_freq"], w["gn_w"], w["gn_b"], sr0)
    else:
        c0, sd0, sr0 = states
        n = nb * l
        o_a, sd, cb = _delta_step(proj.reshape(n, 1, PROJ_W), ba.reshape(n, 1, BA_W), w["conv_w"], w["a_log"],
                                  w["dt_bias"], w["dn_norm"], sd0, c0)
        o_b, sr = _ret_step(proj.reshape(n, 1, PROJ_W), w["inv_freq"], w["gn_w"], w["gn_b"], sr0)
        o_a = o_a.reshape(nb, l, DN_VW)
        o_b = o_b.reshape(nb, l, RET_VW)
    x1, hffn, hf32, logits_t = _post_mixer(o_a, o_b, proj, x, mod, per_token, w["w_down_a"], w["w_down_b"],
                                           w["w_out"], w["norm_ffn"], w["router_w"], tm_post)
    y = _moe(hffn, hf32, logits_t, x1, mod, per_token, w, tr, min(l, 512), min(l, 128))
    return y, cb, sd, sr


def kernel(x_prompt, x_sample, state_conv, state_delta, state_ret, c_prompt, c_sample, w_mod, b_mod, norm_mix_w, w_in, conv_w, a_log, dt_bias, dn_norm_w, ret_gn_w, ret_gn_b, w_down_a, w_down_b, w_out, norm_ffn_w, router_w, router_bias, w_gate, w_up, w_down, ws_gate, ws_up, ws_down, final_norm_w):
    bp, lp, _ = x_prompt.shape
    bs = x_sample.shape[0]
    half = RET_DK // 2
    w_in0 = w_in[0]
    c0, c1 = DN_CONV_W, DN_CONV_W + 2 * DN_HEADS
    w_ba = jnp.pad(w_in0[:, c0:c1], ((0, 0), (0, BA_W - 2 * DN_HEADS))).astype(BF16)
    w = {
        "norm_mix": norm_mix_w[0].reshape(1, D_MODEL),
        "w_main": jnp.concatenate([w_in0[:, :c0], w_in0[:, c1:]], axis=1).astype(BF16),
        "w_ba": w_ba,
        "w_bat": w_in0[:, c0:c1].T.astype(BF16),
        "conv_w": conv_w[0], "a_log": a_log[0], "dt_bias": dt_bias[0], "dn_norm": dn_norm_w[0],
        "inv_freq": (ROPE_BASE ** (-jnp.arange(half, dtype=F32) / half)).reshape(1, half),
        "gn_w": ret_gn_w[0], "gn_b": ret_gn_b[0],
        "w_down_a": w_down_a[0].astype(BF16), "w_down_b": w_down_b[0].astype(BF16), "w_out": w_out[0].astype(BF16),
        "norm_ffn": norm_ffn_w[0].reshape(1, D_MODEL), "router_w": router_w[0], "router_bias": router_bias[0],
        "w_gate": w_gate[0].astype(BF16), "w_up": w_up[0].astype(BF16), "w_down": w_down[0].astype(BF16),
        "ws_gate": ws_gate[0].astype(BF16), "ws_up": ws_up[0].astype(BF16), "ws_down": ws_down[0].astype(BF16),
        "final_norm": final_norm_w.reshape(1, D_MODEL),
    }
    mod = _modulation(jnp.concatenate([c_prompt, c_sample], axis=0), w_mod[0], b_mod[0])
    mod_p = mod[:bp].reshape(bp, 1, MOD_CHUNKS * D_MODEL)
    mod_s = mod[bp:].reshape(1, bs, MOD_CHUNKS * D_MODEL)

    y_p, conv_p, delta_p, ret_p = _trunk(x_prompt, mod_p, False, None, w)
    y_s, conv_s, delta_s, ret_s = _trunk(x_sample.reshape(1, bs, D_MODEL), mod_s, True,
                                         (state_conv[0], state_delta[0], state_ret[0]), w)
    return (y_p, y_s.reshape(bs, 1, D_MODEL), conv_p[None], delta_p[None], ret_p[None],
            conv_s[None], delta_s[None], ret_s[None])
```

```python
import functools
import math

import jax
import jax.numpy as jnp
from jax import lax
from jax.experimental import pallas as pl
from jax.experimental.pallas import tpu as pltpu

F32 = jnp.float32
BF16 = jnp.bfloat16

D_MODEL = 1024
DN_HEADS, DN_DK, DN_DV, DN_CONV = 8, 128, 128, 4
DN_QK = DN_HEADS * DN_DK
DN_VW = DN_HEADS * DN_DV
DN_CONV_W = 2 * DN_QK + DN_VW
RET_HEADS, RET_DK, RET_DV = 4, 256, 512
RET_QK = RET_HEADS * RET_DK
RET_VW = RET_HEADS * RET_DV
ROPE_BASE = 10000.0
PAST_LEN = 16384
N_EXPERTS, TOP_K, N_GROUPS, TOPK_GROUPS = 64, 8, 8, 4
GROUP_SIZE = N_EXPERTS // N_GROUPS
D_EXPERT, D_SHARED = 256, 256
ROUTED_SCALE = 2.5
MOD_CHUNKS = 6
EPS = 1e-6

PROJ_W = 12 * D_MODEL
COL_Q_A, COL_K_A, COL_V_A, COL_Z_A = 0, 1, 2, 3
COL_Q_B, COL_K_B = 4, 5
COL_V_B, COL_G_B = 3, 4
COL_GATE_A, COL_GATE_B = 10, 11
BA_W = 128
CHUNK = 128
CONV_HALO = 8
VMEM_LIMIT = 56 * 1024 * 1024
LANES = 128
MOE_BLOCK = 256
TOKEN_TILE = (D_MODEL // LANES, LANES)

NN = (((1,), (0,)), ((), ()))
NT = (((1,), (1,)), ((), ()))
TN = (((0,), (0,)), ((), ()))


def _mm(a, b, dims=NN):
    return lax.dot_general(a.astype(BF16), b.astype(BF16), dims, preferred_element_type=F32)


def _mm_hi(a, b, dims=NN):
    return lax.dot_general(a.astype(F32), b.astype(F32), dims, precision=lax.Precision.HIGHEST,
                           preferred_element_type=F32)


def _hi_lo(x):
    hi = x.astype(BF16)
    return hi, (x - hi.astype(F32)).astype(BF16)


def _split_lhs(a):
    hi, lo = _hi_lo(a)
    return jnp.concatenate([hi, lo, hi], axis=1)


def _split_rhs(b):
    hi, lo = _hi_lo(b)
    return jnp.concatenate([hi, hi, lo], axis=0)


def _mm_split(a3, b3):
    return lax.dot_general(a3, b3, NN, preferred_element_type=F32)


def _sigmoid(x):
    return jax.nn.sigmoid(x)


def _silu(x):
    return x * _sigmoid(x)


def _softplus(x):
    return jnp.maximum(x, 0.0) + jnp.log1p(jnp.exp(-jnp.abs(x)))


def _rms(x, w):
    return x * lax.rsqrt(jnp.mean(x * x, axis=-1, keepdims=True) + EPS) * w


def _params(*sem):
    return pltpu.CompilerParams(dimension_semantics=sem, vmem_limit_bytes=VMEM_LIMIT)


def _mod_kernel(c_ref, w_ref, b_ref, o_ref):
    o_ref[...] = _mm_hi(_silu(c_ref[...]), w_ref[...]) + b_ref[...]


def _modulation(c, w_mod, b_mod):
    n = c.shape[0]
    tn = 1536
    return pl.pallas_call(
        _mod_kernel,
        out_shape=jax.ShapeDtypeStruct((n, MOD_CHUNKS * D_MODEL), F32),
        grid=(MOD_CHUNKS * D_MODEL // tn,),
        in_specs=[pl.BlockSpec((n, D_MODEL), lambda j: (0, 0)),
                  pl.BlockSpec((D_MODEL, tn), lambda j: (0, j)),
                  pl.BlockSpec((1, tn), lambda j: (0, j))],
        out_specs=pl.BlockSpec((n, tn), lambda j: (0, j)),
        compiler_params=_params("parallel"),
        name="modulation",
    )(c, w_mod, b_mod.reshape(1, -1))


def _mod_spec(per_token, tm, col):
    if per_token:
        return pl.BlockSpec((None, tm, D_MODEL), lambda b, i, *_: (b, i, col))
    return pl.BlockSpec((None, 1, D_MODEL), lambda b, i, *_: (b, 0, col))


def _inproj_kernel(x_ref, sh_ref, sc_ref, nw_ref, w_ref, wba_ref, wbat_ref, out_ref, ba_ref, bat_ref, h_scr):
    @pl.when(pl.program_id(2) == 0)
    def _():
        h = _rms(x_ref[...], nw_ref[...]) * (1.0 + sc_ref[...]) + sh_ref[...]
        hb = h.astype(BF16)
        h_scr[...] = hb
        ba_ref[...] = _mm(hb, wba_ref[...])
        bat_ref[...] = _mm(wbat_ref[...], hb, NT)

    out_ref[...] = _mm(h_scr[...], w_ref[...]).astype(BF16)


def _in_projection(x, mod, per_token, norm_w, w_main, w_ba, w_bat, tm, tn):
    nb, l, _ = x.shape
    return pl.pallas_call(
        _inproj_kernel,
        out_shape=(jax.ShapeDtypeStruct((nb, l, PROJ_W), BF16),
                   jax.ShapeDtypeStruct((nb, l, BA_W), F32),
                   jax.ShapeDtypeStruct((nb, 2 * DN_HEADS, l), F32)),
        grid=(nb, l // tm, PROJ_W // tn),
        in_specs=[pl.BlockSpec((None, tm, D_MODEL), lambda b, i, j: (b, i, 0)),
                  _mod_spec(per_token, tm, 0),
                  _mod_spec(per_token, tm, 1),
                  pl.BlockSpec((1, D_MODEL), lambda b, i, j: (0, 0)),
                  pl.BlockSpec((D_MODEL, tn), lambda b, i, j: (0, j)),
                  pl.BlockSpec((D_MODEL, BA_W), lambda b, i, j: (0, 0)),
                  pl.BlockSpec((2 * DN_HEADS, D_MODEL), lambda b, i, j: (0, 0))],
        out_specs=(pl.BlockSpec((None, tm, tn), lambda b, i, j: (b, i, j)),
                   pl.BlockSpec((None, tm, BA_W), lambda b, i, j: (b, i, 0)),
                   pl.BlockSpec((None, 2 * DN_HEADS, tm), lambda b, i, j: (b, 0, i))),
        scratch_shapes=[pltpu.VMEM((tm, D_MODEL), BF16)],
        compiler_params=_params("parallel", "parallel", "arbitrary"),
        name="in_projection",
    )(x, mod, mod, norm_w, w_main, w_ba, w_bat)


def _delta_prefill_kernel(q_ref, k_ref, v_ref, z_ref, ba_ref, bat_ref, cw_ref, alog_l_ref, dtb_l_ref,
                          alog_c_ref, dtb_c_ref, nw_ref, s0_ref, c0_ref,
                          o_ref, sout_ref, cout_ref, ext_scr, s_scr, qkv_scr, pow_scr, inv_scr, uw_scr):
    step = pl.program_id(1)
    c = CHUNK
    lo = CONV_HALO - (DN_CONV - 1)

    @pl.when(step == 0)
    def _():
        s_scr[...] = s0_ref[...]
        ext_scr[lo:CONV_HALO, :] = c0_ref[...]

    ext_scr[CONV_HALO:, 0:DN_QK] = q_ref[...].astype(F32)
    ext_scr[CONV_HALO:, DN_QK:2 * DN_QK] = k_ref[...].astype(F32)
    ext_scr[CONV_HALO:, 2 * DN_QK:] = v_ref[...].astype(F32)

    row = lax.broadcasted_iota(jnp.int32, (c, c), 0)
    col = lax.broadcasted_iota(jnp.int32, (c, c), 1)
    causal = row >= col
    strict = row > col
    eye = (row == col).astype(F32)
    ltri = causal.astype(F32)
    utri = (row <= col).astype(F32)

    ba = ba_ref[...]
    beta_tok = _sigmoid(ba)
    g_tok = -jnp.exp(alog_l_ref[...]) * _softplus(ba + dtb_l_ref[...])
    gc_tok = _mm_hi(ltri, g_tok)
    bat = bat_ref[...]
    g_t = -jnp.exp(alog_c_ref[...]) * _softplus(bat[DN_HEADS:, :] + dtb_c_ref[...])
    gc_t = _mm_hi(g_t, utri)

    def conv(c0):
        acc = ext_scr[lo:lo + c, c0:c0 + DN_DK] * cw_ref[0:1, c0:c0 + DN_DK]
        for i in range(1, DN_CONV):
            acc = acc + ext_scr[lo + i:lo + i + c, c0:c0 + DN_DK] * cw_ref[i:i + 1, c0:c0 + DN_DK]
        return _silu(acc)

    def decay_of(h):
        gc = gc_tok[:, DN_HEADS + h:DN_HEADS + h + 1]
        return gc, jnp.where(causal, jnp.exp(jnp.where(causal, gc - gc_t[h:h + 1, :], 0.0)), 0.0)

    for h in range(DN_HEADS):
        q = conv(h * DN_DK)
        k = conv(DN_QK + h * DN_DK)
        q = q * lax.rsqrt(jnp.sum(q * q, axis=-1, keepdims=True) + EPS) * (DN_DK ** -0.5)
        k = k * lax.rsqrt(jnp.sum(k * k, axis=-1, keepdims=True) + EPS)
        qkv_scr[:, h * DN_DK:(h + 1) * DN_DK] = q
        qkv_scr[:, DN_QK + h * DN_DK:DN_QK + (h + 1) * DN_DK] = k
        qkv_scr[:, 2 * DN_QK + h * DN_DV:2 * DN_QK + (h + 1) * DN_DV] = conv(2 * DN_QK + h * DN_DV)
        _, decay = decay_of(h)
        a_mat = jnp.where(strict, _mm(k * beta_tok[:, h:h + 1], k, NT) * decay, 0.0)
        pow_scr[h] = a_mat
        inv_scr[h] = eye - a_mat

    for _ in range(int(math.log2(c)) - 1):
        for h in range(DN_HEADS):
            a_pow = pow_scr[h]
            pow_scr[h] = _mm_split(_split_lhs(a_pow), _split_rhs(a_pow))
        for h in range(DN_HEADS):
            inv = inv_scr[h]
            inv_scr[h] = inv + _mm_split(_split_lhs(inv), _split_rhs(pow_scr[h]))

    def qkv_of(h):
        return (qkv_scr[:, h * DN_DK:(h + 1) * DN_DK], qkv_scr[:, DN_QK + h * DN_DK:DN_QK + (h + 1) * DN_DK],
                qkv_scr[:, 2 * DN_QK + h * DN_DV:2 * DN_QK + (h + 1) * DN_DV])

    for h in range(DN_HEADS):
        q, k, v = qkv_of(h)
        beta = beta_tok[:, h:h + 1]
        gc, decay = decay_of(h)
        rhs = jnp.concatenate([v * beta, k * beta * jnp.exp(gc)], axis=-1)
        uw_scr[h] = _mm_split(_split_lhs(inv_scr[h]), _split_rhs(rhs))
        pow_scr[h] = jnp.where(causal, _mm(q, k, NT) * decay, 0.0)
    for h in range(DN_HEADS):
        inv_scr[h] = uw_scr[h, :, 0:DN_DV] - _mm(uw_scr[h, :, DN_DV:], s_scr[h])
    for h in range(DN_HEADS):
        q, k, _ = qkv_of(h)
        gc = gc_tok[:, DN_HEADS + h:DN_HEADS + h + 1]
        g_last = gc[c - 1:c, :]
        s = s_scr[h]
        v_new = inv_scr[h]
        o = _mm(q * jnp.exp(gc), s) + _mm(pow_scr[h], v_new)
        s_scr[h] = s * jnp.exp(g_last) + _mm(k * jnp.exp(g_last - gc), v_new, TN)
        z = z_ref[:, h * DN_DV:(h + 1) * DN_DV].astype(F32)
        o_ref[:, h * DN_DV:(h + 1) * DN_DV] = (_rms(o, nw_ref[...]) * _silu(z)).astype(BF16)

    ext_scr[0:CONV_HALO, :] = ext_scr[c:c + CONV_HALO, :]

    @pl.when(step == pl.num_programs(1) - 1)
    def _():
        sout_ref[...] = s_scr[...]
        cout_ref[...] = ext_scr[lo:CONV_HALO, :]


def _lane_pad(v, offset):
    return jnp.zeros((1, BA_W), F32).at[0, offset:offset + v.shape[0]].set(v.astype(F32))


def _delta_prefill(proj, ba, bat, conv_w, a_log, dt_bias, dn_norm_w, s0, c0):
    nb, l, _ = proj.shape
    c = CHUNK
    col_spec = lambda j: pl.BlockSpec((None, c, D_MODEL), lambda b, i: (b, i, j))
    full = lambda shape: pl.BlockSpec(shape, lambda b, i: (0,) * len(shape))
    return pl.pallas_call(
        _delta_prefill_kernel,
        out_shape=(jax.ShapeDtypeStruct((nb, l, DN_VW), BF16),
                   jax.ShapeDtypeStruct((nb, DN_HEADS, DN_DK, DN_DV), F32),
                   jax.ShapeDtypeStruct((nb, DN_CONV - 1, DN_CONV_W), F32)),
        grid=(nb, l // c),
        in_specs=[col_spec(COL_Q_A), col_spec(COL_K_A), col_spec(COL_V_A), col_spec(COL_Z_A),
                  pl.BlockSpec((None, c, BA_W), lambda b, i: (b, i, 0)),
                  pl.BlockSpec((None, 2 * DN_HEADS, c), lambda b, i: (b, 0, i)),
                  full((DN_CONV, DN_CONV_W)), full((1, BA_W)), full((1, BA_W)),
                  full((DN_HEADS, 1)), full((DN_HEADS, 1)), full((1, DN_DV)),
                  pl.BlockSpec((None, DN_HEADS, DN_DK, DN_DV), lambda b, i: (b, 0, 0, 0)),
                  pl.BlockSpec((None, DN_CONV - 1, DN_CONV_W), lambda b, i: (b, 0, 0))],
        out_specs=(pl.BlockSpec((None, c, DN_VW), lambda b, i: (b, i, 0)),
                   pl.BlockSpec((None, DN_HEADS, DN_DK, DN_DV), lambda b, i: (b, 0, 0, 0)),
                   pl.BlockSpec((None, DN_CONV - 1, DN_CONV_W), lambda b, i: (b, 0, 0))),
        scratch_shapes=[pltpu.VMEM((c + CONV_HALO, DN_CONV_W), F32),
                        pltpu.VMEM((DN_HEADS, DN_DK, DN_DV), F32),
                        pltpu.VMEM((c, DN_CONV_W), F32),
                        pltpu.VMEM((DN_HEADS, c, c), F32),
                        pltpu.VMEM((DN_HEADS, c, c), F32),
                        pltpu.VMEM((DN_HEADS, c, DN_DV + DN_DK), F32)],
        compiler_params=_params("parallel", "arbitrary"),
        name="delta_prefill",
    )(proj, proj, proj, proj, ba, bat, conv_w, _lane_pad(a_log, DN_HEADS), _lane_pad(dt_bias, DN_HEADS),
      a_log.reshape(DN_HEADS, 1), dt_bias.reshape(DN_HEADS, 1), dn_norm_w.reshape(1, DN_DV), s0, c0)


def _log_gamma(h):
    return math.log(1.0 - 2.0 ** (-5.0 - h))


def _rotate(x, cos, sin):
    half = x.shape[-1] // 2
    x1, x2 = x[:, :half], x[:, half:]
    return jnp.concatenate([x1 * cos - x2 * sin, x1 * sin + x2 * cos], axis=-1)


def _group_norm_gate(o, gw, gb, gate):
    mu = jnp.mean(o, axis=-1, keepdims=True)
    var = jnp.mean(jnp.square(o - mu), axis=-1, keepdims=True)
    return ((o - mu) * lax.rsqrt(var + EPS) * gw + gb) * _silu(gate)


def _ret_prefill_kernel(q_ref, k_ref, v_ref, g_ref, inv_ref, gw_ref, gb_ref, s0_ref, o_ref, sout_ref, s_scr):
    step = pl.program_id(1)
    c = CHUNK

    @pl.when(step == 0)
    def _():
        s_scr[...] = s0_ref[...]

    idx_c = lax.broadcasted_iota(jnp.int32, (c, 1), 0)
    pos = (step * c + idx_c).astype(F32)
    ang = pos * inv_ref[...]
    cos, sin = jnp.cos(ang), jnp.sin(ang)
    rel = (lax.broadcasted_iota(jnp.int32, (c, c), 0) - lax.broadcasted_iota(jnp.int32, (c, c), 1)).astype(F32)
    idx = idx_c.astype(F32)

    for h in range(RET_HEADS):
        lg = _log_gamma(h)
        q = _rotate(q_ref[:, h * RET_DK:(h + 1) * RET_DK].astype(F32), cos, sin)
        k = _rotate(k_ref[:, h * RET_DK:(h + 1) * RET_DK].astype(F32), cos, sin) * (RET_DK ** -0.5)
        v = v_ref[:, h * RET_DV:(h + 1) * RET_DV]
        d_mat = jnp.where(rel >= 0, jnp.exp(jnp.maximum(rel, 0.0) * lg), 0.0)
        intra = _mm(_mm(q, k, NT) * d_mat, v)
        s = s_scr[h]
        cross = _mm(q * jnp.exp((idx + 1.0) * lg), s)
        s_scr[h] = s * math.exp(c * lg) + _mm(k * jnp.exp((c - 1.0 - idx) * lg), v, TN)
        sl = slice(h * RET_DV, (h + 1) * RET_DV)
        o_ref[:, sl] = _group_norm_gate(intra + cross, gw_ref[:, sl], gb_ref[:, sl],
                                        g_ref[:, sl].astype(F32)).astype(BF16)

    @pl.when(step == pl.num_programs(1) - 1)
    def _():
        sout_ref[...] = s_scr[...]


def _ret_prefill(proj, inv_freq, gn_w, gn_b, s0):
    nb, l, _ = proj.shape
    c = CHUNK
    full = lambda shape: pl.BlockSpec(shape, lambda b, i: (0,) * len(shape))
    state_spec = pl.BlockSpec((None, RET_HEADS, RET_DK, RET_DV), lambda b, i: (b, 0, 0, 0))
    return pl.pallas_call(
        _ret_prefill_kernel,
        out_shape=(jax.ShapeDtypeStruct((nb, l, RET_VW), BF16),
                   jax.ShapeDtypeStruct((nb, RET_HEADS, RET_DK, RET_DV), F32)),
        grid=(nb, l // c),
        in_specs=[pl.BlockSpec((None, c, RET_QK), lambda b, i: (b, i, COL_Q_B)),
                  pl.BlockSpec((None, c, RET_QK), lambda b, i: (b, i, COL_K_B)),
                  pl.BlockSpec((None, c, RET_VW), lambda b, i: (b, i, COL_V_B)),
                  pl.BlockSpec((None, c, RET_VW), lambda b, i: (b, i, COL_G_B)),
                  full((1, RET_DK // 2)), full((1, RET_VW)), full((1, RET_VW)), state_spec],
        out_specs=(pl.BlockSpec((None, c, RET_VW), lambda b, i: (b, i, 0)), state_spec),
        scratch_shapes=[pltpu.VMEM((RET_HEADS, RET_DK, RET_DV), F32)],
        compiler_params=_params("parallel", "arbitrary"),
        name="retention_prefill",
    )(proj, proj, proj, proj, inv_freq, gn_w.reshape(1, RET_VW), gn_b.reshape(1, RET_VW), s0)


ROWS = 8


def _row_mm(a, s):
    return _mm(jnp.broadcast_to(a, (ROWS, a.shape[-1])), s)[0:1, :]


def _outer(a, b):
    first = lax.broadcasted_iota(jnp.int32, (ROWS, a.shape[-1]), 0) == 0
    a8 = jnp.where(first, jnp.broadcast_to(a, (ROWS, a.shape[-1])), 0.0)
    return _mm(a8, jnp.broadcast_to(b, (ROWS, b.shape[-1])), TN)


def _delta_step_kernel(qkv_ref, z_ref, ba_ref, cw_ref, alog_ref, dtb_ref, nw_ref, s_ref, c_ref,
                       o_ref, sout_ref, cout_ref):
    u_new = qkv_ref[...].astype(F32)
    buf = c_ref[...]
    acc = u_new * cw_ref[DN_CONV - 1:DN_CONV, :]
    for i in range(DN_CONV - 1):
        acc = acc + buf[i:i + 1, :] * cw_ref[i:i + 1, :]
    qkv = _silu(acc)
    cout_ref[0:DN_CONV - 2, :] = buf[1:, :]
    cout_ref[DN_CONV - 2:, :] = u_new
    ba = ba_ref[...]
    beta_all = _sigmoid(ba)
    g_all = -jnp.exp(alog_ref[...]) * _softplus(ba + dtb_ref[...])
    for h in range(DN_HEADS):
        q = qkv[:, h * DN_DK:(h + 1) * DN_DK]
        k = qkv[:, DN_QK + h * DN_DK:DN_QK + (h + 1) * DN_DK]
        v = qkv[:, 2 * DN_QK + h * DN_DV:2 * DN_QK + (h + 1) * DN_DV]
        q = q * lax.rsqrt(jnp.sum(q * q, axis=-1, keepdims=True) + EPS) * (DN_DK ** -0.5)
        k = k * lax.rsqrt(jnp.sum(k * k, axis=-1, keepdims=True) + EPS)
        beta = beta_all[:, h:h + 1]
        eg = jnp.exp(g_all[:, DN_HEADS + h:DN_HEADS + h + 1])
        s = s_ref[h]
        kb = k * beta
        v_new = v * beta - _row_mm(kb * eg, s)
        o = _row_mm(q * eg, s) + jnp.sum(q * k, axis=-1, keepdims=True) * v_new
        sout_ref[h] = s * eg + _outer(k, v_new)
        z = z_ref[:, h * DN_DV:(h + 1) * DN_DV].astype(F32)
        o_ref[:, h * DN_DV:(h + 1) * DN_DV] = (_rms(o, nw_ref[...]) * _silu(z)).astype(BF16)


def _delta_step(proj, ba, conv_w, a_log, dt_bias, dn_norm_w, s0, c0):
    n = proj.shape[0]
    full = lambda shape: pl.BlockSpec(shape, lambda b: (0,) * len(shape))
    state_spec = pl.BlockSpec((None, DN_HEADS, DN_DK, DN_DV), lambda b: (b, 0, 0, 0))
    conv_spec = pl.BlockSpec((None, DN_CONV - 1, DN_CONV_W), lambda b: (b, 0, 0))
    return pl.pallas_call(
        _delta_step_kernel,
        out_shape=(jax.ShapeDtypeStruct((n, 1, DN_VW), BF16),
                   jax.ShapeDtypeStruct((n, DN_HEADS, DN_DK, DN_DV), F32),
                   jax.ShapeDtypeStruct((n, DN_CONV - 1, DN_CONV_W), F32)),
        grid=(n,),
        in_specs=[pl.BlockSpec((None, 1, DN_CONV_W), lambda b: (b, 0, 0)),
                  pl.BlockSpec((None, 1, DN_VW), lambda b: (b, 0, COL_Z_A)),
                  pl.BlockSpec((None, 1, BA_W), lambda b: (b, 0, 0)),
                  full((DN_CONV, DN_CONV_W)), full((1, BA_W)), full((1, BA_W)), full((1, DN_DV)),
                  state_spec, conv_spec],
        out_specs=(pl.BlockSpec((None, 1, DN_VW), lambda b: (b, 0, 0)), state_spec, conv_spec),
        compiler_params=_params("parallel"),
        name="delta_step",
    )(proj, proj, ba, conv_w, _lane_pad(a_log, DN_HEADS), _lane_pad(dt_bias, DN_HEADS),
      dn_norm_w.reshape(1, DN_DV), s0, c0)


def _ret_step_kernel(q_ref, k_ref, v_ref, g_ref, inv_ref, gw_ref, gb_ref, s_ref, o_ref, sout_ref):
    ang = float(PAST_LEN) * inv_ref[...]
    cos, sin = jnp.cos(ang), jnp.sin(ang)
    for h in range(RET_HEADS):
        gamma = math.exp(_log_gamma(h))
        q = _rotate(q_ref[:, h * RET_DK:(h + 1) * RET_DK].astype(F32), cos, sin)
        k = _rotate(k_ref[:, h * RET_DK:(h + 1) * RET_DK].astype(F32), cos, sin) * (RET_DK ** -0.5)
        v = v_ref[:, h * RET_DV:(h + 1) * RET_DV].astype(F32)
        s = s_ref[h]
        o = jnp.sum(q * k, axis=-1, keepdims=True) * v + _row_mm(q * gamma, s)
        sout_ref[h] = s * gamma + _outer(k, v)
        sl = slice(h * RET_DV, (h + 1) * RET_DV)
        o_ref[:, sl] = _group_norm_gate(o, gw_ref[:, sl], gb_ref[:, sl], g_ref[:, sl].astype(F32)).astype(BF16)


def _ret_step(proj, inv_freq, gn_w, gn_b, s0):
    n = proj.shape[0]
    full = lambda shape: pl.BlockSpec(shape, lambda b: (0,) * len(shape))
    state_spec = pl.BlockSpec((None, RET_HEADS, RET_DK, RET_DV), lambda b: (b, 0, 0, 0))
    return pl.pallas_call(
        _ret_step_kernel,
        out_shape=(jax.ShapeDtypeStruct((n, 1, RET_VW), BF16),
                   jax.ShapeDtypeStruct((n, RET_HEADS, RET_DK, RET_DV), F32)),
        grid=(n,),
        in_specs=[pl.BlockSpec((None, 1, RET_QK), lambda b: (b, 0, COL_Q_B)),
                  pl.BlockSpec((None, 1, RET_QK), lambda b: (b, 0, COL_K_B)),
                  pl.BlockSpec((None, 1, RET_VW), lambda b: (b, 0, COL_V_B)),
                  pl.BlockSpec((None, 1, RET_VW), lambda b: (b, 0, COL_G_B)),
                  full((1, RET_DK // 2)), full((1, RET_VW)), full((1, RET_VW)), state_spec],
        out_specs=(pl.BlockSpec((None, 1, RET_VW), lambda b: (b, 0, 0)), state_spec),
        compiler_params=_params("parallel"),
        name="retention_step",
    )(proj, proj, proj, proj, inv_freq, gn_w.reshape(1, RET_VW), gn_b.reshape(1, RET_VW), s0)


def _post_mixer_kernel(oa_ref, ob_ref, ga_ref, gb_ref, x_ref, g1_ref, sh2_ref, sc2_ref, wda_ref, wdb_ref,
                       wout_ref, nw_ref, rw_ref, x1_ref, h_ref, hf_ref, logit_ref):
    y_a = _mm(oa_ref[...], wda_ref[...])
    y_b = _mm(ob_ref[...], wdb_ref[...])
    merged = _sigmoid(ga_ref[...].astype(F32)) * y_a + _sigmoid(gb_ref[...].astype(F32)) * y_b
    x1 = x_ref[...] + g1_ref[...] * _mm(merged, wout_ref[...])
    x1_ref[...] = x1
    hf = _rms(x1, nw_ref[...]) * (1.0 + sc2_ref[...]) + sh2_ref[...]
    h_ref[...] = hf.astype(BF16)
    hf_ref[...] = hf.reshape(hf_ref.shape)
    logit_ref[...] = _mm_hi(rw_ref[...], hf, NT)


def _post_mixer(o_a, o_b, proj, x, mod, per_token, w_down_a, w_down_b, w_out, norm_w, router_w, tm):
    nb, l, _ = x.shape
    tok = lambda w, j=0: pl.BlockSpec((None, tm, w), lambda b, i: (b, i, j))
    full = lambda shape: pl.BlockSpec(shape, lambda b, i: (0,) * len(shape))
    return pl.pallas_call(
        _post_mixer_kernel,
        out_shape=(jax.ShapeDtypeStruct((nb, l, D_MODEL), F32),
                   jax.ShapeDtypeStruct((nb, l, D_MODEL), BF16),
                   jax.ShapeDtypeStruct((nb, l) + TOKEN_TILE, F32),
                   jax.ShapeDtypeStruct((nb, N_EXPERTS, l), F32)),
        grid=(nb, l // tm),
        in_specs=[tok(DN_VW), tok(RET_VW), tok(D_MODEL, COL_GATE_A), tok(D_MODEL, COL_GATE_B), tok(D_MODEL),
                  _mod_spec(per_token, tm, 2), _mod_spec(per_token, tm, 3), _mod_spec(per_token, tm, 4),
                  full((DN_VW, D_MODEL)), full((RET_VW, D_MODEL)), full((D_MODEL, D_MODEL)),
                  full((1, D_MODEL)), full((N_EXPERTS, D_MODEL))],
        out_specs=(tok(D_MODEL), tok(D_MODEL),
                   pl.BlockSpec((None, tm) + TOKEN_TILE, lambda b, i: (b, i, 0, 0)),
                   pl.BlockSpec((None, N_EXPERTS, tm), lambda b, i: (b, 0, i))),
        compiler_params=_params("parallel", "parallel"),
        name="post_mixer",
    )(o_a, o_b, proj, proj, x, mod, mod, mod, w_down_a, w_down_b, w_out, norm_w, router_w)


def _first_max(x, axis, n):
    m = jnp.max(x, axis=axis, keepdims=True)
    ids = lax.broadcasted_iota(jnp.int32, x.shape, axis)
    first = jnp.min(jnp.where(x == m, ids, n), axis=axis, keepdims=True)
    return m, ids == first


def _eye(n):
    return (lax.broadcasted_iota(jnp.int32, (n, n), 0) == lax.broadcasted_iota(jnp.int32, (n, n), 1)).astype(F32)


def _route_kernel(logit_ref, bias_ref, ek_ref, rk_ref, wkt_ref, cnt_ref, run_scr):
    @pl.when(jnp.logical_and(pl.program_id(0) == 0, pl.program_id(1) == 0))
    def _():
        run_scr[...] = jnp.zeros_like(run_scr)

    t = logit_ref.shape[-1]
    scores = _sigmoid(logit_ref[...])
    biased = scores + bias_ref[...]
    neg = jnp.float32(-jnp.inf)
    b3 = biased.reshape(N_GROUPS, GROUP_SIZE, t)
    m1, hit = _first_max(b3, 1, GROUP_SIZE)
    m2 = jnp.max(jnp.where(hit, neg, b3), axis=1, keepdims=True)
    gscore = (m1 + m2).reshape(N_GROUPS, t)
    gmask = jnp.zeros((N_GROUPS, t), jnp.bool_)
    for _ in range(TOPK_GROUPS):
        _, hit = _first_max(gscore, 0, N_GROUPS)
        gmask = jnp.logical_or(gmask, hit)
        gscore = jnp.where(hit, neg, gscore)
    emask = jnp.broadcast_to(gmask.reshape(N_GROUPS, 1, t), (N_GROUPS, GROUP_SIZE, t)).reshape(N_EXPERTS, t)
    masked = jnp.where(emask, biased, neg)
    eid = lax.broadcasted_iota(jnp.int32, (N_EXPERTS, t), 0).astype(F32)
    hits, e_rows, w_rows = [], [], []
    for _ in range(TOP_K):
        _, hit = _first_max(masked, 0, N_EXPERTS)
        hits.append(hit)
        e_rows.append(jnp.sum(jnp.where(hit, eid, 0.0), axis=0, keepdims=True))
        w_rows.append(jnp.sum(jnp.where(hit, scores, 0.0), axis=0, keepdims=True))
        masked = jnp.where(hit, neg, masked)
    sel = sum(hit.astype(F32) for hit in hits)
    before = lax.broadcasted_iota(jnp.int32, (t, t), 0) < lax.broadcasted_iota(jnp.int32, (t, t), 1)
    rank = run_scr[:, 0:1] + _mm(sel, before.astype(F32))
    rk_rows = [jnp.sum(jnp.where(hit, rank, 0.0), axis=0, keepdims=True) for hit in hits]
    run_scr[...] = run_scr[...] + jnp.sum(sel, axis=1, keepdims=True)
    wk = jnp.concatenate(w_rows, axis=0)
    wk = wk / jnp.sum(wk, axis=0, keepdims=True) * ROUTED_SCALE
    ek_ref[...] = jnp.concatenate(e_rows, axis=0).astype(jnp.int32)
    rk_ref[...] = jnp.concatenate(rk_rows, axis=0).astype(jnp.int32)
    wkt_ref[...] = _mm_hi(wk, _eye(TOP_K), TN)
    cnt_ref[...] = run_scr[...]


def _route(logits_t, router_bias, tr):
    nb, _, l = logits_t.shape
    pick = pl.BlockSpec((None, TOP_K, tr), lambda b, i: (b, 0, i))
    return pl.pallas_call(
        _route_kernel,
        out_shape=(jax.ShapeDtypeStruct((nb, TOP_K, l), jnp.int32),
                   jax.ShapeDtypeStruct((nb, TOP_K, l), jnp.int32),
                   jax.ShapeDtypeStruct((nb, l, TOP_K), F32),
                   jax.ShapeDtypeStruct((N_EXPERTS, LANES), F32)),
        grid=(nb, l // tr),
        in_specs=[pl.BlockSpec((None, N_EXPERTS, tr), lambda b, i: (b, 0, i)),
                  pl.BlockSpec((N_EXPERTS, 1), lambda b, i: (0, 0))],
        out_specs=(pick, pick, pl.BlockSpec((None, tr, TOP_K), lambda b, i: (b, i, 0)),
                   pl.BlockSpec((N_EXPERTS, LANES), lambda b, i: (0, 0))),
        scratch_shapes=[pltpu.VMEM((N_EXPERTS, LANES), F32)],
        compiler_params=_params("arbitrary", "arbitrary"),
        name="route",
    )(logits_t, router_bias.reshape(N_EXPERTS, 1))


def _plan_kernel(cnt_ref, ek_ref, rk_ref, pos_ref, be_ref, pe_ref, nu_ref, start_scr):
    @pl.when(jnp.logical_and(pl.program_id(0) == 0, pl.program_id(1) == 0))
    def _():
        padded = jnp.ceil(cnt_ref[...] * (1.0 / MOE_BLOCK)) * MOE_BLOCK
        r = lax.broadcasted_iota(jnp.int32, (N_EXPERTS, N_EXPERTS), 0)
        c = lax.broadcasted_iota(jnp.int32, (N_EXPERTS, N_EXPERTS), 1)
        pad_end = _mm_hi((r >= c).astype(F32), padded)
        start_scr[...] = pad_end - padded
        pe_ref[...] = pad_end.astype(jnp.int32)
        nu_ref[...] = (pad_end[N_EXPERTS - 1:, :] * (1.0 / MOE_BLOCK)).astype(jnp.int32)
        nbp = be_ref.shape[-1]
        first_row = lax.broadcasted_iota(jnp.int32, (1, nbp), 1).astype(F32) * MOE_BLOCK
        owner = jnp.sum((pad_end[:, 0:1] <= first_row).astype(F32), axis=0, keepdims=True)
        be_ref[...] = jnp.minimum(owner, N_EXPERTS - 1.0).astype(jnp.int32)

    t = ek_ref.shape[-1]
    eid = lax.broadcasted_iota(jnp.int32, (N_EXPERTS, t), 0)
    start = start_scr[:, 0:1]
    rows = [jnp.sum(jnp.where(eid == ek_ref[k:k + 1, :], start, 0.0), axis=0, keepdims=True)
            for k in range(TOP_K)]
    pos_ref[...] = jnp.concatenate(rows, axis=0).astype(jnp.int32) + rk_ref[...]


def _plan(counts, ek, rk, tr, n_blocks):
    nb, _, l = ek.shape
    nbp = -(-n_blocks // LANES) * LANES
    pick = pl.BlockSpec((None, TOP_K, tr), lambda b, i: (b, 0, i))
    const = lambda shape: pl.BlockSpec(shape, lambda b, i: (0, 0))
    return pl.pallas_call(
        _plan_kernel,
        out_shape=(jax.ShapeDtypeStruct((nb, TOP_K, l), jnp.int32),
                   jax.ShapeDtypeStruct((1, nbp), jnp.int32),
                   jax.ShapeDtypeStruct((N_EXPERTS, LANES), jnp.int32),
                   jax.ShapeDtypeStruct((1, LANES), jnp.int32)),
        grid=(nb, l // tr),
        in_specs=[const((N_EXPERTS, LANES)), pick, pick],
        out_specs=(pick, const((1, nbp)), const((N_EXPERTS, LANES)), const((1, LANES))),
        scratch_shapes=[pltpu.VMEM((N_EXPERTS, LANES), F32)],
        compiler_params=_params("arbitrary", "arbitrary"),
        name="plan",
    )(counts, ek, rk)


def _dispatch_kernel(pe_ref, pos_ref, x_ref, xs_hbm, zero_scr, sem):
    b, i = pl.program_id(0), pl.program_id(1)
    tm = pos_ref.shape[-1]

    def row(k, t):
        return pltpu.make_async_copy(x_ref.at[t], xs_hbm.at[pos_ref[k, t]], sem)

    @pl.when(jnp.logical_and(b == 0, i == 0))
    def _():
        zero_scr[...] = jnp.zeros_like(zero_scr)

        def tail_copy(e):
            return pltpu.make_async_copy(zero_scr, xs_hbm.at[pl.ds(pe_ref[e] - MOE_BLOCK, MOE_BLOCK)], sem)

        def nonempty(e):
            return pe_ref[e] > jnp.where(e > 0, pe_ref[jnp.maximum(e - 1, 0)], 0)

        @pl.loop(0, N_EXPERTS)
        def _(e):
            @pl.when(nonempty(e))
            def _():
                tail_copy(e).start()

        @pl.loop(0, N_EXPERTS)
        def _(e):
            @pl.when(nonempty(e))
            def _():
                tail_copy(e).wait()

    @pl.loop(0, tm)
    def _(t):
        for k in range(TOP_K):
            row(k, t).start(priority=k % 2)

    @pl.loop(0, tm)
    def _(t):
        for k in range(TOP_K):
            row(k, t).wait()


def _dispatch(x, pos, pad_end, n_rows, tm):
    nb, l = x.shape[:2]
    return pl.pallas_call(
        _dispatch_kernel,
        out_shape=jax.ShapeDtypeStruct((n_rows,) + TOKEN_TILE, F32),
        grid_spec=pltpu.PrefetchScalarGridSpec(
            num_scalar_prefetch=1,
            grid=(nb, l // tm),
            in_specs=[pl.BlockSpec((None, TOP_K, tm), lambda b, i, pe: (b, 0, i), memory_space=pltpu.SMEM),
                      pl.BlockSpec((tm,) + TOKEN_TILE, lambda b, i, pe: (b * (l // tm) + i, 0, 0))],
            out_specs=pl.BlockSpec(memory_space=pl.ANY),
            scratch_shapes=[pltpu.VMEM((MOE_BLOCK,) + TOKEN_TILE, F32), pltpu.SemaphoreType.DMA(())]),
        compiler_params=_params("arbitrary", "arbitrary"),
        name="dispatch",
    )(pad_end, pos, x.reshape((nb * l,) + TOKEN_TILE))


def _experts_kernel(be_ref, nu_ref, x_ref, wg_ref, wu_ref, wd_ref, y_ref):
    @pl.when(pl.program_id(0) < nu_ref[0])
    def _():
        xb = x_ref[...].reshape(MOE_BLOCK, D_MODEL).astype(BF16)
        y = _mm(_silu(_mm(xb, wg_ref[...])) * _mm(xb, wu_ref[...]), wd_ref[...])
        y_ref[...] = y.reshape(y_ref.shape)


def _experts(xs, block_expert, n_used, w_gate, w_up, w_down):
    n_rows, d = xs.shape[0], D_MODEL
    used = lambda i, nu: jnp.minimum(i, nu[0] - 1)
    rows = pl.BlockSpec((MOE_BLOCK,) + TOKEN_TILE, lambda i, be, nu: (used(i, nu), 0, 0))
    return pl.pallas_call(
        _experts_kernel,
        out_shape=jax.ShapeDtypeStruct((n_rows,) + TOKEN_TILE, F32),
        grid_spec=pltpu.PrefetchScalarGridSpec(
            num_scalar_prefetch=2,
            grid=(n_rows // MOE_BLOCK,),
            in_specs=[rows,
                      pl.BlockSpec((None, d, D_EXPERT), lambda i, be, nu: (be[used(i, nu)], 0, 0)),
                      pl.BlockSpec((None, d, D_EXPERT), lambda i, be, nu: (be[used(i, nu)], 0, 0)),
                      pl.BlockSpec((None, D_EXPERT, d), lambda i, be, nu: (be[used(i, nu)], 0, 0))],
            out_specs=rows),
        compiler_params=_params("arbitrary"),
        name="experts",
    )(block_expert, n_used, xs, w_gate, w_up, w_down)


def _combine_kernel(pos_ref, pos_next_ref, wkt_ref, ys_hbm, h_ref, x1_ref, g2_ref, wsg_ref, wsu_ref, wsd_ref,
                    fw_ref, y_ref, rows_scr, sem):
    s = pl.program_id(0)
    tm = pos_ref.shape[-1]
    slot = lax.rem(s, 2)

    def row(p_ref, sl, k, t):
        return pltpu.make_async_copy(ys_hbm.at[p_ref[k, t]], rows_scr.at[sl, k, t], sem.at[sl])

    def fetch(p_ref, sl):
        @pl.loop(0, tm)
        def _(t):
            for k in range(TOP_K):
                row(p_ref, sl, k, t).start(priority=k % 2)

    @pl.when(s == 0)
    def _():
        fetch(pos_ref, 0)

    @pl.when(s + 1 < pl.num_programs(0))
    def _():
        fetch(pos_next_ref, 1 - slot)

    hb = h_ref[...]
    acc = _mm(_silu(_mm(hb, wsg_ref[...])) * _mm(hb, wsu_ref[...]), wsd_ref[...])

    @pl.loop(0, tm)
    def _(t):
        for k in range(TOP_K):
            row(pos_ref, slot, k, t).wait()

    wkt = wkt_ref[...]
    for k in range(TOP_K):
        acc = acc + rows_scr[slot, k].reshape(tm, D_MODEL) * wkt[:, k:k + 1]
    y_ref[...] = _rms(x1_ref[...] + g2_ref[...] * acc, fw_ref[...])


def _combine(ys, pos, wkt, hffn, x1, mod, per_token, ws_gate, ws_up, ws_down, final_w, tm):
    nb, l, d = x1.shape
    tps = l // tm
    n_tiles = nb * tps
    tok = lambda w: pl.BlockSpec((None, tm, w), lambda s: (s // tps, s % tps, 0))
    full = lambda shape: pl.BlockSpec(shape, lambda s: (0,) * len(shape))
    pick = lambda step: pl.BlockSpec((None, TOP_K, tm), lambda s: (step(s) // tps, 0, step(s) % tps),
                                     memory_space=pltpu.SMEM)
    if per_token:
        g2_spec = pl.BlockSpec((None, tm, d), lambda s: (s // tps, s % tps, 5))
    else:
        g2_spec = pl.BlockSpec((None, 1, d), lambda s: (s // tps, 0, 5))
    return pl.pallas_call(
        _combine_kernel,
        out_shape=jax.ShapeDtypeStruct((nb, l, d), F32),
        grid=(n_tiles,),
        in_specs=[pick(lambda s: s), pick(lambda s: jnp.minimum(s + 1, n_tiles - 1)),
                  tok(TOP_K), pl.BlockSpec(memory_space=pl.ANY), tok(d), tok(d), g2_spec,
                  full((d, D_SHARED)), full((d, D_SHARED)), full((D_SHARED, d)), full((1, d))],
        out_specs=tok(d),
        scratch_shapes=[pltpu.VMEM((2, TOP_K, tm) + TOKEN_TILE, F32), pltpu.SemaphoreType.DMA((2,))],
        compiler_params=_params("arbitrary"),
        name="combine",
    )(pos, pos, wkt, ys, hffn, x1, mod, ws_gate, ws_up, ws_down, final_w)


def _moe(hffn, hf32, logits_t, x1, mod, per_token, w, tr, tm_dispatch, tm_combine):
    nb, l, _ = x1.shape
    n_blocks = nb * l * TOP_K // MOE_BLOCK + N_EXPERTS
    ek, rk, wkt, counts = _route(logits_t, w["router_bias"], tr)
    pos, block_expert, pad_end, n_used = _plan(counts, ek, rk, tr, n_blocks)
    xs = _dispatch(hf32, pos, pad_end[:, 0], n_blocks * MOE_BLOCK, tm_dispatch)
    ys = _experts(xs, block_expert[0, :n_blocks], n_used[0, :1], w["w_gate"], w["w_up"], w["w_down"])
    return _combine(ys, pos, wkt, hffn, x1, mod, per_token, w["ws_gate"], w["ws_up"], w["ws_down"],
                    w["final_norm"], tm_combine)


def _trunk(x, mod, per_token, states, w):
    nb, l, _ = x.shape
    tm_proj, tn_proj, tm_post, tr, tm_moe = min(l, 1024), 1024, min(l, 512), min(l, 512), min(l, 1024)
    proj, ba, bat = _in_projection(x, mod, per_token, w["norm_mix"], w["w_main"], w["w_ba"], w["w_bat"],
                                   tm_proj, tn_proj)
    if states is None:
        c0 = jnp.zeros((nb, DN_CONV - 1, DN_CONV_W), F32)
        sd0 = jnp.zeros((nb, DN_HEADS, DN_DK, DN_DV), F32)
        sr0 = jnp.zeros((nb, RET_HEADS, RET_DK, RET_DV), F32)
        o_a, sd, cb = _delta_prefill(proj, ba, bat, w["conv_w"], w["a_log"], w["dt_bias"], w["dn_norm"], sd0, c0)
        o_b, sr = _ret_prefill(proj, w["inv_freq"], w["gn_w"], w["gn_b"], sr0)
    else:
        c0, sd0, sr0 = states
        n = nb * l
        o_a, sd, cb = _delta_step(proj.reshape(n, 1, PROJ_W), ba.reshape(n, 1, BA_W), w["conv_w"], w["a_log"],
                                  w["dt_bias"], w["dn_norm"], sd0, c0)
        o_b, sr = _ret_step(proj.reshape(n, 1, PROJ_W), w["inv_freq"], w["gn_w"], w["gn_b"], sr0)
        o_a = o_a.reshape(nb, l, DN_VW)
        o_b = o_b.reshape(nb, l, RET_VW)
    x1, hffn, hf32, logits_t = _post_mixer(o_a, o_b, proj, x, mod, per_token, w["w_down_a"], w["w_down_b"],
                                           w["w_out"], w["norm_ffn"], w["router_w"], tm_post)
    y = _moe(hffn, hf32, logits_t, x1, mod, per_token, w, tr, min(l, 512), min(l, 128))
    return y, cb, sd, sr


def kernel(x_prompt, x_sample, state_conv, state_delta, state_ret, c_prompt, c_sample, w_mod, b_mod, norm_mix_w, w_in, conv_w, a_log, dt_bias, dn_norm_w, ret_gn_w, ret_gn_b, w_down_a, w_down_b, w_out, norm_ffn_w, router_w, router_bias, w_gate, w_up, w_down, ws_gate, ws_up, ws_down, final_norm_w):
    bp, lp, _ = x_prompt.shape
    bs = x_sample.shape[0]
    half = RET_DK // 2
    w_in0 = w_in[0]
    c0, c1 = DN_CONV_W, DN_CONV_W + 2 * DN_HEADS
    w_ba = jnp.pad(w_in0[:, c0:c1], ((0, 0), (0, BA_W - 2 * DN_HEADS))).astype(BF16)
    w = {
        "norm_mix": norm_mix_w[0].reshape(1, D_MODEL),
        "w_main": jnp.concatenate([w_in0[:, :c0], w_in0[:, c1:]], axis=1).astype(BF16),
        "w_ba": w_ba,
        "w_bat": w_in0[:, c0:c1].T.astype(BF16),
        "conv_w": conv_w[0], "a_log": a_log[0], "dt_bias": dt_bias[0], "dn_norm": dn_norm_w[0],
        "inv_freq": (ROPE_BASE ** (-jnp.arange(half, dtype=F32) / half)).reshape(1, half),
        "gn_w": ret_gn_w[0], "gn_b": ret_gn_b[0],
        "w_down_a": w_down_a[0].astype(BF16), "w_down_b": w_down_b[0].astype(BF16), "w_out": w_out[0].astype(BF16),
        "norm_ffn": norm_ffn_w[0].reshape(1, D_MODEL), "router_w": router_w[0], "router_bias": router_bias[0],
        "w_gate": w_gate[0].astype(BF16), "w_up": w_up[0].astype(BF16), "w_down": w_down[0].astype(BF16),
        "ws_gate": ws_gate[0].astype(BF16), "ws_up": ws_up[0].astype(BF16), "ws_down": ws_down[0].astype(BF16),
        "final_norm": final_norm_w.reshape(1, D_MODEL),
    }
    mod = _modulation(jnp.concatenate([c_prompt, c_sample], axis=0), w_mod[0], b_mod[0])
    mod_p = mod[:bp].reshape(bp, 1, MOD_CHUNKS * D_MODEL)
    mod_s = mod[bp:].reshape(1, bs, MOD_CHUNKS * D_MODEL)

    y_p, conv_p, delta_p, ret_p = _trunk(x_prompt, mod_p, False, None, w)
    y_s, conv_s, delta_s, ret_s = _trunk(x_sample.reshape(1, bs, D_MODEL), mod_s, True,
                                         (state_conv[0], state_delta[0], state_ret[0]), w)
    return (y_p, y_s.reshape(bs, 1, D_MODEL), conv_p[None], delta_p[None], ret_p[None],
            conv_s[None], delta_s[None], ret_s[None])
```

```python
import functools
import math

import jax
import jax.numpy as jnp
from jax import lax
from jax.experimental import pallas as pl
from jax.experimental.pallas import tpu as pltpu

F32 = jnp.float32
BF16 = jnp.bfloat16

D_MODEL = 1024
DN_HEADS, DN_DK, DN_DV, DN_CONV = 8, 128, 128, 4
DN_QK = DN_HEADS * DN_DK
DN_VW = DN_HEADS * DN_DV
DN_CONV_W = 2 * DN_QK + DN_VW
RET_HEADS, RET_DK, RET_DV = 4, 256, 512
RET_QK = RET_HEADS * RET_DK
RET_VW = RET_HEADS * RET_DV
ROPE_BASE = 10000.0
PAST_LEN = 16384
N_EXPERTS, TOP_K, N_GROUPS, TOPK_GROUPS = 64, 8, 8, 4
GROUP_SIZE = N_EXPERTS // N_GROUPS
D_EXPERT, D_SHARED = 256, 256
ROUTED_SCALE = 2.5
MOD_CHUNKS = 6
EPS = 1e-6

PROJ_W = 12 * D_MODEL
COL_Q_A, COL_K_A, COL_V_A, COL_Z_A = 0, 1, 2, 3
COL_Q_B, COL_K_B = 4, 5
COL_V_B, COL_G_B = 3, 4
COL_GATE_A, COL_GATE_B = 10, 11
BA_W = 128
CHUNK = 128
CONV_HALO = 8
VMEM_LIMIT = 56 * 1024 * 1024
LANES = 128
MOE_BLOCK = 512
TOKEN_TILE = (D_MODEL // LANES, LANES)

NN = (((1,), (0,)), ((), ()))
NT = (((1,), (1,)), ((), ()))
TN = (((0,), (0,)), ((), ()))


def _mm(a, b, dims=NN):
    return lax.dot_general(a.astype(BF16), b.astype(BF16), dims, preferred_element_type=F32)


def _mm_hi(a, b, dims=NN):
    return lax.dot_general(a.astype(F32), b.astype(F32), dims, precision=lax.Precision.HIGHEST,
                           preferred_element_type=F32)


def _hi_lo(x):
    hi = x.astype(BF16)
    return hi, (x - hi.astype(F32)).astype(BF16)


def _split_lhs(a):
    hi, lo = _hi_lo(a)
    return jnp.concatenate([hi, lo, hi], axis=1)


def _split_rhs(b):
    hi, lo = _hi_lo(b)
    return jnp.concatenate([hi, hi, lo], axis=0)


def _mm_split(a3, b3):
    return lax.dot_general(a3, b3, NN, preferred_element_type=F32)


def _sigmoid(x):
    return jax.nn.sigmoid(x)


def _silu(x):
    return x * _sigmoid(x)


def _softplus(x):
    return jnp.maximum(x, 0.0) + jnp.log1p(jnp.exp(-jnp.abs(x)))


def _rms(x, w):
    return x * lax.rsqrt(jnp.mean(x * x, axis=-1, keepdims=True) + EPS) * w


def _params(*sem):
    return pltpu.CompilerParams(dimension_semantics=sem, vmem_limit_bytes=VMEM_LIMIT)


def _mod_kernel(c_ref, w_ref, b_ref, o_ref):
    o_ref[...] = _mm_hi(_silu(c_ref[...]), w_ref[...]) + b_ref[...]


def _modulation(c, w_mod, b_mod):
    n = c.shape[0]
    tn = 1536
    return pl.pallas_call(
        _mod_kernel,
        out_shape=jax.ShapeDtypeStruct((n, MOD_CHUNKS * D_MODEL), F32),
        grid=(MOD_CHUNKS * D_MODEL // tn,),
        in_specs=[pl.BlockSpec((n, D_MODEL), lambda j: (0, 0)),
                  pl.BlockSpec((D_MODEL, tn), lambda j: (0, j)),
                  pl.BlockSpec((1, tn), lambda j: (0, j))],
        out_specs=pl.BlockSpec((n, tn), lambda j: (0, j)),
        compiler_params=_params("parallel"),
        name="modulation",
    )(c, w_mod, b_mod.reshape(1, -1))


def _mod_spec(per_token, tm, col):
    if per_token:
        return pl.BlockSpec((None, tm, D_MODEL), lambda b, i, *_: (b, i, col))
    return pl.BlockSpec((None, 1, D_MODEL), lambda b, i, *_: (b, 0, col))


def _inproj_kernel(x_ref, sh_ref, sc_ref, nw_ref, w_ref, wba_ref, wbat_ref, out_ref, ba_ref, bat_ref, h_scr):
    @pl.when(pl.program_id(2) == 0)
    def _():
        h = _rms(x_ref[...], nw_ref[...]) * (1.0 + sc_ref[...]) + sh_ref[...]
        hb = h.astype(BF16)
        h_scr[...] = hb
        ba_ref[...] = _mm(hb, wba_ref[...])
        bat_ref[...] = _mm(wbat_ref[...], hb, NT)

    out_ref[...] = _mm(h_scr[...], w_ref[...]).astype(BF16)


def _in_projection(x, mod, per_token, norm_w, w_main, w_ba, w_bat, tm, tn):
    nb, l, _ = x.shape
    return pl.pallas_call(
        _inproj_kernel,
        out_shape=(jax.ShapeDtypeStruct((nb, l, PROJ_W), BF16),
                   jax.ShapeDtypeStruct((nb, l, BA_W), F32),
                   jax.ShapeDtypeStruct((nb, 2 * DN_HEADS, l), F32)),
        grid=(nb, l // tm, PROJ_W // tn),
        in_specs=[pl.BlockSpec((None, tm, D_MODEL), lambda b, i, j: (b, i, 0)),
                  _mod_spec(per_token, tm, 0),
                  _mod_spec(per_token, tm, 1),
                  pl.BlockSpec((1, D_MODEL), lambda b, i, j: (0, 0)),
                  pl.BlockSpec((D_MODEL, tn), lambda b, i, j: (0, j)),
                  pl.BlockSpec((D_MODEL, BA_W), lambda b, i, j: (0, 0)),
                  pl.BlockSpec((2 * DN_HEADS, D_MODEL), lambda b, i, j: (0, 0))],
        out_specs=(pl.BlockSpec((None, tm, tn), lambda b, i, j: (b, i, j)),
                   pl.BlockSpec((None, tm, BA_W), lambda b, i, j: (b, i, 0)),
                   pl.BlockSpec((None, 2 * DN_HEADS, tm), lambda b, i, j: (b, 0, i))),
        scratch_shapes=[pltpu.VMEM((tm, D_MODEL), BF16)],
        compiler_params=_params("parallel", "parallel", "arbitrary"),
        name="in_projection",
    )(x, mod, mod, norm_w, w_main, w_ba, w_bat)


def _delta_prefill_kernel(q_ref, k_ref, v_ref, z_ref, bat_ref, cw_ref, alog_c_ref, dtb_c_ref, nw_ref, s0_ref, c0_ref,
                          o_ref, sout_ref, cout_ref,
                          ext_scr, s_scr, qkv_scr, pow_scr, inv_scr, uw_scr, phi_scr, plo_scr):
    step = pl.program_id(1)
    c = CHUNK
    lo = CONV_HALO - (DN_CONV - 1)

    @pl.when(step == 0)
    def _():
        s_scr[...] = s0_ref[...]
        ext_scr[lo:CONV_HALO, :] = c0_ref[...]

    ext_scr[CONV_HALO:, 0:DN_QK] = q_ref[...].astype(F32)
    ext_scr[CONV_HALO:, DN_QK:2 * DN_QK] = k_ref[...].astype(F32)
    ext_scr[CONV_HALO:, 2 * DN_QK:] = v_ref[...].astype(F32)

    row = lax.broadcasted_iota(jnp.int32, (c, c), 0)
    col = lax.broadcasted_iota(jnp.int32, (c, c), 1)
    causal = row >= col
    strict = row > col
    eye = (row == col).astype(F32)
    utri = (row <= col).astype(F32)

    bat = bat_ref[...]
    g_t = -jnp.exp(alog_c_ref[...]) * _softplus(bat[DN_HEADS:, :] + dtb_c_ref[...])
    gc_t = _mm_hi(g_t, utri)
    gates = _mm_hi(jnp.concatenate([_sigmoid(bat[:DN_HEADS, :]), gc_t], axis=0), _eye(2 * DN_HEADS), TN)
    beta_tok, gc_tok = gates[:, :DN_HEADS], gates

    def conv(c0):
        acc = ext_scr[lo:lo + c, c0:c0 + DN_DK] * cw_ref[0:1, c0:c0 + DN_DK]
        for i in range(1, DN_CONV):
            acc = acc + ext_scr[lo + i:lo + i + c, c0:c0 + DN_DK] * cw_ref[i:i + 1, c0:c0 + DN_DK]
        return _silu(acc)

    def decay_of(h):
        gc = gc_tok[:, DN_HEADS + h:DN_HEADS + h + 1]
        return gc, jnp.where(causal, jnp.exp(jnp.where(causal, gc - gc_t[h:h + 1, :], 0.0)), 0.0)

    for h in range(DN_HEADS):
        q = conv(h * DN_DK)
        k = conv(DN_QK + h * DN_DK)
        q = q * lax.rsqrt(jnp.sum(q * q, axis=-1, keepdims=True) + EPS) * (DN_DK ** -0.5)
        k = k * lax.rsqrt(jnp.sum(k * k, axis=-1, keepdims=True) + EPS)
        qkv_scr[:, h * DN_DK:(h + 1) * DN_DK] = q
        qkv_scr[:, DN_QK + h * DN_DK:DN_QK + (h + 1) * DN_DK] = k
        qkv_scr[:, 2 * DN_QK + h * DN_DV:2 * DN_QK + (h + 1) * DN_DV] = conv(2 * DN_QK + h * DN_DV)
        _, decay = decay_of(h)
        a_mat = jnp.where(strict, _mm(k * beta_tok[:, h:h + 1], k, NT) * decay, 0.0)
        phi_scr[h], plo_scr[h] = _hi_lo(a_mat)
        inv_scr[h] = eye - a_mat

    for _ in range(int(math.log2(c)) - 1):
        for h in range(DN_HEADS):
            p_hi, p_lo = phi_scr[h], plo_scr[h]
            phi_scr[h], plo_scr[h] = _hi_lo(_mm_split(jnp.concatenate([p_hi, p_lo, p_hi], axis=1),
                                                      jnp.concatenate([p_hi, p_hi, p_lo], axis=0)))
        for h in range(DN_HEADS):
            inv = inv_scr[h]
            p_hi, p_lo = phi_scr[h], plo_scr[h]
            inv_scr[h] = inv + _mm_split(_split_lhs(inv), jnp.concatenate([p_hi, p_hi, p_lo], axis=0))

    def qkv_of(h):
        return (qkv_scr[:, h * DN_DK:(h + 1) * DN_DK], qkv_scr[:, DN_QK + h * DN_DK:DN_QK + (h + 1) * DN_DK],
                qkv_scr[:, 2 * DN_QK + h * DN_DV:2 * DN_QK + (h + 1) * DN_DV])

    for h in range(DN_HEADS):
        q, k, v = qkv_of(h)
        beta = beta_tok[:, h:h + 1]
        gc, decay = decay_of(h)
        rhs = jnp.concatenate([v * beta, k * beta * jnp.exp(gc)], axis=-1)
        uw_scr[h] = _mm_split(_split_lhs(inv_scr[h]), _split_rhs(rhs))
        pow_scr[h] = jnp.where(causal, _mm(q, k, NT) * decay, 0.0)
    for h in range(DN_HEADS):
        inv_scr[h] = uw_scr[h, :, 0:DN_DV] - _mm(uw_scr[h, :, DN_DV:], s_scr[h])
    for h in range(DN_HEADS):
        q, k, _ = qkv_of(h)
        gc = gc_tok[:, DN_HEADS + h:DN_HEADS + h + 1]
        g_last = gc[c - 1:c, :]
        s = s_scr[h]
        v_new = inv_scr[h]
        o = _mm(q * jnp.exp(gc), s) + _mm(pow_scr[h], v_new)
        s_scr[h] = s * jnp.exp(g_last) + _mm(k * jnp.exp(g_last - gc), v_new, TN)
        z = z_ref[:, h * DN_DV:(h + 1) * DN_DV].astype(F32)
        o_ref[:, h * DN_DV:(h + 1) * DN_DV] = (_rms(o, nw_ref[...]) * _silu(z)).astype(BF16)

    ext_scr[0:CONV_HALO, :] = ext_scr[c:c + CONV_HALO, :]

    @pl.when(step == pl.num_programs(1) - 1)
    def _():
        sout_ref[...] = s_scr[...]
        cout_ref[...] = ext_scr[lo:CONV_HALO, :]


def _lane_pad(v, offset):
    return jnp.zeros((1, BA_W), F32).at[0, offset:offset + v.shape[0]].set(v.astype(F32))


def _delta_prefill(proj, bat, conv_w, a_log, dt_bias, dn_norm_w, s0, c0):
    nb, l, _ = proj.shape
    c = CHUNK
    col_spec = lambda j: pl.BlockSpec((None, c, D_MODEL), lambda b, i: (b, i, j))
    full = lambda shape: pl.BlockSpec(shape, lambda b, i: (0,) * len(shape))
    return pl.pallas_call(
        _delta_prefill_kernel,
        out_shape=(jax.ShapeDtypeStruct((nb, l, DN_VW), BF16),
                   jax.ShapeDtypeStruct((nb, DN_HEADS, DN_DK, DN_DV), F32),
                   jax.ShapeDtypeStruct((nb, DN_CONV - 1, DN_CONV_W), F32)),
        grid=(nb, l // c),
        in_specs=[col_spec(COL_Q_A), col_spec(COL_K_A), col_spec(COL_V_A), col_spec(COL_Z_A),
                  pl.BlockSpec((None, 2 * DN_HEADS, c), lambda b, i: (b, 0, i)),
                  full((DN_CONV, DN_CONV_W)), full((DN_HEADS, 1)), full((DN_HEADS, 1)), full((1, DN_DV)),
                  pl.BlockSpec((None, DN_HEADS, DN_DK, DN_DV), lambda b, i: (b, 0, 0, 0)),
                  pl.BlockSpec((None, DN_CONV - 1, DN_CONV_W), lambda b, i: (b, 0, 0))],
        out_specs=(pl.BlockSpec((None, c, DN_VW), lambda b, i: (b, i, 0)),
                   pl.BlockSpec((None, DN_HEADS, DN_DK, DN_DV), lambda b, i: (b, 0, 0, 0)),
                   pl.BlockSpec((None, DN_CONV - 1, DN_CONV_W), lambda b, i: (b, 0, 0))),
        scratch_shapes=[pltpu.VMEM((c + CONV_HALO, DN_CONV_W), F32),
                        pltpu.VMEM((DN_HEADS, DN_DK, DN_DV), F32),
                        pltpu.VMEM((c, DN_CONV_W), F32),
                        pltpu.VMEM((DN_HEADS, c, c), F32),
                        pltpu.VMEM((DN_HEADS, c, c), F32),
                        pltpu.VMEM((DN_HEADS, c, DN_DV + DN_DK), F32),
                        pltpu.VMEM((DN_HEADS, c, c), BF16),
                        pltpu.VMEM((DN_HEADS, c, c), BF16)],
        compiler_params=_params("parallel", "arbitrary"),
        name="delta_prefill",
    )(proj, proj, proj, proj, bat, conv_w, a_log.reshape(DN_HEADS, 1), dt_bias.reshape(DN_HEADS, 1),
      dn_norm_w.reshape(1, DN_DV), s0, c0)


def _log_gamma(h):
    return math.log(1.0 - 2.0 ** (-5.0 - h))


def _rotate(x, cos, sin):
    half = x.shape[-1] // 2
    x1, x2 = x[:, :half], x[:, half:]
    return jnp.concatenate([x1 * cos - x2 * sin, x1 * sin + x2 * cos], axis=-1)


def _group_norm_gate(o, gw, gb, gate):
    mu = jnp.mean(o, axis=-1, keepdims=True)
    var = jnp.mean(jnp.square(o - mu), axis=-1, keepdims=True)
    return ((o - mu) * lax.rsqrt(var + EPS) * gw + gb) * _silu(gate)


def _ret_prefill_kernel(q_ref, k_ref, v_ref, g_ref, inv_ref, gw_ref, gb_ref, s0_ref, o_ref, sout_ref, s_scr):
    step = pl.program_id(1)
    c = CHUNK

    @pl.when(step == 0)
    def _():
        s_scr[...] = s0_ref[...]

    idx_c = lax.broadcasted_iota(jnp.int32, (c, 1), 0)
    pos = (step * c + idx_c).astype(F32)
    ang = pos * inv_ref[...]
    cos, sin = jnp.cos(ang), jnp.sin(ang)
    rel = (lax.broadcasted_iota(jnp.int32, (c, c), 0) - lax.broadcasted_iota(jnp.int32, (c, c), 1)).astype(F32)
    idx = idx_c.astype(F32)

    for h in range(RET_HEADS):
        lg = _log_gamma(h)
        q = _rotate(q_ref[:, h * RET_DK:(h + 1) * RET_DK].astype(F32), cos, sin)
        k = _rotate(k_ref[:, h * RET_DK:(h + 1) * RET_DK].astype(F32), cos, sin) * (RET_DK ** -0.5)
        v = v_ref[:, h * RET_DV:(h + 1) * RET_DV]
        d_mat = jnp.where(rel >= 0, jnp.exp(jnp.maximum(rel, 0.0) * lg), 0.0)
        intra = _mm(_mm(q, k, NT) * d_mat, v)
        s = s_scr[h]
        cross = _mm(q * jnp.exp((idx + 1.0) * lg), s)
        s_scr[h] = s * math.exp(c * lg) + _mm(k * jnp.exp((c - 1.0 - idx) * lg), v, TN)
        sl = slice(h * RET_DV, (h + 1) * RET_DV)
        o_ref[:, sl] = _group_norm_gate(intra + cross, gw_ref[:, sl], gb_ref[:, sl],
                                        g_ref[:, sl].astype(F32)).astype(BF16)

    @pl.when(step == pl.num_programs(1) - 1)
    def _():
        sout_ref[...] = s_scr[...]


def _ret_prefill(proj, inv_freq, gn_w, gn_b, s0):
    nb, l, _ = proj.shape
    c = CHUNK
    full = lambda shape: pl.BlockSpec(shape, lambda b, i: (0,) * len(shape))
    state_spec = pl.BlockSpec((None, RET_HEADS, RET_DK, RET_DV), lambda b, i: (b, 0, 0, 0))
    return pl.pallas_call(
        _ret_prefill_kernel,
        out_shape=(jax.ShapeDtypeStruct((nb, l, RET_VW), BF16),
                   jax.ShapeDtypeStruct((nb, RET_HEADS, RET_DK, RET_DV), F32)),
        grid=(nb, l // c),
        in_specs=[pl.BlockSpec((None, c, RET_QK), lambda b, i: (b, i, COL_Q_B)),
                  pl.BlockSpec((None, c, RET_QK), lambda b, i: (b, i, COL_K_B)),
                  pl.BlockSpec((None, c, RET_VW), lambda b, i: (b, i, COL_V_B)),
                  pl.BlockSpec((None, c, RET_VW), lambda b, i: (b, i, COL_G_B)),
                  full((1, RET_DK // 2)), full((1, RET_VW)), full((1, RET_VW)), state_spec],
        out_specs=(pl.BlockSpec((None, c, RET_VW), lambda b, i: (b, i, 0)), state_spec),
        scratch_shapes=[pltpu.VMEM((RET_HEADS, RET_DK, RET_DV), F32)],
        compiler_params=_params("parallel", "arbitrary"),
        name="retention_prefill",
    )(proj, proj, proj, proj, inv_freq, gn_w.reshape(1, RET_VW), gn_b.reshape(1, RET_VW), s0)


ROWS = 8
SEQ_PER_STEP = 4


def _row_mm(a, s):
    return _mm(jnp.broadcast_to(a, (ROWS, a.shape[-1])), s)[0:1, :]


def _outer(a, b):
    first = lax.broadcasted_iota(jnp.int32, (ROWS, a.shape[-1]), 0) == 0
    a8 = jnp.where(first, jnp.broadcast_to(a, (ROWS, a.shape[-1])), 0.0)
    return _mm(a8, jnp.broadcast_to(b, (ROWS, b.shape[-1])), TN)


def _delta_step_kernel(qkv_ref, z_ref, ba_ref, cw_ref, alog_ref, dtb_ref, nw_ref, s_ref, c_ref,
                       o_ref, sout_ref, cout_ref):
    for j in range(qkv_ref.shape[0]):
        u_new = qkv_ref[j].astype(F32)
        buf = c_ref[j]
        acc = u_new * cw_ref[DN_CONV - 1:DN_CONV, :]
        for i in range(DN_CONV - 1):
            acc = acc + buf[i:i + 1, :] * cw_ref[i:i + 1, :]
        qkv = _silu(acc)
        cout_ref[j, 0:DN_CONV - 2, :] = buf[1:, :]
        cout_ref[j, DN_CONV - 2:, :] = u_new
        ba = ba_ref[j]
        beta_all = _sigmoid(ba)
        g_all = -jnp.exp(alog_ref[...]) * _softplus(ba + dtb_ref[...])
        for h in range(DN_HEADS):
            q = qkv[:, h * DN_DK:(h + 1) * DN_DK]
            k = qkv[:, DN_QK + h * DN_DK:DN_QK + (h + 1) * DN_DK]
            v = qkv[:, 2 * DN_QK + h * DN_DV:2 * DN_QK + (h + 1) * DN_DV]
            q = q * lax.rsqrt(jnp.sum(q * q, axis=-1, keepdims=True) + EPS) * (DN_DK ** -0.5)
            k = k * lax.rsqrt(jnp.sum(k * k, axis=-1, keepdims=True) + EPS)
            beta = beta_all[:, h:h + 1]
            eg = jnp.exp(g_all[:, DN_HEADS + h:DN_HEADS + h + 1])
            s = s_ref[j, h]
            kb = k * beta
            v_new = v * beta - _row_mm(kb * eg, s)
            o = _row_mm(q * eg, s) + jnp.sum(q * k, axis=-1, keepdims=True) * v_new
            sout_ref[j, h] = s * eg + _outer(k, v_new)
            z = z_ref[j, :, h * DN_DV:(h + 1) * DN_DV].astype(F32)
            o_ref[j, :, h * DN_DV:(h + 1) * DN_DV] = (_rms(o, nw_ref[...]) * _silu(z)).astype(BF16)


def _delta_step(proj, ba, conv_w, a_log, dt_bias, dn_norm_w, s0, c0):
    n = proj.shape[0]
    ns = math.gcd(n, SEQ_PER_STEP)
    full = lambda shape: pl.BlockSpec(shape, lambda b: (0,) * len(shape))
    state_spec = pl.BlockSpec((ns, DN_HEADS, DN_DK, DN_DV), lambda b: (b, 0, 0, 0))
    conv_spec = pl.BlockSpec((ns, DN_CONV - 1, DN_CONV_W), lambda b: (b, 0, 0))
    return pl.pallas_call(
        _delta_step_kernel,
        out_shape=(jax.ShapeDtypeStruct((n, 1, DN_VW), BF16),
                   jax.ShapeDtypeStruct((n, DN_HEADS, DN_DK, DN_DV), F32),
                   jax.ShapeDtypeStruct((n, DN_CONV - 1, DN_CONV_W), F32)),
        grid=(n // ns,),
        in_specs=[pl.BlockSpec((ns, 1, DN_CONV_W), lambda b: (b, 0, 0)),
                  pl.BlockSpec((ns, 1, DN_VW), lambda b: (b, 0, COL_Z_A)),
                  pl.BlockSpec((ns, 1, BA_W), lambda b: (b, 0, 0)),
                  full((DN_CONV, DN_CONV_W)), full((1, BA_W)), full((1, BA_W)), full((1, DN_DV)),
                  state_spec, conv_spec],
        out_specs=(pl.BlockSpec((ns, 1, DN_VW), lambda b: (b, 0, 0)), state_spec, conv_spec),
        compiler_params=_params("parallel"),
        name="delta_step",
    )(proj, proj, ba, conv_w, _lane_pad(a_log, DN_HEADS), _lane_pad(dt_bias, DN_HEADS),
      dn_norm_w.reshape(1, DN_DV), s0, c0)


def _ret_step_kernel(q_ref, k_ref, v_ref, g_ref, inv_ref, gw_ref, gb_ref, s_ref, o_ref, sout_ref):
    ang = float(PAST_LEN) * inv_ref[...]
    cos, sin = jnp.cos(ang), jnp.sin(ang)
    for h in range(RET_HEADS):
        gamma = math.exp(_log_gamma(h))
        q = _rotate(q_ref[:, h * RET_DK:(h + 1) * RET_DK].astype(F32), cos, sin)
        k = _rotate(k_ref[:, h * RET_DK:(h + 1) * RET_DK].astype(F32), cos, sin) * (RET_DK ** -0.5)
        v = v_ref[:, h * RET_DV:(h + 1) * RET_DV].astype(F32)
        s = s_ref[h]
        o = jnp.sum(q * k, axis=-1, keepdims=True) * v + _row_mm(q * gamma, s)
        sout_ref[h] = s * gamma + _outer(k, v)
        sl = slice(h * RET_DV, (h + 1) * RET_DV)
        o_ref[:, sl] = _group_norm_gate(o, gw_ref[:, sl], gb_ref[:, sl], g_ref[:, sl].astype(F32)).astype(BF16)


def _ret_step(proj, inv_freq, gn_w, gn_b, s0):
    n = proj.shape[0]
    full = lambda shape: pl.BlockSpec(shape, lambda b: (0,) * len(shape))
    state_spec = pl.BlockSpec((None, RET_HEADS, RET_DK, RET_DV), lambda b: (b, 0, 0, 0))
    return pl.pallas_call(
        _ret_step_kernel,
        out_shape=(jax.ShapeDtypeStruct((n, 1, RET_VW), BF16),
                   jax.ShapeDtypeStruct((n, RET_HEADS, RET_DK, RET_DV), F32)),
        grid=(n,),
        in_specs=[pl.BlockSpec((None, 1, RET_QK), lambda b: (b, 0, COL_Q_B)),
                  pl.BlockSpec((None, 1, RET_QK), lambda b: (b, 0, COL_K_B)),
                  pl.BlockSpec((None, 1, RET_VW), lambda b: (b, 0, COL_V_B)),
                  pl.BlockSpec((None, 1, RET_VW), lambda b: (b, 0, COL_G_B)),
                  full((1, RET_DK // 2)), full((1, RET_VW)), full((1, RET_VW)), state_spec],
        out_specs=(pl.BlockSpec((None, 1, RET_VW), lambda b: (b, 0, 0)), state_spec),
        compiler_params=_params("parallel"),
        name="retention_step",
    )(proj, proj, proj, proj, inv_freq, gn_w.reshape(1, RET_VW), gn_b.reshape(1, RET_VW), s0)


def _post_mixer_kernel(oa_ref, ob_ref, ga_ref, gb_ref, x_ref, g1_ref, sh2_ref, sc2_ref, wda_ref, wdb_ref,
                       wout_ref, nw_ref, rw_ref, x1_ref, h_ref, hf_ref, logit_ref):
    y_a = _mm(oa_ref[...], wda_ref[...])
    y_b = _mm(ob_ref[...], wdb_ref[...])
    merged = _sigmoid(ga_ref[...].astype(F32)) * y_a + _sigmoid(gb_ref[...].astype(F32)) * y_b
    x1 = x_ref[...] + g1_ref[...] * _mm(merged, wout_ref[...])
    x1_ref[...] = x1
    hf = _rms(x1, nw_ref[...]) * (1.0 + sc2_ref[...]) + sh2_ref[...]
    h_ref[...] = hf.astype(BF16)
    hf_ref[...] = hf.reshape(hf_ref.shape)
    logit_ref[...] = _mm_hi(rw_ref[...], hf, NT)


def _post_mixer(o_a, o_b, proj, x, mod, per_token, w_down_a, w_down_b, w_out, norm_w, router_w, tm):
    nb, l, _ = x.shape
    tok = lambda w, j=0: pl.BlockSpec((None, tm, w), lambda b, i: (b, i, j))
    full = lambda shape: pl.BlockSpec(shape, lambda b, i: (0,) * len(shape))
    return pl.pallas_call(
        _post_mixer_kernel,
        out_shape=(jax.ShapeDtypeStruct((nb, l, D_MODEL), F32),
                   jax.ShapeDtypeStruct((nb, l, D_MODEL), BF16),
                   jax.ShapeDtypeStruct((nb, l) + TOKEN_TILE, F32),
                   jax.ShapeDtypeStruct((nb, N_EXPERTS, l), F32)),
        grid=(nb, l // tm),
        in_specs=[tok(DN_VW), tok(RET_VW), tok(D_MODEL, COL_GATE_A), tok(D_MODEL, COL_GATE_B), tok(D_MODEL),
                  _mod_spec(per_token, tm, 2), _mod_spec(per_token, tm, 3), _mod_spec(per_token, tm, 4),
                  full((DN_VW, D_MODEL)), full((RET_VW, D_MODEL)), full((D_MODEL, D_MODEL)),
                  full((1, D_MODEL)), full((N_EXPERTS, D_MODEL))],
        out_specs=(tok(D_MODEL), tok(D_MODEL),
                   pl.BlockSpec((None, tm) + TOKEN_TILE, lambda b, i: (b, i, 0, 0)),
                   pl.BlockSpec((None, N_EXPERTS, tm), lambda b, i: (b, 0, i))),
        compiler_params=_params("parallel", "parallel"),
        name="post_mixer",
    )(o_a, o_b, proj, proj, x, mod, mod, mod, w_down_a, w_down_b, w_out, norm_w, router_w)


def _first_max(x, axis, n):
    m = jnp.max(x, axis=axis, keepdims=True)
    ids = lax.broadcasted_iota(jnp.int32, x.shape, axis)
    first = jnp.min(jnp.where(x == m, ids, n), axis=axis, keepdims=True)
    return m, ids == first


def _eye(n):
    return (lax.broadcasted_iota(jnp.int32, (n, n), 0) == lax.broadcasted_iota(jnp.int32, (n, n), 1)).astype(F32)


def _route_kernel(logit_ref, bias_ref, ek_ref, rk_ref, wkt_ref, cnt_ref, run_scr):
    @pl.when(jnp.logical_and(pl.program_id(0) == 0, pl.program_id(1) == 0))
    def _():
        run_scr[...] = jnp.zeros_like(run_scr)

    t = logit_ref.shape[-1]
    scores = _sigmoid(logit_ref[...])
    biased = scores + bias_ref[...]
    neg = jnp.float32(-jnp.inf)
    b3 = biased.reshape(N_GROUPS, GROUP_SIZE, t)
    m1, hit = _first_max(b3, 1, GROUP_SIZE)
    m2 = jnp.max(jnp.where(hit, neg, b3), axis=1, keepdims=True)
    gscore = (m1 + m2).reshape(N_GROUPS, t)
    gmask = jnp.zeros((N_GROUPS, t), jnp.bool_)
    for _ in range(TOPK_GROUPS):
        _, hit = _first_max(gscore, 0, N_GROUPS)
        gmask = jnp.logical_or(gmask, hit)
        gscore = jnp.where(hit, neg, gscore)
    emask = jnp.broadcast_to(gmask.reshape(N_GROUPS, 1, t), (N_GROUPS, GROUP_SIZE, t)).reshape(N_EXPERTS, t)
    masked = jnp.where(emask, biased, neg)
    eid = lax.broadcasted_iota(jnp.int32, (N_EXPERTS, t), 0).astype(F32)
    hits, e_rows, w_rows = [], [], []
    for _ in range(TOP_K):
        _, hit = _first_max(masked, 0, N_EXPERTS)
        hits.append(hit)
        e_rows.append(jnp.sum(jnp.where(hit, eid, 0.0), axis=0, keepdims=True))
        w_rows.append(jnp.sum(jnp.where(hit, scores, 0.0), axis=0, keepdims=True))
        masked = jnp.where(hit, neg, masked)
    sel = sum(hit.astype(F32) for hit in hits)
    before = lax.broadcasted_iota(jnp.int32, (t, t), 0) < lax.broadcasted_iota(jnp.int32, (t, t), 1)
    rank = run_scr[:, 0:1] + _mm(sel, before.astype(F32))
    rk_rows = [jnp.sum(jnp.where(hit, rank, 0.0), axis=0, keepdims=True) for hit in hits]
    run_scr[...] = run_scr[...] + jnp.sum(sel, axis=1, keepdims=True)
    wk = jnp.concatenate(w_rows, axis=0)
    wk = wk / jnp.sum(wk, axis=0, keepdims=True) * ROUTED_SCALE
    ek_ref[...] = jnp.concatenate(e_rows, axis=0).astype(jnp.int32)
    rk_ref[...] = jnp.concatenate(rk_rows, axis=0).astype(jnp.int32)
    wkt_ref[...] = _mm_hi(wk, _eye(TOP_K), TN)
    cnt_ref[...] = run_scr[...]


def _route(logits_t, router_bias, tr):
    nb, _, l = logits_t.shape
    pick = pl.BlockSpec((None, TOP_K, tr), lambda b, i: (b, 0, i))
    return pl.pallas_call(
        _route_kernel,
        out_shape=(jax.ShapeDtypeStruct((nb, TOP_K, l), jnp.int32),
                   jax.ShapeDtypeStruct((nb, TOP_K, l), jnp.int32),
                   jax.ShapeDtypeStruct((nb, l, TOP_K), F32),
                   jax.ShapeDtypeStruct((N_EXPERTS, LANES), F32)),
        grid=(nb, l // tr),
        in_specs=[pl.BlockSpec((None, N_EXPERTS, tr), lambda b, i: (b, 0, i)),
                  pl.BlockSpec((N_EXPERTS, 1), lambda b, i: (0, 0))],
        out_specs=(pick, pick, pl.BlockSpec((None, tr, TOP_K), lambda b, i: (b, i, 0)),
                   pl.BlockSpec((N_EXPERTS, LANES), lambda b, i: (0, 0))),
        scratch_shapes=[pltpu.VMEM((N_EXPERTS, LANES), F32)],
        compiler_params=_params("arbitrary", "arbitrary"),
        name="route",
    )(logits_t, router_bias.reshape(N_EXPERTS, 1))


def _plan_kernel(cnt_ref, ek_ref, rk_ref, pos_ref, be_ref, pe_ref, nu_ref, start_scr):
    @pl.when(jnp.logical_and(pl.program_id(0) == 0, pl.program_id(1) == 0))
    def _():
        padded = jnp.ceil(cnt_ref[...] * (1.0 / MOE_BLOCK)) * MOE_BLOCK
        r = lax.broadcasted_iota(jnp.int32, (N_EXPERTS, N_EXPERTS), 0)
        c = lax.broadcasted_iota(jnp.int32, (N_EXPERTS, N_EXPERTS), 1)
        pad_end = _mm_hi((r >= c).astype(F32), padded)
        start_scr[...] = pad_end - padded
        pe_ref[...] = pad_end.astype(jnp.int32)
        nu_ref[...] = (pad_end[N_EXPERTS - 1:, :] * (1.0 / MOE_BLOCK)).astype(jnp.int32)
        nbp = be_ref.shape[-1]
        first_row = lax.broadcasted_iota(jnp.int32, (1, nbp), 1).astype(F32) * MOE_BLOCK
        owner = jnp.sum((pad_end[:, 0:1] <= first_row).astype(F32), axis=0, keepdims=True)
        be_ref[...] = jnp.minimum(owner, N_EXPERTS - 1.0).astype(jnp.int32)

    t = ek_ref.shape[-1]
    eid = lax.broadcasted_iota(jnp.int32, (N_EXPERTS, t), 0)
    start = start_scr[:, 0:1]
    rows = [jnp.sum(jnp.where(eid == ek_ref[k:k + 1, :], start, 0.0), axis=0, keepdims=True)
            for k in range(TOP_K)]
    pos_ref[...] = jnp.concatenate(rows, axis=0).astype(jnp.int32) + rk_ref[...]


def _plan(counts, ek, rk, tr, n_blocks):
    nb, _, l = ek.shape
    nbp = -(-n_blocks // LANES) * LANES
    pick = pl.BlockSpec((None, TOP_K, tr), lambda b, i: (b, 0, i))
    const = lambda shape: pl.BlockSpec(shape, lambda b, i: (0, 0))
    return pl.pallas_call(
        _plan_kernel,
        out_shape=(jax.ShapeDtypeStruct((nb, TOP_K, l), jnp.int32),
                   jax.ShapeDtypeStruct((1, nbp), jnp.int32),
                   jax.ShapeDtypeStruct((N_EXPERTS, LANES), jnp.int32),
                   jax.ShapeDtypeStruct((1, LANES), jnp.int32)),
        grid=(nb, l // tr),
        in_specs=[const((N_EXPERTS, LANES)), pick, pick],
        out_specs=(pick, const((1, nbp)), const((N_EXPERTS, LANES)), const((1, LANES))),
        scratch_shapes=[pltpu.VMEM((N_EXPERTS, LANES), F32)],
        compiler_params=_params("arbitrary", "arbitrary"),
        name="plan",
    )(counts, ek, rk)


def _dispatch_kernel(pe_ref, pos_ref, x_ref, xs_hbm, zero_scr, sem):
    b, i = pl.program_id(0), pl.program_id(1)
    tm = pos_ref.shape[-1]

    def row(k, t):
        return pltpu.make_async_copy(x_ref.at[t], xs_hbm.at[pos_ref[k, t]], sem)

    @pl.when(jnp.logical_and(b == 0, i == 0))
    def _():
        zero_scr[...] = jnp.zeros_like(zero_scr)

        def tail_copy(e):
            return pltpu.make_async_copy(zero_scr, xs_hbm.at[pl.ds(pe_ref[e] - MOE_BLOCK, MOE_BLOCK)], sem)

        def nonempty(e):
            return pe_ref[e] > jnp.where(e > 0, pe_ref[jnp.maximum(e - 1, 0)], 0)

        @pl.loop(0, N_EXPERTS)
        def _(e):
            @pl.when(nonempty(e))
            def _():
                tail_copy(e).start()

        @pl.loop(0, N_EXPERTS)
        def _(e):
            @pl.when(nonempty(e))
            def _():
                tail_copy(e).wait()

    @pl.loop(0, tm)
    def _(t):
        for k in range(TOP_K):
            row(k, t).start(priority=k % 2)

    @pl.loop(0, tm)
    def _(t):
        for k in range(TOP_K):
            row(k, t).wait()


def _dispatch(x, pos, pad_end, n_rows, tm):
    nb, l = x.shape[:2]
    return pl.pallas_call(
        _dispatch_kernel,
        out_shape=jax.ShapeDtypeStruct((n_rows,) + TOKEN_TILE, F32),
        grid_spec=pltpu.PrefetchScalarGridSpec(
            num_scalar_prefetch=1,
            grid=(nb, l // tm),
            in_specs=[pl.BlockSpec((None, TOP_K, tm), lambda b, i, pe: (b, 0, i), memory_space=pltpu.SMEM),
                      pl.BlockSpec((tm,) + TOKEN_TILE, lambda b, i, pe: (b * (l // tm) + i, 0, 0))],
            out_specs=pl.BlockSpec(memory_space=pl.ANY),
            scratch_shapes=[pltpu.VMEM((MOE_BLOCK,) + TOKEN_TILE, F32), pltpu.SemaphoreType.DMA(())]),
        compiler_params=_params("arbitrary", "arbitrary"),
        name="dispatch",
    )(pad_end, pos, x.reshape((nb * l,) + TOKEN_TILE))


def _experts_kernel(be_ref, nu_ref, x_ref, wg_ref, wu_ref, wd_ref, y_ref):
    @pl.when(pl.program_id(0) < nu_ref[0])
    def _():
        xb = x_ref[...].reshape(MOE_BLOCK, D_MODEL).astype(BF16)
        y = _mm(_silu(_mm(xb, wg_ref[...])) * _mm(xb, wu_ref[...]), wd_ref[...])
        y_ref[...] = y.reshape(y_ref.shape)


def _experts(xs, block_expert, n_used, w_gate, w_up, w_down):
    n_rows, d = xs.shape[0], D_MODEL
    used = lambda i, nu: jnp.minimum(i, nu[0] - 1)
    rows = pl.BlockSpec((MOE_BLOCK,) + TOKEN_TILE, lambda i, be, nu: (used(i, nu), 0, 0))
    return pl.pallas_call(
        _experts_kernel,
        out_shape=jax.ShapeDtypeStruct((n_rows,) + TOKEN_TILE, F32),
        grid_spec=pltpu.PrefetchScalarGridSpec(
            num_scalar_prefetch=2,
            grid=(n_rows // MOE_BLOCK,),
            in_specs=[rows,
                      pl.BlockSpec((None, d, D_EXPERT), lambda i, be, nu: (be[used(i, nu)], 0, 0)),
                      pl.BlockSpec((None, d, D_EXPERT), lambda i, be, nu: (be[used(i, nu)], 0, 0)),
                      pl.BlockSpec((None, D_EXPERT, d), lambda i, be, nu: (be[used(i, nu)], 0, 0))],
            out_specs=rows),
        compiler_params=_params("arbitrary"),
        name="experts",
    )(block_expert, n_used, xs, w_gate, w_up, w_down)


def _combine_kernel(pos_ref, pos_next_ref, wkt_ref, ys_hbm, h_ref, x1_ref, g2_ref, wsg_ref, wsu_ref, wsd_ref,
                    fw_ref, y_ref, rows_scr, sem):
    s = pl.program_id(0)
    tm = pos_ref.shape[-1]
    slot = lax.rem(s, 2)

    def row(p_ref, sl, k, t):
        return pltpu.make_async_copy(ys_hbm.at[p_ref[k, t]], rows_scr.at[sl, k, t], sem.at[sl])

    def fetch(p_ref, sl):
        @pl.loop(0, tm)
        def _(t):
            for k in range(TOP_K):
                row(p_ref, sl, k, t).start(priority=k % 2)

    @pl.when(s == 0)
    def _():
        fetch(pos_ref, 0)

    @pl.when(s + 1 < pl.num_programs(0))
    def _():
        fetch(pos_next_ref, 1 - slot)

    hb = h_ref[...]
    acc = _mm(_silu(_mm(hb, wsg_ref[...])) * _mm(hb, wsu_ref[...]), wsd_ref[...])

    @pl.loop(0, tm)
    def _(t):
        for k in range(TOP_K):
            row(pos_ref, slot, k, t).wait()

    wkt = wkt_ref[...]
    for k in range(TOP_K):
        acc = acc + rows_scr[slot, k].reshape(tm, D_MODEL) * wkt[:, k:k + 1]
    y_ref[...] = _rms(x1_ref[...] + g2_ref[...] * acc, fw_ref[...])


def _combine(ys, pos, wkt, hffn, x1, mod, per_token, ws_gate, ws_up, ws_down, final_w, tm):
    nb, l, d = x1.shape
    tps = l // tm
    n_tiles = nb * tps
    tok = lambda w: pl.BlockSpec((None, tm, w), lambda s: (s // tps, s % tps, 0))
    full = lambda shape: pl.BlockSpec(shape, lambda s: (0,) * len(shape))
    pick = lambda step: pl.BlockSpec((None, TOP_K, tm), lambda s: (step(s) // tps, 0, step(s) % tps),
                                     memory_space=pltpu.SMEM)
    if per_token:
        g2_spec = pl.BlockSpec((None, tm, d), lambda s: (s // tps, s % tps, 5))
    else:
        g2_spec = pl.BlockSpec((None, 1, d), lambda s: (s // tps, 0, 5))
    return pl.pallas_call(
        _combine_kernel,
        out_shape=jax.ShapeDtypeStruct((nb, l, d), F32),
        grid=(n_tiles,),
        in_specs=[pick(lambda s: s), pick(lambda s: jnp.minimum(s + 1, n_tiles - 1)),
                  tok(TOP_K), pl.BlockSpec(memory_space=pl.ANY), tok(d), tok(d), g2_spec,
                  full((d, D_SHARED)), full((d, D_SHARED)), full((D_SHARED, d)), full((1, d))],
        out_specs=tok(d),
        scratch_shapes=[pltpu.VMEM((2, TOP_K, tm) + TOKEN_TILE, F32), pltpu.SemaphoreType.DMA((2,))],
        compiler_params=_params("arbitrary"),
        name="combine",
    )(pos, pos, wkt, ys, hffn, x1, mod, ws_gate, ws_up, ws_down, final_w)


def _moe(hffn, hf32, logits_t, x1, mod, per_token, w, tr, tm_dispatch, tm_combine):
    nb, l, _ = x1.shape
    n_blocks = nb * l * TOP_K // MOE_BLOCK + N_EXPERTS
    ek, rk, wkt, counts = _route(logits_t, w["router_bias"], tr)
    pos, block_expert, pad_end, n_used = _plan(counts, ek, rk, tr, n_blocks)
    xs = _dispatch(hf32, pos, pad_end[:, 0], n_blocks * MOE_BLOCK, tm_dispatch)
    ys = _experts(xs, block_expert[0, :n_blocks], n_used[0, :1], w["w_gate"], w["w_up"], w["w_down"])
    return _combine(ys, pos, wkt, hffn, x1, mod, per_token, w["ws_gate"], w["ws_up"], w["ws_down"],
                    w["final_norm"], tm_combine)


def _trunk(x, mod, per_token, states, w):
    nb, l, _ = x.shape
    tm_proj, tn_proj, tm_post, tr = min(l, 1024), 2048, min(l, 512), min(l, 512)
    proj, ba, bat = _in_projection(x, mod, per_token, w["norm_mix"], w["w_main"], w["w_ba"], w["w_bat"],
                                   tm_proj, tn_proj)
    if states is None:
        c0 = jnp.zeros((nb, DN_CONV - 1, DN_CONV_W), F32)
        sd0 = jnp.zeros((nb, DN_HEADS, DN_DK, DN_DV), F32)
        sr0 = jnp.zeros((nb, RET_HEADS, RET_DK, RET_DV), F32)
        o_a, sd, cb = _delta_prefill(proj, bat, w["conv_w"], w["a_log"], w["dt_bias"], w["dn_norm"], sd0, c0)
        o_b, sr = _ret_prefill(proj, w["inv_freq"], w["gn_w"], w["gn_b"], sr0)
    else:
        c0, sd0, sr0 = states
        n = nb * l
        o_a, sd, cb = _delta_step(proj.reshape(n, 1, PROJ_W), ba.reshape(n, 1, BA_W), w["conv_w"], w["a_log"],
                                  w["dt_bias"], w["dn_norm"], sd0, c0)
        o_b, sr = _ret_step(proj.reshape(n, 1, PROJ_W), w["inv_freq"], w["gn_w"], w["gn_b"], sr0)
        o_a = o_a.reshape(nb, l, DN_VW)
        o_b = o_b.reshape(nb, l, RET_VW)
    x1, hffn, hf32, logits_t = _post_mixer(o_a, o_b, proj, x, mod, per_token, w["w_down_a"], w["w_down_b"],
                                           w["w_out"], w["norm_ffn"], w["router_w"], tm_post)
    y = _moe(hffn, hf32, logits_t, x1, mod, per_token, w, tr, min(l, 512), min(l, 128))
    return y, cb, sd, sr


def kernel(x_prompt, x_sample, state_conv, state_delta, state_ret, c_prompt, c_sample, w_mod, b_mod, norm_mix_w, w_in, conv_w, a_log, dt_bias, dn_norm_w, ret_gn_w, ret_gn_b, w_down_a, w_down_b, w_out, norm_ffn_w, router_w, router_bias, w_gate, w_up, w_down, ws_gate, ws_up, ws_down, final_norm_w):
    bp, lp, _ = x_prompt.shape
    bs = x_sample.shape[0]
    half = RET_DK // 2
    w_in0 = w_in[0]
    c0, c1 = DN_CONV_W, DN_CONV_W + 2 * DN_HEADS
    w_ba = jnp.pad(w_in0[:, c0:c1], ((0, 0), (0, BA_W - 2 * DN_HEADS))).astype(BF16)
    w = {
        "norm_mix": norm_mix_w[0].reshape(1, D_MODEL),
        "w_main": jnp.concatenate([w_in0[:, :c0], w_in0[:, c1:]], axis=1).astype(BF16),
        "w_ba": w_ba,
        "w_bat": w_in0[:, c0:c1].T.astype(BF16),
        "conv_w": conv_w[0], "a_log": a_log[0], "dt_bias": dt_bias[0], "dn_norm": dn_norm_w[0],
        "inv_freq": (ROPE_BASE ** (-jnp.arange(half, dtype=F32) / half)).reshape(1, half),
        "gn_w": ret_gn_w[0], "gn_b": ret_gn_b[0],
        "w_down_a": w_down_a[0].astype(BF16), "w_down_b": w_down_b[0].astype(BF16), "w_out": w_out[0].astype(BF16),
        "norm_ffn": norm_ffn_w[0].reshape(1, D_MODEL), "router_w": router_w[0], "router_bias": router_bias[0],
        "w_gate": w_gate[0].astype(BF16), "w_up": w_up[0].astype(BF16), "w_down": w_down[0].astype(BF16),
        "ws_gate": ws_gate[0].astype(BF16), "ws_up": ws_up[0].astype(BF16), "ws_down": ws_down[0].astype(BF16),
        "final_norm": final_norm_w.reshape(1, D_MODEL),
    }
    mod = _modulation(jnp.concatenate([c_prompt, c_sample], axis=0), w_mod[0], b_mod[0])
    mod_p = mod[:bp].reshape(bp, 1, MOD_CHUNKS * D_MODEL)
    mod_s = mod[bp:].reshape(1, bs, MOD_CHUNKS * D_MODEL)

    y_p, conv_p, delta_p, ret_p = _trunk(x_prompt, mod_p, False, None, w)
    y_s, conv_s, delta_s, ret_s = _trunk(x_sample.reshape(1, bs, D_MODEL), mod_s, True,
                                         (state_conv[0], state_delta[0], state_ret[0]), w)
    return (y_p, y_s.reshape(bs, 1, D_MODEL), conv_p[None], delta_p[None], ret_p[None],
            conv_s[None], delta_s[None], ret_s[None])
```

```python
import functools
import math

import jax
import jax.numpy as jnp
from jax import lax
from jax.experimental import pallas as pl
from jax.experimental.pallas import tpu as pltpu

F32 = jnp.float32
BF16 = jnp.bfloat16

D_MODEL = 1024
DN_HEADS, DN_DK, DN_DV, DN_CONV = 8, 128, 128, 4
DN_QK = DN_HEADS * DN_DK
DN_VW = DN_HEADS * DN_DV
DN_CONV_W = 2 * DN_QK + DN_VW
RET_HEADS, RET_DK, RET_DV = 4, 256, 512
RET_QK = RET_HEADS * RET_DK
RET_VW = RET_HEADS * RET_DV
ROPE_BASE = 10000.0
PAST_LEN = 16384
N_EXPERTS, TOP_K, N_GROUPS, TOPK_GROUPS = 64, 8, 8, 4
GROUP_SIZE = N_EXPERTS // N_GROUPS
D_EXPERT, D_SHARED = 256, 256
ROUTED_SCALE = 2.5
MOD_CHUNKS = 6
EPS = 1e-6

PROJ_W = 12 * D_MODEL
COL_Q_A, COL_K_A, COL_V_A, COL_Z_A = 0, 1, 2, 3
COL_Q_B, COL_K_B = 4, 5
COL_V_B, COL_G_B = 3, 4
COL_GATE_A, COL_GATE_B = 10, 11
BA_W = 128
CHUNK = 128
CONV_HALO = 8
VMEM_LIMIT = 56 * 1024 * 1024
LANES = 128
MOE_BLOCK = 512
TOKEN_TILE = (D_MODEL // LANES, LANES)

NN = (((1,), (0,)), ((), ()))
NT = (((1,), (1,)), ((), ()))
TN = (((0,), (0,)), ((), ()))


def _mm(a, b, dims=NN):
    return lax.dot_general(a.astype(BF16), b.astype(BF16), dims, preferred_element_type=F32)


def _mm_hi(a, b, dims=NN):
    return lax.dot_general(a.astype(F32), b.astype(F32), dims, precision=lax.Precision.HIGHEST,
                           preferred_element_type=F32)


def _hi_lo(x):
    hi = x.astype(BF16)
    return hi, (x - hi.astype(F32)).astype(BF16)


def _split_lhs(a):
    hi, lo = _hi_lo(a)
    return jnp.concatenate([hi, lo, hi], axis=1)


def _split_rhs(b):
    hi, lo = _hi_lo(b)
    return jnp.concatenate([hi, hi, lo], axis=0)


def _mm_split(a3, b3):
    return lax.dot_general(a3, b3, NN, preferred_element_type=F32)


def _sigmoid(x):
    return jax.nn.sigmoid(x)


def _silu(x):
    return x * _sigmoid(x)


def _softplus(x):
    return jnp.maximum(x, 0.0) + jnp.log1p(jnp.exp(-jnp.abs(x)))


def _rms(x, w):
    return x * lax.rsqrt(jnp.mean(x * x, axis=-1, keepdims=True) + EPS) * w


def _params(*sem):
    return pltpu.CompilerParams(dimension_semantics=sem, vmem_limit_bytes=VMEM_LIMIT)


def _mod_kernel(c_ref, w_ref, b_ref, o_ref):
    o_ref[...] = _mm_hi(_silu(c_ref[...]), w_ref[...]) + b_ref[...]


def _modulation(c, w_mod, b_mod):
    n = c.shape[0]
    tn = 1536
    return pl.pallas_call(
        _mod_kernel,
        out_shape=jax.ShapeDtypeStruct((n, MOD_CHUNKS * D_MODEL), F32),
        grid=(MOD_CHUNKS * D_MODEL // tn,),
        in_specs=[pl.BlockSpec((n, D_MODEL), lambda j: (0, 0)),
                  pl.BlockSpec((D_MODEL, tn), lambda j: (0, j)),
                  pl.BlockSpec((1, tn), lambda j: (0, j))],
        out_specs=pl.BlockSpec((n, tn), lambda j: (0, j)),
        compiler_params=_params("parallel"),
        name="modulation",
    )(c, w_mod, b_mod.reshape(1, -1))


def _mod_spec(per_token, tm, col):
    if per_token:
        return pl.BlockSpec((None, tm, D_MODEL), lambda b, i, *_: (b, i, col))
    return pl.BlockSpec((None, 1, D_MODEL), lambda b, i, *_: (b, 0, col))


def _inproj_kernel(x_ref, sh_ref, sc_ref, nw_ref, w_ref, wba_ref, wbat_ref, out_ref, ba_ref, bat_ref, h_scr):
    @pl.when(pl.program_id(2) == 0)
    def _():
        h = _rms(x_ref[...], nw_ref[...]) * (1.0 + sc_ref[...]) + sh_ref[...]
        hb = h.astype(BF16)
        h_scr[...] = hb
        ba_ref[...] = _mm(hb, wba_ref[...])
        bat_ref[...] = _mm(wbat_ref[...], hb, NT)

    out_ref[...] = _mm(h_scr[...], w_ref[...]).astype(BF16)


def _in_projection(x, mod, per_token, norm_w, w_main, w_ba, w_bat, tm, tn):
    nb, l, _ = x.shape
    return pl.pallas_call(
        _inproj_kernel,
        out_shape=(jax.ShapeDtypeStruct((nb, l, PROJ_W), BF16),
                   jax.ShapeDtypeStruct((nb, l, BA_W), F32),
                   jax.ShapeDtypeStruct((nb, 2 * DN_HEADS, l), F32)),
        grid=(nb, l // tm, PROJ_W // tn),
        in_specs=[pl.BlockSpec((None, tm, D_MODEL), lambda b, i, j: (b, i, 0)),
                  _mod_spec(per_token, tm, 0),
                  _mod_spec(per_token, tm, 1),
                  pl.BlockSpec((1, D_MODEL), lambda b, i, j: (0, 0)),
                  pl.BlockSpec((D_MODEL, tn), lambda b, i, j: (0, j)),
                  pl.BlockSpec((D_MODEL, BA_W), lambda b, i, j: (0, 0)),
                  pl.BlockSpec((2 * DN_HEADS, D_MODEL), lambda b, i, j: (0, 0))],
        out_specs=(pl.BlockSpec((None, tm, tn), lambda b, i, j: (b, i, j)),
                   pl.BlockSpec((None, tm, BA_W), lambda b, i, j: (b, i, 0)),
                   pl.BlockSpec((None, 2 * DN_HEADS, tm), lambda b, i, j: (b, 0, i))),
        scratch_shapes=[pltpu.VMEM((tm, D_MODEL), BF16)],
        compiler_params=_params("parallel", "parallel", "arbitrary"),
        name="in_projection",
    )(x, mod, mod, norm_w, w_main, w_ba, w_bat)


def _delta_prefill_kernel(q_ref, k_ref, v_ref, z_ref, bat_ref, cw_ref, alog_c_ref, dtb_c_ref, nw_ref, s0_ref, c0_ref,
                          o_ref, sout_ref, cout_ref,
                          ext_scr, s_scr, qkv_scr, pow_scr, inv_scr, uw_scr, phi_scr, plo_scr):
    step = pl.program_id(1)
    c = CHUNK
    lo = CONV_HALO - (DN_CONV - 1)

    @pl.when(step == 0)
    def _():
        s_scr[...] = s0_ref[...]
        ext_scr[lo:CONV_HALO, :] = c0_ref[...]

    ext_scr[CONV_HALO:, 0:DN_QK] = q_ref[...].astype(F32)
    ext_scr[CONV_HALO:, DN_QK:2 * DN_QK] = k_ref[...].astype(F32)
    ext_scr[CONV_HALO:, 2 * DN_QK:] = v_ref[...].astype(F32)

    row = lax.broadcasted_iota(jnp.int32, (c, c), 0)
    col = lax.broadcasted_iota(jnp.int32, (c, c), 1)
    causal = row >= col
    strict = row > col
    eye = (row == col).astype(F32)
    utri = (row <= col).astype(F32)

    bat = bat_ref[...]
    g_t = -jnp.exp(alog_c_ref[...]) * _softplus(bat[DN_HEADS:, :] + dtb_c_ref[...])
    gc_t = _mm_hi(g_t, utri)
    gates = _mm_hi(jnp.concatenate([_sigmoid(bat[:DN_HEADS, :]), gc_t], axis=0), _eye(2 * DN_HEADS), TN)
    beta_tok, gc_tok = gates[:, :DN_HEADS], gates

    def conv(c0):
        acc = ext_scr[lo:lo + c, c0:c0 + DN_DK] * cw_ref[0:1, c0:c0 + DN_DK]
        for i in range(1, DN_CONV):
            acc = acc + ext_scr[lo + i:lo + i + c, c0:c0 + DN_DK] * cw_ref[i:i + 1, c0:c0 + DN_DK]
        return _silu(acc)

    def decay_of(h):
        gc = gc_tok[:, DN_HEADS + h:DN_HEADS + h + 1]
        return gc, jnp.where(causal, jnp.exp(jnp.where(causal, gc - gc_t[h:h + 1, :], 0.0)), 0.0)

    for h in range(DN_HEADS):
        q = conv(h * DN_DK)
        k = conv(DN_QK + h * DN_DK)
        q = q * lax.rsqrt(jnp.sum(q * q, axis=-1, keepdims=True) + EPS) * (DN_DK ** -0.5)
        k = k * lax.rsqrt(jnp.sum(k * k, axis=-1, keepdims=True) + EPS)
        qkv_scr[:, h * DN_DK:(h + 1) * DN_DK] = q
        qkv_scr[:, DN_QK + h * DN_DK:DN_QK + (h + 1) * DN_DK] = k
        qkv_scr[:, 2 * DN_QK + h * DN_DV:2 * DN_QK + (h + 1) * DN_DV] = conv(2 * DN_QK + h * DN_DV)
        _, decay = decay_of(h)
        a_mat = jnp.where(strict, _mm(k * beta_tok[:, h:h + 1], k, NT) * decay, 0.0)
        phi_scr[h], plo_scr[h] = _hi_lo(a_mat)
        inv_scr[h] = eye - a_mat

    for _ in range(int(math.log2(c)) - 1):
        for h in range(DN_HEADS):
            p_hi, p_lo = phi_scr[h], plo_scr[h]
            phi_scr[h], plo_scr[h] = _hi_lo(_mm_split(jnp.concatenate([p_hi, p_lo, p_hi], axis=1),
                                                      jnp.concatenate([p_hi, p_hi, p_lo], axis=0)))
        for h in range(DN_HEADS):
            inv = inv_scr[h]
            p_hi, p_lo = phi_scr[h], plo_scr[h]
            inv_scr[h] = inv + _mm_split(_split_lhs(inv), jnp.concatenate([p_hi, p_hi, p_lo], axis=0))

    def qkv_of(h):
        return (qkv_scr[:, h * DN_DK:(h + 1) * DN_DK], qkv_scr[:, DN_QK + h * DN_DK:DN_QK + (h + 1) * DN_DK],
                qkv_scr[:, 2 * DN_QK + h * DN_DV:2 * DN_QK + (h + 1) * DN_DV])

    for h in range(DN_HEADS):
        q, k, v = qkv_of(h)
        beta = beta_tok[:, h:h + 1]
        gc, decay = decay_of(h)
        rhs = jnp.concatenate([v * beta, k * beta * jnp.exp(gc)], axis=-1)
        uw_scr[h] = _mm_split(_split_lhs(inv_scr[h]), _split_rhs(rhs))
        pow_scr[h] = jnp.where(causal, _mm(q, k, NT) * decay, 0.0)
    for h in range(DN_HEADS):
        inv_scr[h] = uw_scr[h, :, 0:DN_DV] - _mm(uw_scr[h, :, DN_DV:], s_scr[h])
    for h in range(DN_HEADS):
        q, k, _ = qkv_of(h)
        gc = gc_tok[:, DN_HEADS + h:DN_HEADS + h + 1]
        g_last = gc[c - 1:c, :]
        s = s_scr[h]
        v_new = inv_scr[h]
        o = _mm(q * jnp.exp(gc), s) + _mm(pow_scr[h], v_new)
        s_scr[h] = s * jnp.exp(g_last) + _mm(k * jnp.exp(g_last - gc), v_new, TN)
        z = z_ref[:, h * DN_DV:(h + 1) * DN_DV].astype(F32)
        o_ref[:, h * DN_DV:(h + 1) * DN_DV] = (_rms(o, nw_ref[...]) * _silu(z)).astype(BF16)

    ext_scr[0:CONV_HALO, :] = ext_scr[c:c + CONV_HALO, :]

    @pl.when(step == pl.num_programs(1) - 1)
    def _():
        sout_ref[...] = s_scr[...]
        cout_ref[...] = ext_scr[lo:CONV_HALO, :]


def _lane_pad(v, offset):
    return jnp.zeros((1, BA_W), F32).at[0, offset:offset + v.shape[0]].set(v.astype(F32))


def _delta_prefill(proj, bat, conv_w, a_log, dt_bias, dn_norm_w, s0, c0):
    nb, l, _ = proj.shape
    c = CHUNK
    col_spec = lambda j: pl.BlockSpec((None, c, D_MODEL), lambda b, i: (b, i, j))
    full = lambda shape: pl.BlockSpec(shape, lambda b, i: (0,) * len(shape))
    return pl.pallas_call(
        _delta_prefill_kernel,
        out_shape=(jax.ShapeDtypeStruct((nb, l, DN_VW), BF16),
                   jax.ShapeDtypeStruct((nb, DN_HEADS, DN_DK, DN_DV), F32),
                   jax.ShapeDtypeStruct((nb, DN_CONV - 1, DN_CONV_W), F32)),
        grid=(nb, l // c),
        in_specs=[col_spec(COL_Q_A), col_spec(COL_K_A), col_spec(COL_V_A), col_spec(COL_Z_A),
                  pl.BlockSpec((None, 2 * DN_HEADS, c), lambda b, i: (b, 0, i)),
                  full((DN_CONV, DN_CONV_W)), full((DN_HEADS, 1)), full((DN_HEADS, 1)), full((1, DN_DV)),
                  pl.BlockSpec((None, DN_HEADS, DN_DK, DN_DV), lambda b, i: (b, 0, 0, 0)),
                  pl.BlockSpec((None, DN_CONV - 1, DN_CONV_W), lambda b, i: (b, 0, 0))],
        out_specs=(pl.BlockSpec((None, c, DN_VW), lambda b, i: (b, i, 0)),
                   pl.BlockSpec((None, DN_HEADS, DN_DK, DN_DV), lambda b, i: (b, 0, 0, 0)),
                   pl.BlockSpec((None, DN_CONV - 1, DN_CONV_W), lambda b, i: (b, 0, 0))),
        scratch_shapes=[pltpu.VMEM((c + CONV_HALO, DN_CONV_W), F32),
                        pltpu.VMEM((DN_HEADS, DN_DK, DN_DV), F32),
                        pltpu.VMEM((c, DN_CONV_W), F32),
                        pltpu.VMEM((DN_HEADS, c, c), F32),
                        pltpu.VMEM((DN_HEADS, c, c), F32),
                        pltpu.VMEM((DN_HEADS, c, DN_DV + DN_DK), F32),
                        pltpu.VMEM((DN_HEADS, c, c), BF16),
                        pltpu.VMEM((DN_HEADS, c, c), BF16)],
        compiler_params=_params("parallel", "arbitrary"),
        name="delta_prefill",
    )(proj, proj, proj, proj, bat, conv_w, a_log.reshape(DN_HEADS, 1), dt_bias.reshape(DN_HEADS, 1),
      dn_norm_w.reshape(1, DN_DV), s0, c0)


def _log_gamma(h):
    return math.log(1.0 - 2.0 ** (-5.0 - h))


def _rotate(x, cos, sin):
    half = x.shape[-1] // 2
    x1, x2 = x[:, :half], x[:, half:]
    return jnp.concatenate([x1 * cos - x2 * sin, x1 * sin + x2 * cos], axis=-1)


def _group_norm_gate(o, gw, gb, gate):
    mu = jnp.mean(o, axis=-1, keepdims=True)
    var = jnp.mean(jnp.square(o - mu), axis=-1, keepdims=True)
    return ((o - mu) * lax.rsqrt(var + EPS) * gw + gb) * _silu(gate)


def _ret_prefill_kernel(q_ref, k_ref, v_ref, g_ref, inv_ref, gw_ref, gb_ref, s0_ref, o_ref, sout_ref, s_scr):
    step = pl.program_id(1)
    c = CHUNK

    @pl.when(step == 0)
    def _():
        s_scr[...] = s0_ref[...]

    idx_c = lax.broadcasted_iota(jnp.int32, (c, 1), 0)
    pos = (step * c + idx_c).astype(F32)
    ang = pos * inv_ref[...]
    cos, sin = jnp.cos(ang), jnp.sin(ang)
    rel = (lax.broadcasted_iota(jnp.int32, (c, c), 0) - lax.broadcasted_iota(jnp.int32, (c, c), 1)).astype(F32)
    idx = idx_c.astype(F32)

    for h in range(RET_HEADS):
        lg = _log_gamma(h)
        q = _rotate(q_ref[:, h * RET_DK:(h + 1) * RET_DK].astype(F32), cos, sin)
        k = _rotate(k_ref[:, h * RET_DK:(h + 1) * RET_DK].astype(F32), cos, sin) * (RET_DK ** -0.5)
        v = v_ref[:, h * RET_DV:(h + 1) * RET_DV]
        d_mat = jnp.where(rel >= 0, jnp.exp(jnp.maximum(rel, 0.0) * lg), 0.0)
        intra = _mm(_mm(q, k, NT) * d_mat, v)
        s = s_scr[h]
        cross = _mm(q * jnp.exp((idx + 1.0) * lg), s)
        s_scr[h] = s * math.exp(c * lg) + _mm(k * jnp.exp((c - 1.0 - idx) * lg), v, TN)
        sl = slice(h * RET_DV, (h + 1) * RET_DV)
        o_ref[:, sl] = _group_norm_gate(intra + cross, gw_ref[:, sl], gb_ref[:, sl],
                                        g_ref[:, sl].astype(F32)).astype(BF16)

    @pl.when(step == pl.num_programs(1) - 1)
    def _():
        sout_ref[...] = s_scr[...]


def _ret_prefill(proj, inv_freq, gn_w, gn_b, s0):
    nb, l, _ = proj.shape
    c = CHUNK
    full = lambda shape: pl.BlockSpec(shape, lambda b, i: (0,) * len(shape))
    state_spec = pl.BlockSpec((None, RET_HEADS, RET_DK, RET_DV), lambda b, i: (b, 0, 0, 0))
    return pl.pallas_call(
        _ret_prefill_kernel,
        out_shape=(jax.ShapeDtypeStruct((nb, l, RET_VW), BF16),
                   jax.ShapeDtypeStruct((nb, RET_HEADS, RET_DK, RET_DV), F32)),
        grid=(nb, l // c),
        in_specs=[pl.BlockSpec((None, c, RET_QK), lambda b, i: (b, i, COL_Q_B)),
                  pl.BlockSpec((None, c, RET_QK), lambda b, i: (b, i, COL_K_B)),
                  pl.BlockSpec((None, c, RET_VW), lambda b, i: (b, i, COL_V_B)),
                  pl.BlockSpec((None, c, RET_VW), lambda b, i: (b, i, COL_G_B)),
                  full((1, RET_DK // 2)), full((1, RET_VW)), full((1, RET_VW)), state_spec],
        out_specs=(pl.BlockSpec((None, c, RET_VW), lambda b, i: (b, i, 0)), state_spec),
        scratch_shapes=[pltpu.VMEM((RET_HEADS, RET_DK, RET_DV), F32)],
        compiler_params=_params("parallel", "arbitrary"),
        name="retention_prefill",
    )(proj, proj, proj, proj, inv_freq, gn_w.reshape(1, RET_VW), gn_b.reshape(1, RET_VW), s0)


ROWS = 8
SEQ_PER_STEP = 4


def _row_mm(a, s):
    return _mm(jnp.broadcast_to(a, (ROWS, a.shape[-1])), s)[0:1, :]


def _outer(a, b):
    first = lax.broadcasted_iota(jnp.int32, (ROWS, a.shape[-1]), 0) == 0
    a8 = jnp.where(first, jnp.broadcast_to(a, (ROWS, a.shape[-1])), 0.0)
    return _mm(a8, jnp.broadcast_to(b, (ROWS, b.shape[-1])), TN)


def _delta_step_kernel(qkv_ref, z_ref, ba_ref, cw_ref, alog_ref, dtb_ref, nw_ref, s_ref, c_ref,
                       o_ref, sout_ref, cout_ref):
    for j in range(qkv_ref.shape[0]):
        u_new = qkv_ref[j].astype(F32)
        buf = c_ref[j]
        acc = u_new * cw_ref[DN_CONV - 1:DN_CONV, :]
        for i in range(DN_CONV - 1):
            acc = acc + buf[i:i + 1, :] * cw_ref[i:i + 1, :]
        qkv = _silu(acc)
        cout_ref[j, 0:DN_CONV - 2, :] = buf[1:, :]
        cout_ref[j, DN_CONV - 2:, :] = u_new
        ba = ba_ref[j]
        beta_all = _sigmoid(ba)
        g_all = -jnp.exp(alog_ref[...]) * _softplus(ba + dtb_ref[...])
        for h in range(DN_HEADS):
            q = qkv[:, h * DN_DK:(h + 1) * DN_DK]
            k = qkv[:, DN_QK + h * DN_DK:DN_QK + (h + 1) * DN_DK]
            v = qkv[:, 2 * DN_QK + h * DN_DV:2 * DN_QK + (h + 1) * DN_DV]
            q = q * lax.rsqrt(jnp.sum(q * q, axis=-1, keepdims=True) + EPS) * (DN_DK ** -0.5)
            k = k * lax.rsqrt(jnp.sum(k * k, axis=-1, keepdims=True) + EPS)
            beta = beta_all[:, h:h + 1]
            eg = jnp.exp(g_all[:, DN_HEADS + h:DN_HEADS + h + 1])
            s = s_ref[j, h]
            kb = k * beta
            v_new = v * beta - _row_mm(kb * eg, s)
            o = _row_mm(q * eg, s) + jnp.sum(q * k, axis=-1, keepdims=True) * v_new
            sout_ref[j, h] = s * eg + _outer(k, v_new)
            z = z_ref[j, :, h * DN_DV:(h + 1) * DN_DV].astype(F32)
            o_ref[j, :, h * DN_DV:(h + 1) * DN_DV] = (_rms(o, nw_ref[...]) * _silu(z)).astype(BF16)


def _delta_step(proj, ba, conv_w, a_log, dt_bias, dn_norm_w, s0, c0):
    n = proj.shape[0]
    ns = math.gcd(n, SEQ_PER_STEP)
    full = lambda shape: pl.BlockSpec(shape, lambda b: (0,) * len(shape))
    state_spec = pl.BlockSpec((ns, DN_HEADS, DN_DK, DN_DV), lambda b: (b, 0, 0, 0))
    conv_spec = pl.BlockSpec((ns, DN_CONV - 1, DN_CONV_W), lambda b: (b, 0, 0))
    return pl.pallas_call(
        _delta_step_kernel,
        out_shape=(jax.ShapeDtypeStruct((n, 1, DN_VW), BF16),
                   jax.ShapeDtypeStruct((n, DN_HEADS, DN_DK, DN_DV), F32),
                   jax.ShapeDtypeStruct((n, DN_CONV - 1, DN_CONV_W), F32)),
        grid=(n // ns,),
        in_specs=[pl.BlockSpec((ns, 1, DN_CONV_W), lambda b: (b, 0, 0)),
                  pl.BlockSpec((ns, 1, DN_VW), lambda b: (b, 0, COL_Z_A)),
                  pl.BlockSpec((ns, 1, BA_W), lambda b: (b, 0, 0)),
                  full((DN_CONV, DN_CONV_W)), full((1, BA_W)), full((1, BA_W)), full((1, DN_DV)),
                  state_spec, conv_spec],
        out_specs=(pl.BlockSpec((ns, 1, DN_VW), lambda b: (b, 0, 0)), state_spec, conv_spec),
        compiler_params=_params("parallel"),
        name="delta_step",
    )(proj, proj, ba, conv_w, _lane_pad(a_log, DN_HEADS), _lane_pad(dt_bias, DN_HEADS),
      dn_norm_w.reshape(1, DN_DV), s0, c0)


def _ret_step_kernel(q_ref, k_ref, v_ref, g_ref, inv_ref, gw_ref, gb_ref, s_ref, o_ref, sout_ref):
    ang = float(PAST_LEN) * inv_ref[...]
    cos, sin = jnp.cos(ang), jnp.sin(ang)
    for h in range(RET_HEADS):
        gamma = math.exp(_log_gamma(h))
        q = _rotate(q_ref[:, h * RET_DK:(h + 1) * RET_DK].astype(F32), cos, sin)
        k = _rotate(k_ref[:, h * RET_DK:(h + 1) * RET_DK].astype(F32), cos, sin) * (RET_DK ** -0.5)
        v = v_ref[:, h * RET_DV:(h + 1) * RET_DV].astype(F32)
        s = s_ref[h]
        o = jnp.sum(q * k, axis=-1, keepdims=True) * v + _row_mm(q * gamma, s)
        sout_ref[h] = s * gamma + _outer(k, v)
        sl = slice(h * RET_DV, (h + 1) * RET_DV)
        o_ref[:, sl] = _group_norm_gate(o, gw_ref[:, sl], gb_ref[:, sl], g_ref[:, sl].astype(F32)).astype(BF16)


def _ret_step(proj, inv_freq, gn_w, gn_b, s0):
    n = proj.shape[0]
    full = lambda shape: pl.BlockSpec(shape, lambda b: (0,) * len(shape))
    state_spec = pl.BlockSpec((None, RET_HEADS, RET_DK, RET_DV), lambda b: (b, 0, 0, 0))
    return pl.pallas_call(
        _ret_step_kernel,
        out_shape=(jax.ShapeDtypeStruct((n, 1, RET_VW), BF16),
                   jax.ShapeDtypeStruct((n, RET_HEADS, RET_DK, RET_DV), F32)),
        grid=(n,),
        in_specs=[pl.BlockSpec((None, 1, RET_QK), lambda b: (b, 0, COL_Q_B)),
                  pl.BlockSpec((None, 1, RET_QK), lambda b: (b, 0, COL_K_B)),
                  pl.BlockSpec((None, 1, RET_VW), lambda b: (b, 0, COL_V_B)),
                  pl.BlockSpec((None, 1, RET_VW), lambda b: (b, 0, COL_G_B)),
                  full((1, RET_DK // 2)), full((1, RET_VW)), full((1, RET_VW)), state_spec],
        out_specs=(pl.BlockSpec((None, 1, RET_VW), lambda b: (b, 0, 0)), state_spec),
        compiler_params=_params("parallel"),
        name="retention_step",
    )(proj, proj, proj, proj, inv_freq, gn_w.reshape(1, RET_VW), gn_b.reshape(1, RET_VW), s0)


def _post_mixer_kernel(oa_ref, ob_ref, ga_ref, gb_ref, x_ref, g1_ref, sh2_ref, sc2_ref, wda_ref, wdb_ref,
                       wout_ref, nw_ref, rw_ref, x1_ref, h_ref, hf_ref, logit_ref):
    y_a = _mm(oa_ref[...], wda_ref[...])
    y_b = _mm(ob_ref[...], wdb_ref[...])
    merged = _sigmoid(ga_ref[...].astype(F32)) * y_a + _sigmoid(gb_ref[...].astype(F32)) * y_b
    x1 = x_ref[...] + g1_ref[...] * _mm(merged, wout_ref[...])
    x1_ref[...] = x1
    hf = _rms(x1, nw_ref[...]) * (1.0 + sc2_ref[...]) + sh2_ref[...]
    h_ref[...] = hf.astype(BF16)
    hf_ref[...] = hf.reshape(hf_ref.shape)
    logit_ref[...] = _mm_hi(rw_ref[...], hf, NT)


def _post_mixer(o_a, o_b, proj, x, mod, per_token, w_down_a, w_down_b, w_out, norm_w, router_w, tm):
    nb, l, _ = x.shape
    tok = lambda w, j=0: pl.BlockSpec((None, tm, w), lambda b, i: (b, i, j))
    full = lambda shape: pl.BlockSpec(shape, lambda b, i: (0,) * len(shape))
    return pl.pallas_call(
        _post_mixer_kernel,
        out_shape=(jax.ShapeDtypeStruct((nb, l, D_MODEL), F32),
                   jax.ShapeDtypeStruct((nb, l, D_MODEL), BF16),
                   jax.ShapeDtypeStruct((nb, l) + TOKEN_TILE, F32),
                   jax.ShapeDtypeStruct((nb, N_EXPERTS, l), F32)),
        grid=(nb, l // tm),
        in_specs=[tok(DN_VW), tok(RET_VW), tok(D_MODEL, COL_GATE_A), tok(D_MODEL, COL_GATE_B), tok(D_MODEL),
                  _mod_spec(per_token, tm, 2), _mod_spec(per_token, tm, 3), _mod_spec(per_token, tm, 4),
                  full((DN_VW, D_MODEL)), full((RET_VW, D_MODEL)), full((D_MODEL, D_MODEL)),
                  full((1, D_MODEL)), full((N_EXPERTS, D_MODEL))],
        out_specs=(tok(D_MODEL), tok(D_MODEL),
                   pl.BlockSpec((None, tm) + TOKEN_TILE, lambda b, i: (b, i, 0, 0)),
                   pl.BlockSpec((None, N_EXPERTS, tm), lambda b, i: (b, 0, i))),
        compiler_params=_params("parallel", "parallel"),
        name="post_mixer",
    )(o_a, o_b, proj, proj, x, mod, mod, mod, w_down_a, w_down_b, w_out, norm_w, router_w)


def _first_max(x, axis, n):
    m = jnp.max(x, axis=axis, keepdims=True)
    ids = lax.broadcasted_iota(jnp.int32, x.shape, axis)
    first = jnp.min(jnp.where(x == m, ids, n), axis=axis, keepdims=True)
    return m, ids == first


def _eye(n):
    return (lax.broadcasted_iota(jnp.int32, (n, n), 0) == lax.broadcasted_iota(jnp.int32, (n, n), 1)).astype(F32)


class _Streams:
    def __init__(self, nb, l, ls, tm):
        self.tm, self.tps = tm, l // tm
        self.n_p, self.n_s = nb * (l // tm), ls // tm
        self.n = self.n_p + self.n_s

    def prompt(self, s):
        sp = jnp.minimum(s, self.n_p - 1)
        return sp // self.tps, sp % self.tps

    def sample(self, s, n_tiles=None):
        return jnp.clip(s - self.n_p, 0, (self.n_s if n_tiles is None else n_tiles) - 1)


def _route_kernel(n_prompt_tiles, lp_ref, ls_ref, bias_ref, ek_ref, rk_ref, wkt_ref, cnt_ref, run_scr):
    @pl.when(pl.program_id(0) == 0)
    def _():
        run_scr[...] = jnp.zeros_like(run_scr)

    t = lp_ref.shape[-1]
    scores = _sigmoid(jnp.where(pl.program_id(0) < n_prompt_tiles, lp_ref[...], ls_ref[...]))
    biased = scores + bias_ref[...]
    neg = jnp.float32(-jnp.inf)
    b3 = biased.reshape(N_GROUPS, GROUP_SIZE, t)
    m1, hit = _first_max(b3, 1, GROUP_SIZE)
    m2 = jnp.max(jnp.where(hit, neg, b3), axis=1, keepdims=True)
    gscore = (m1 + m2).reshape(N_GROUPS, t)
    gmask = jnp.zeros((N_GROUPS, t), jnp.bool_)
    for _ in range(TOPK_GROUPS):
        _, hit = _first_max(gscore, 0, N_GROUPS)
        gmask = jnp.logical_or(gmask, hit)
        gscore = jnp.where(hit, neg, gscore)
    emask = jnp.broadcast_to(gmask.reshape(N_GROUPS, 1, t), (N_GROUPS, GROUP_SIZE, t)).reshape(N_EXPERTS, t)
    masked = jnp.where(emask, biased, neg)
    eid = lax.broadcasted_iota(jnp.int32, (N_EXPERTS, t), 0).astype(F32)
    hits, e_rows, w_rows = [], [], []
    for _ in range(TOP_K):
        _, hit = _first_max(masked, 0, N_EXPERTS)
        hits.append(hit)
        e_rows.append(jnp.sum(jnp.where(hit, eid, 0.0), axis=0, keepdims=True))
        w_rows.append(jnp.sum(jnp.where(hit, scores, 0.0), axis=0, keepdims=True))
        masked = jnp.where(hit, neg, masked)
    sel = sum(hit.astype(F32) for hit in hits)
    before = lax.broadcasted_iota(jnp.int32, (t, t), 0) < lax.broadcasted_iota(jnp.int32, (t, t), 1)
    rank = run_scr[:, 0:1] + _mm(sel, before.astype(F32))
    rk_rows = [jnp.sum(jnp.where(hit, rank, 0.0), axis=0, keepdims=True) for hit in hits]
    run_scr[...] = run_scr[...] + jnp.sum(sel, axis=1, keepdims=True)
    wk = jnp.concatenate(w_rows, axis=0)
    wk = wk / jnp.sum(wk, axis=0, keepdims=True) * ROUTED_SCALE
    ek_ref[...] = jnp.concatenate(e_rows, axis=0).astype(jnp.int32)
    rk_ref[...] = jnp.concatenate(rk_rows, axis=0).astype(jnp.int32)
    wkt_ref[...] = _mm_hi(wk, _eye(TOP_K), TN)
    cnt_ref[...] = run_scr[...]


def _route(logits_p, logits_s, router_bias, st):
    t_all = st.n * st.tm
    pick = pl.BlockSpec((None, TOP_K, st.tm), lambda s: (0, 0, s))
    return pl.pallas_call(
        functools.partial(_route_kernel, st.n_p),
        out_shape=(jax.ShapeDtypeStruct((1, TOP_K, t_all), jnp.int32),
                   jax.ShapeDtypeStruct((1, TOP_K, t_all), jnp.int32),
                   jax.ShapeDtypeStruct((1, t_all, TOP_K), F32),
                   jax.ShapeDtypeStruct((N_EXPERTS, LANES), F32)),
        grid=(st.n,),
        in_specs=[pl.BlockSpec((None, N_EXPERTS, st.tm), lambda s: (st.prompt(s)[0], 0, st.prompt(s)[1])),
                  pl.BlockSpec((None, N_EXPERTS, st.tm), lambda s: (0, 0, st.sample(s))),
                  pl.BlockSpec((N_EXPERTS, 1), lambda s: (0, 0))],
        out_specs=(pick, pick, pl.BlockSpec((None, st.tm, TOP_K), lambda s: (0, s, 0)),
                   pl.BlockSpec((N_EXPERTS, LANES), lambda s: (0, 0))),
        scratch_shapes=[pltpu.VMEM((N_EXPERTS, LANES), F32)],
        compiler_params=_params("arbitrary"),
        name="route",
    )(logits_p, logits_s, router_bias.reshape(N_EXPERTS, 1))


def _plan_kernel(block, cnt_ref, ek_ref, rk_ref, pos_ref, be_ref, pe_ref, nu_ref, start_scr):
    @pl.when(jnp.logical_and(pl.program_id(0) == 0, pl.program_id(1) == 0))
    def _():
        padded = jnp.ceil(cnt_ref[...] * (1.0 / block)) * block
        r = lax.broadcasted_iota(jnp.int32, (N_EXPERTS, N_EXPERTS), 0)
        c = lax.broadcasted_iota(jnp.int32, (N_EXPERTS, N_EXPERTS), 1)
        pad_end = _mm_hi((r >= c).astype(F32), padded)
        start_scr[...] = pad_end - padded
        pe_ref[...] = pad_end.astype(jnp.int32)
        nu_ref[...] = (pad_end[N_EXPERTS - 1:, :] * (1.0 / block)).astype(jnp.int32)
        nbp = be_ref.shape[-1]
        first_row = lax.broadcasted_iota(jnp.int32, (1, nbp), 1).astype(F32) * block
        owner = jnp.sum((pad_end[:, 0:1] <= first_row).astype(F32), axis=0, keepdims=True)
        be_ref[...] = jnp.minimum(owner, N_EXPERTS - 1.0).astype(jnp.int32)

    t = ek_ref.shape[-1]
    eid = lax.broadcasted_iota(jnp.int32, (N_EXPERTS, t), 0)
    start = start_scr[:, 0:1]
    rows = [jnp.sum(jnp.where(eid == ek_ref[k:k + 1, :], start, 0.0), axis=0, keepdims=True)
            for k in range(TOP_K)]
    pos_ref[...] = jnp.concatenate(rows, axis=0).astype(jnp.int32) + rk_ref[...]


def _plan(counts, ek, rk, tr, n_blocks, block):
    nb, _, l = ek.shape
    nbp = -(-n_blocks // LANES) * LANES
    pick = pl.BlockSpec((None, TOP_K, tr), lambda b, i: (b, 0, i))
    const = lambda shape: pl.BlockSpec(shape, lambda b, i: (0, 0))
    return pl.pallas_call(
        functools.partial(_plan_kernel, block),
        out_shape=(jax.ShapeDtypeStruct((nb, TOP_K, l), jnp.int32),
                   jax.ShapeDtypeStruct((1, nbp), jnp.int32),
                   jax.ShapeDtypeStruct((N_EXPERTS, LANES), jnp.int32),
                   jax.ShapeDtypeStruct((1, LANES), jnp.int32)),
        grid=(nb, l // tr),
        in_specs=[const((N_EXPERTS, LANES)), pick, pick],
        out_specs=(pick, const((1, nbp)), const((N_EXPERTS, LANES)), const((1, LANES))),
        scratch_shapes=[pltpu.VMEM((N_EXPERTS, LANES), F32)],
        compiler_params=_params("arbitrary", "arbitrary"),
        name="plan",
    )(counts, ek, rk)


def _dispatch_kernel(n_prompt_tiles, pe_ref, pos_ref, xp_ref, xq_ref, xs_hbm, zero_scr, sem):
    s = pl.program_id(0)
    tm = pos_ref.shape[-1]

    @pl.when(s == 0)
    def _():
        zero_scr[...] = jnp.zeros_like(zero_scr)

        def tail_copy(e):
            block = zero_scr.shape[0]
            return pltpu.make_async_copy(zero_scr, xs_hbm.at[pl.ds(pe_ref[e] - block, block)], sem)

        def nonempty(e):
            return pe_ref[e] > jnp.where(e > 0, pe_ref[jnp.maximum(e - 1, 0)], 0)

        @pl.loop(0, N_EXPERTS)
        def _(e):
            @pl.when(nonempty(e))
            def _():
                tail_copy(e).start()

        @pl.loop(0, N_EXPERTS)
        def _(e):
            @pl.when(nonempty(e))
            def _():
                tail_copy(e).wait()

    def scatter(x_ref):
        def row(k, t):
            return pltpu.make_async_copy(x_ref.at[t], xs_hbm.at[pos_ref[k, t]], sem)

        @pl.loop(0, tm)
        def _(t):
            for k in range(TOP_K):
                row(k, t).start(priority=k % 2)

        @pl.loop(0, tm)
        def _(t):
            for k in range(TOP_K):
                row(k, t).wait()

    @pl.when(s < n_prompt_tiles)
    def _():
        scatter(xp_ref)

    @pl.when(s >= n_prompt_tiles)
    def _():
        scatter(xq_ref)


def _dispatch(x_p, x_s, pos, pad_end, n_rows, st, block):
    tile = (st.tm,) + TOKEN_TILE
    flat = lambda x: x.reshape((-1,) + TOKEN_TILE)
    return pl.pallas_call(
        functools.partial(_dispatch_kernel, st.n_p),
        out_shape=jax.ShapeDtypeStruct((n_rows,) + TOKEN_TILE, F32),
        grid_spec=pltpu.PrefetchScalarGridSpec(
            num_scalar_prefetch=1,
            grid=(st.n,),
            in_specs=[pl.BlockSpec((None, TOP_K, st.tm), lambda s, pe: (0, 0, s), memory_space=pltpu.SMEM),
                      pl.BlockSpec(tile, lambda s, pe: (jnp.minimum(s, st.n_p - 1), 0, 0)),
                      pl.BlockSpec(tile, lambda s, pe: (st.sample(s), 0, 0))],
            out_specs=pl.BlockSpec(memory_space=pl.ANY),
            scratch_shapes=[pltpu.VMEM((block,) + TOKEN_TILE, F32), pltpu.SemaphoreType.DMA(())]),
        compiler_params=_params("arbitrary"),
        name="dispatch",
    )(pad_end, pos, flat(x_p), flat(x_s))


def _experts_kernel(be_ref, nu_ref, x_ref, wg_ref, wu_ref, wd_ref, y_ref):
    @pl.when(pl.program_id(0) < nu_ref[0])
    def _():
        xb = x_ref[...].reshape(x_ref.shape[0], D_MODEL).astype(BF16)
        y = _mm(_silu(_mm(xb, wg_ref[...])) * _mm(xb, wu_ref[...]), wd_ref[...])
        y_ref[...] = y.reshape(y_ref.shape)


def _experts(xs, block_expert, n_used, w_gate, w_up, w_down, block):
    n_rows, d = xs.shape[0], D_MODEL
    used = lambda i, nu: jnp.minimum(i, nu[0] - 1)
    rows = pl.BlockSpec((block,) + TOKEN_TILE, lambda i, be, nu: (used(i, nu), 0, 0))
    return pl.pallas_call(
        _experts_kernel,
        out_shape=jax.ShapeDtypeStruct((n_rows,) + TOKEN_TILE, F32),
        grid_spec=pltpu.PrefetchScalarGridSpec(
            num_scalar_prefetch=2,
            grid=(n_rows // block,),
            in_specs=[rows,
                      pl.BlockSpec((None, d, D_EXPERT), lambda i, be, nu: (be[used(i, nu)], 0, 0)),
                      pl.BlockSpec((None, d, D_EXPERT), lambda i, be, nu: (be[used(i, nu)], 0, 0)),
                      pl.BlockSpec((None, D_EXPERT, d), lambda i, be, nu: (be[used(i, nu)], 0, 0))],
            out_specs=rows),
        compiler_params=_params("arbitrary"),
        name="experts",
    )(block_expert, n_used, xs, w_gate, w_up, w_down)


def _combine_kernel(n_prompt_tiles, pos_ref, pos_next_ref, wkt_ref, ys_hbm, hp_ref, hq_ref, x1p_ref, x1q_ref,
                    g2p_ref, g2q_ref, wsg_ref, wsu_ref, wsd_ref, fw_ref, yp_ref, yq_ref, rows_scr, sem):
    s = pl.program_id(0)
    tm = pos_ref.shape[-1]
    slot = lax.rem(s, 2)
    in_prompt = s < n_prompt_tiles

    def row(p_ref, sl, k, t):
        return pltpu.make_async_copy(ys_hbm.at[p_ref[k, t]], rows_scr.at[sl, k, t], sem.at[sl])

    def fetch(p_ref, sl):
        @pl.loop(0, tm)
        def _(t):
            for k in range(TOP_K):
                row(p_ref, sl, k, t).start(priority=k % 2)

    @pl.when(s == 0)
    def _():
        fetch(pos_ref, 0)

    @pl.when(s + 1 < pl.num_programs(0))
    def _():
        fetch(pos_next_ref, 1 - slot)

    hb = jnp.where(in_prompt, hp_ref[...], hq_ref[...])
    acc = _mm(_silu(_mm(hb, wsg_ref[...])) * _mm(hb, wsu_ref[...]), wsd_ref[...])

    @pl.loop(0, tm)
    def _(t):
        for k in range(TOP_K):
            row(pos_ref, slot, k, t).wait()

    wkt = wkt_ref[...]
    for k in range(TOP_K):
        acc = acc + rows_scr[slot, k].reshape(tm, D_MODEL) * wkt[:, k:k + 1]

    @pl.when(in_prompt)
    def _():
        yp_ref[...] = _rms(x1p_ref[...] + g2p_ref[...] * acc, fw_ref[...])

    @pl.when(jnp.logical_not(in_prompt))
    def _():
        yq_ref[...] = _rms(x1q_ref[...] + g2q_ref[...] * acc, fw_ref[...])


def _combine(ys, pos, wkt, h_p, h_s, x1_p, x1_s, mod_p, mod_s, ws_gate, ws_up, ws_down, final_w, st):
    nb, l, d = x1_p.shape
    real = x1_s.shape[1] // st.tm
    prompt = lambda w, col=0: pl.BlockSpec((None, st.tm, w), lambda s: (*st.prompt(s), col))
    sample = lambda w, n, col=0: pl.BlockSpec((None, st.tm, w), lambda s: (0, st.sample(s, n), col))
    full = lambda shape: pl.BlockSpec(shape, lambda s: (0,) * len(shape))
    pick = lambda step: pl.BlockSpec((None, TOP_K, st.tm), lambda s: (0, 0, step(s)), memory_space=pltpu.SMEM)
    return pl.pallas_call(
        functools.partial(_combine_kernel, st.n_p),
        out_shape=(jax.ShapeDtypeStruct((nb, l, d), F32), jax.ShapeDtypeStruct((1, st.n_s * st.tm, d), F32)),
        grid=(st.n,),
        in_specs=[pick(lambda s: s), pick(lambda s: jnp.minimum(s + 1, st.n - 1)),
                  pl.BlockSpec((None, st.tm, TOP_K), lambda s: (0, s, 0)), pl.BlockSpec(memory_space=pl.ANY),
                  prompt(d), sample(d, real), prompt(d), sample(d, real),
                  pl.BlockSpec((None, 1, d), lambda s: (st.prompt(s)[0], 0, 5)), sample(d, real, 5),
                  full((d, D_SHARED)), full((d, D_SHARED)), full((D_SHARED, d)), full((1, d))],
        out_specs=(prompt(d), sample(d, None)),
        scratch_shapes=[pltpu.VMEM((2, TOP_K, st.tm) + TOKEN_TILE, F32), pltpu.SemaphoreType.DMA((2,))],
        compiler_params=_params("arbitrary"),
        name="combine",
    )(pos, pos, wkt, ys, h_p, h_s, x1_p, x1_s, mod_p, mod_s, ws_gate, ws_up, ws_down, final_w)


def _moe(pre_p, pre_s, mod_p, mod_s, w):
    x1_p, h_p, hf3_p, logits_p = pre_p
    x1_s, h_s, hf3_s, logits_s = pre_s
    nb, l, _ = x1_p.shape
    ls = x1_s.shape[1]
    tr = min(l, 512)
    ls_pad = -(-ls // tr) * tr
    hf3_s = jnp.pad(hf3_s, ((0, 0), (0, ls_pad - ls), (0, 0), (0, 0)))
    logits_s = jnp.pad(logits_s, ((0, 0), (0, 0), (0, ls_pad - ls)))
    st = _Streams(nb, l, ls_pad, tr)
    st_c = _Streams(nb, l, ls_pad, min(l, ls, 128))
    n_tokens = st.n * st.tm
    n_blocks = n_tokens * TOP_K // MOE_BLOCK + N_EXPERTS
    ek, rk, wkt, counts = _route(logits_p, logits_s, w["router_bias"], st)
    pos, block_expert, pad_end, n_used = _plan(counts, ek, rk, tr, n_blocks, MOE_BLOCK)
    xs = _dispatch(hf3_p, hf3_s, pos, pad_end[:, 0], n_blocks * MOE_BLOCK, st, MOE_BLOCK)
    ys = _experts(xs, block_expert[0, :n_blocks], n_used[0, :1], w["w_gate"], w["w_up"], w["w_down"], MOE_BLOCK)
    y_p, y_s = _combine(ys, pos, wkt, h_p, h_s, x1_p, x1_s, mod_p, mod_s, w["ws_gate"], w["ws_up"], w["ws_down"],
                        w["final_norm"], st_c)
    return y_p, y_s[:, :ls]


def _trunk(x, mod, per_token, states, w):
    nb, l, _ = x.shape
    tm_proj, tn_proj, tm_post = min(l, 1024), 2048, min(l, 512)
    proj, ba, bat = _in_projection(x, mod, per_token, w["norm_mix"], w["w_main"], w["w_ba"], w["w_bat"],
                                   tm_proj, tn_proj)
    if states is None:
        c0 = jnp.zeros((nb, DN_CONV - 1, DN_CONV_W), F32)
        sd0 = jnp.zeros((nb, DN_HEADS, DN_DK, DN_DV), F32)
        sr0 = jnp.zeros((nb, RET_HEADS, RET_DK, RET_DV), F32)
        o_a, sd, cb = _delta_prefill(proj, bat, w["conv_w"], w["a_log"], w["dt_bias"], w["dn_norm"], sd0, c0)
        o_b, sr = _ret_prefill(proj, w["inv_freq"], w["gn_w"], w["gn_b"], sr0)
    else:
        c0, sd0, sr0 = states
        n = nb * l
        o_a, sd, cb = _delta_step(proj.reshape(n, 1, PROJ_W), ba.reshape(n, 1, BA_W), w["conv_w"], w["a_log"],
                                  w["dt_bias"], w["dn_norm"], sd0, c0)
        o_b, sr = _ret_step(proj.reshape(n, 1, PROJ_W), w["inv_freq"], w["gn_w"], w["gn_b"], sr0)
        o_a = o_a.reshape(nb, l, DN_VW)
        o_b = o_b.reshape(nb, l, RET_VW)
    pre_moe = _post_mixer(o_a, o_b, proj, x, mod, per_token, w["w_down_a"], w["w_down_b"], w["w_out"],
                          w["norm_ffn"], w["router_w"], tm_post)
    return pre_moe, cb, sd, sr


def kernel(x_prompt, x_sample, state_conv, state_delta, state_ret, c_prompt, c_sample, w_mod, b_mod, norm_mix_w, w_in, conv_w, a_log, dt_bias, dn_norm_w, ret_gn_w, ret_gn_b, w_down_a, w_down_b, w_out, norm_ffn_w, router_w, router_bias, w_gate, w_up, w_down, ws_gate, ws_up, ws_down, final_norm_w):
    bp, lp, _ = x_prompt.shape
    bs = x_sample.shape[0]
    half = RET_DK // 2
    w_in0 = w_in[0]
    c0, c1 = DN_CONV_W, DN_CONV_W + 2 * DN_HEADS
    w_ba = jnp.pad(w_in0[:, c0:c1], ((0, 0), (0, BA_W - 2 * DN_HEADS))).astype(BF16)
    w = {
        "norm_mix": norm_mix_w[0].reshape(1, D_MODEL),
        "w_main": jnp.concatenate([w_in0[:, :c0], w_in0[:, c1:]], axis=1).astype(BF16),
        "w_ba": w_ba,
        "w_bat": w_in0[:, c0:c1].T.astype(BF16),
        "conv_w": conv_w[0], "a_log": a_log[0], "dt_bias": dt_bias[0], "dn_norm": dn_norm_w[0],
        "inv_freq": (ROPE_BASE ** (-jnp.arange(half, dtype=F32) / half)).reshape(1, half),
        "gn_w": ret_gn_w[0], "gn_b": ret_gn_b[0],
        "w_down_a": w_down_a[0].astype(BF16), "w_down_b": w_down_b[0].astype(BF16), "w_out": w_out[0].astype(BF16),
        "norm_ffn": norm_ffn_w[0].reshape(1, D_MODEL), "router_w": router_w[0], "router_bias": router_bias[0],
        "w_gate": w_gate[0].astype(BF16), "w_up": w_up[0].astype(BF16), "w_down": w_down[0].astype(BF16),
        "ws_gate": ws_gate[0].astype(BF16), "ws_up": ws_up[0].astype(BF16), "ws_down": ws_down[0].astype(BF16),
        "final_norm": final_norm_w.reshape(1, D_MODEL),
    }
    mod = _modulation(jnp.concatenate([c_prompt, c_sample], axis=0), w_mod[0], b_mod[0])
    mod_p = mod[:bp].reshape(bp, 1, MOD_CHUNKS * D_MODEL)
    mod_s = mod[bp:].reshape(1, bs, MOD_CHUNKS * D_MODEL)

    pre_p, conv_p, delta_p, ret_p = _trunk(x_prompt, mod_p, False, None, w)
    pre_s, conv_s, delta_s, ret_s = _trunk(x_sample.reshape(1, bs, D_MODEL), mod_s, True,
                                           (state_conv[0], state_delta[0], state_ret[0]), w)
    y_p, y_s = _moe(pre_p, pre_s, mod_p, mod_s, w)
    return (y_p, y_s.reshape(bs, 1, D_MODEL), conv_p[None], delta_p[None], ret_p[None],
            conv_s[None], delta_s[None], ret_s[None])
```

```python
import functools
import math

import jax
import jax.numpy as jnp
from jax import lax
from jax.experimental import pallas as pl
from jax.experimental.pallas import tpu as pltpu

F32 = jnp.float32
BF16 = jnp.bfloat16

D_MODEL = 1024
DN_HEADS, DN_DK, DN_DV, DN_CONV = 8, 128, 128, 4
DN_QK = DN_HEADS * DN_DK
DN_VW = DN_HEADS * DN_DV
DN_CONV_W = 2 * DN_QK + DN_VW
RET_HEADS, RET_DK, RET_DV = 4, 256, 512
RET_QK = RET_HEADS * RET_DK
RET_VW = RET_HEADS * RET_DV
ROPE_BASE = 10000.0
PAST_LEN = 16384
N_EXPERTS, TOP_K, N_GROUPS, TOPK_GROUPS = 64, 8, 8, 4
GROUP_SIZE = N_EXPERTS // N_GROUPS
D_EXPERT, D_SHARED = 256, 256
ROUTED_SCALE = 2.5
MOD_CHUNKS = 6
EPS = 1e-6

PROJ_W = 12 * D_MODEL
COL_Q_A, COL_K_A, COL_V_A, COL_Z_A = 0, 1, 2, 3
COL_Q_B, COL_K_B = 4, 5
COL_V_B, COL_G_B = 3, 4
COL_GATE_A, COL_GATE_B = 10, 11
BA_W = 128
CHUNK = 128
CONV_HALO = 8
VMEM_LIMIT = 56 * 1024 * 1024
LANES = 128
MOE_BLOCK = 512
TOKEN_TILE = (D_MODEL // LANES, LANES)

NN = (((1,), (0,)), ((), ()))
NT = (((1,), (1,)), ((), ()))
TN = (((0,), (0,)), ((), ()))


def _mm(a, b, dims=NN):
    return lax.dot_general(a.astype(BF16), b.astype(BF16), dims, preferred_element_type=F32)


def _mm_hi(a, b, dims=NN):
    return lax.dot_general(a.astype(F32), b.astype(F32), dims, precision=lax.Precision.HIGHEST,
                           preferred_element_type=F32)


def _hi_lo(x):
    hi = x.astype(BF16)
    return hi, (x - hi.astype(F32)).astype(BF16)


def _split_lhs(a):
    hi, lo = _hi_lo(a)
    return jnp.concatenate([hi, lo, hi], axis=1)


def _split_rhs(b):
    hi, lo = _hi_lo(b)
    return jnp.concatenate([hi, hi, lo], axis=0)


def _mm_split(a3, b3):
    return lax.dot_general(a3, b3, NN, preferred_element_type=F32)


def _sigmoid(x):
    return jax.nn.sigmoid(x)


def _silu(x):
    return x * _sigmoid(x)


def _softplus(x):
    return jnp.maximum(x, 0.0) + jnp.log1p(jnp.exp(-jnp.abs(x)))


def _rms(x, w):
    return x * lax.rsqrt(jnp.mean(x * x, axis=-1, keepdims=True) + EPS) * w


def _params(*sem):
    return pltpu.CompilerParams(dimension_semantics=sem, vmem_limit_bytes=VMEM_LIMIT)


def _mod_kernel(c_ref, w_ref, b_ref, o_ref):
    o_ref[...] = _mm_hi(_silu(c_ref[...]), w_ref[...]) + b_ref[...]


def _modulation(c, w_mod, b_mod):
    n = c.shape[0]
    tn = 1536
    return pl.pallas_call(
        _mod_kernel,
        out_shape=jax.ShapeDtypeStruct((n, MOD_CHUNKS * D_MODEL), F32),
        grid=(MOD_CHUNKS * D_MODEL // tn,),
        in_specs=[pl.BlockSpec((n, D_MODEL), lambda j: (0, 0)),
                  pl.BlockSpec((D_MODEL, tn), lambda j: (0, j)),
                  pl.BlockSpec((1, tn), lambda j: (0, j))],
        out_specs=pl.BlockSpec((n, tn), lambda j: (0, j)),
        compiler_params=_params("parallel"),
        name="modulation",
    )(c, w_mod, b_mod.reshape(1, -1))


def _mod_spec(per_token, tm, col):
    if per_token:
        return pl.BlockSpec((None, tm, D_MODEL), lambda b, i, *_: (b, i, col))
    return pl.BlockSpec((None, 1, D_MODEL), lambda b, i, *_: (b, 0, col))


def _inproj_kernel(x_ref, sh_ref, sc_ref, nw_ref, w_ref, wba_ref, wbat_ref, out_ref, ba_ref, bat_ref, h_scr):
    @pl.when(pl.program_id(2) == 0)
    def _():
        h = _rms(x_ref[...], nw_ref[...]) * (1.0 + sc_ref[...]) + sh_ref[...]
        hb = h.astype(BF16)
        h_scr[...] = hb
        ba_ref[...] = _mm(hb, wba_ref[...])
        bat_ref[...] = _mm(wbat_ref[...], hb, NT)

    out_ref[...] = _mm(h_scr[...], w_ref[...]).astype(BF16)


def _in_projection(x, mod, per_token, norm_w, w_main, w_ba, w_bat, tm, tn):
    nb, l, _ = x.shape
    return pl.pallas_call(
        _inproj_kernel,
        out_shape=(jax.ShapeDtypeStruct((nb, l, PROJ_W), BF16),
                   jax.ShapeDtypeStruct((nb, l, BA_W), F32),
                   jax.ShapeDtypeStruct((nb, 2 * DN_HEADS, l), F32)),
        grid=(nb, l // tm, PROJ_W // tn),
        in_specs=[pl.BlockSpec((None, tm, D_MODEL), lambda b, i, j: (b, i, 0)),
                  _mod_spec(per_token, tm, 0),
                  _mod_spec(per_token, tm, 1),
                  pl.BlockSpec((1, D_MODEL), lambda b, i, j: (0, 0)),
                  pl.BlockSpec((D_MODEL, tn), lambda b, i, j: (0, j)),
                  pl.BlockSpec((D_MODEL, BA_W), lambda b, i, j: (0, 0)),
                  pl.BlockSpec((2 * DN_HEADS, D_MODEL), lambda b, i, j: (0, 0))],
        out_specs=(pl.BlockSpec((None, tm, tn), lambda b, i, j: (b, i, j)),
                   pl.BlockSpec((None, tm, BA_W), lambda b, i, j: (b, i, 0)),
                   pl.BlockSpec((None, 2 * DN_HEADS, tm), lambda b, i, j: (b, 0, i))),
        scratch_shapes=[pltpu.VMEM((tm, D_MODEL), BF16)],
        compiler_params=_params("parallel", "parallel", "arbitrary"),
        name="in_projection",
    )(x, mod, mod, norm_w, w_main, w_ba, w_bat)


def _delta_prefill_kernel(q_ref, k_ref, v_ref, z_ref, bat_ref, cw_ref, alog_c_ref, dtb_c_ref, nw_ref, s0_ref, c0_ref,
                          o_ref, sout_ref, cout_ref,
                          ext_scr, s_scr, qkv_scr, pow_scr, inv_scr, uw_scr, phi_scr, plo_scr):
    step = pl.program_id(1)
    c = CHUNK
    lo = CONV_HALO - (DN_CONV - 1)

    @pl.when(step == 0)
    def _():
        s_scr[...] = s0_ref[...]
        ext_scr[lo:CONV_HALO, :] = c0_ref[...]

    ext_scr[CONV_HALO:, 0:DN_QK] = q_ref[...].astype(F32)
    ext_scr[CONV_HALO:, DN_QK:2 * DN_QK] = k_ref[...].astype(F32)
    ext_scr[CONV_HALO:, 2 * DN_QK:] = v_ref[...].astype(F32)

    row = lax.broadcasted_iota(jnp.int32, (c, c), 0)
    col = lax.broadcasted_iota(jnp.int32, (c, c), 1)
    causal = row >= col
    strict = row > col
    eye = (row == col).astype(F32)
    utri = (row <= col).astype(F32)

    bat = bat_ref[...]
    g_t = -jnp.exp(alog_c_ref[...]) * _softplus(bat[DN_HEADS:, :] + dtb_c_ref[...])
    gc_t = _mm_hi(g_t, utri)
    gates = _mm_hi(jnp.concatenate([_sigmoid(bat[:DN_HEADS, :]), gc_t], axis=0), _eye(2 * DN_HEADS), TN)
    beta_tok, gc_tok = gates[:, :DN_HEADS], gates

    def conv(c0):
        acc = ext_scr[lo:lo + c, c0:c0 + DN_DK] * cw_ref[0:1, c0:c0 + DN_DK]
        for i in range(1, DN_CONV):
            acc = acc + ext_scr[lo + i:lo + i + c, c0:c0 + DN_DK] * cw_ref[i:i + 1, c0:c0 + DN_DK]
        return _silu(acc)

    def decay_of(h):
        gc = gc_tok[:, DN_HEADS + h:DN_HEADS + h + 1]
        return gc, jnp.where(causal, jnp.exp(jnp.where(causal, gc - gc_t[h:h + 1, :], 0.0)), 0.0)

    for h in range(DN_HEADS):
        q = conv(h * DN_DK)
        k = conv(DN_QK + h * DN_DK)
        q = q * lax.rsqrt(jnp.sum(q * q, axis=-1, keepdims=True) + EPS) * (DN_DK ** -0.5)
        k = k * lax.rsqrt(jnp.sum(k * k, axis=-1, keepdims=True) + EPS)
        qkv_scr[:, h * DN_DK:(h + 1) * DN_DK] = q
        qkv_scr[:, DN_QK + h * DN_DK:DN_QK + (h + 1) * DN_DK] = k
        qkv_scr[:, 2 * DN_QK + h * DN_DV:2 * DN_QK + (h + 1) * DN_DV] = conv(2 * DN_QK + h * DN_DV)
        _, decay = decay_of(h)
        a_mat = jnp.where(strict, _mm(k * beta_tok[:, h:h + 1], k, NT) * decay, 0.0)
        phi_scr[h], plo_scr[h] = _hi_lo(a_mat)
        inv_scr[h] = eye - a_mat

    for _ in range(int(math.log2(c)) - 1):
        for h in range(DN_HEADS):
            p_hi, p_lo = phi_scr[h], plo_scr[h]
            phi_scr[h], plo_scr[h] = _hi_lo(_mm_split(jnp.concatenate([p_hi, p_lo, p_hi], axis=1),
                                                      jnp.concatenate([p_hi, p_hi, p_lo], axis=0)))
        for h in range(DN_HEADS):
            inv = inv_scr[h]
            p_hi, p_lo = phi_scr[h], plo_scr[h]
            inv_scr[h] = inv + _mm_split(_split_lhs(inv), jnp.concatenate([p_hi, p_hi, p_lo], axis=0))

    def qkv_of(h):
        return (qkv_scr[:, h * DN_DK:(h + 1) * DN_DK], qkv_scr[:, DN_QK + h * DN_DK:DN_QK + (h + 1) * DN_DK],
                qkv_scr[:, 2 * DN_QK + h * DN_DV:2 * DN_QK + (h + 1) * DN_DV])

    for h in range(DN_HEADS):
        q, k, v = qkv_of(h)
        beta = beta_tok[:, h:h + 1]
        gc, decay = decay_of(h)
        rhs = jnp.concatenate([v * beta, k * beta * jnp.exp(gc)], axis=-1)
        uw_scr[h] = _mm_split(_split_lhs(inv_scr[h]), _split_rhs(rhs))
        pow_scr[h] = jnp.where(causal, _mm(q, k, NT) * decay, 0.0)
    for h in range(DN_HEADS):
        inv_scr[h] = uw_scr[h, :, 0:DN_DV] - _mm(uw_scr[h, :, DN_DV:], s_scr[h])
    for h in range(DN_HEADS):
        q, k, _ = qkv_of(h)
        gc = gc_tok[:, DN_HEADS + h:DN_HEADS + h + 1]
        g_last = gc[c - 1:c, :]
        s = s_scr[h]
        v_new = inv_scr[h]
        o = _mm(q * jnp.exp(gc), s) + _mm(pow_scr[h], v_new)
        s_scr[h] = s * jnp.exp(g_last) + _mm(k * jnp.exp(g_last - gc), v_new, TN)
        z = z_ref[:, h * DN_DV:(h + 1) * DN_DV].astype(F32)
        o_ref[:, h * DN_DV:(h + 1) * DN_DV] = (_rms(o, nw_ref[...]) * _silu(z)).astype(BF16)

    ext_scr[0:CONV_HALO, :] = ext_scr[c:c + CONV_HALO, :]

    @pl.when(step == pl.num_programs(1) - 1)
    def _():
        sout_ref[...] = s_scr[...]
        cout_ref[...] = ext_scr[lo:CONV_HALO, :]


def _lane_pad(v, offset):
    return jnp.zeros((1, BA_W), F32).at[0, offset:offset + v.shape[0]].set(v.astype(F32))


def _delta_prefill(proj, bat, conv_w, a_log, dt_bias, dn_norm_w, s0, c0):
    nb, l, _ = proj.shape
    c = CHUNK
    col_spec = lambda j: pl.BlockSpec((None, c, D_MODEL), lambda b, i: (b, i, j))
    full = lambda shape: pl.BlockSpec(shape, lambda b, i: (0,) * len(shape))
    return pl.pallas_call(
        _delta_prefill_kernel,
        out_shape=(jax.ShapeDtypeStruct((nb, l, DN_VW), BF16),
                   jax.ShapeDtypeStruct((nb, DN_HEADS, DN_DK, DN_DV), F32),
                   jax.ShapeDtypeStruct((nb, DN_CONV - 1, DN_CONV_W), F32)),
        grid=(nb, l // c),
        in_specs=[col_spec(COL_Q_A), col_spec(COL_K_A), col_spec(COL_V_A), col_spec(COL_Z_A),
                  pl.BlockSpec((None, 2 * DN_HEADS, c), lambda b, i: (b, 0, i)),
                  full((DN_CONV, DN_CONV_W)), full((DN_HEADS, 1)), full((DN_HEADS, 1)), full((1, DN_DV)),
                  pl.BlockSpec((None, DN_HEADS, DN_DK, DN_DV), lambda b, i: (b % s0.shape[0], 0, 0, 0)),
                  pl.BlockSpec((None, DN_CONV - 1, DN_CONV_W), lambda b, i: (b % c0.shape[0], 0, 0))],
        out_specs=(pl.BlockSpec((None, c, DN_VW), lambda b, i: (b, i, 0)),
                   pl.BlockSpec((None, DN_HEADS, DN_DK, DN_DV), lambda b, i: (b, 0, 0, 0)),
                   pl.BlockSpec((None, DN_CONV - 1, DN_CONV_W), lambda b, i: (b, 0, 0))),
        scratch_shapes=[pltpu.VMEM((c + CONV_HALO, DN_CONV_W), F32),
                        pltpu.VMEM((DN_HEADS, DN_DK, DN_DV), F32),
                        pltpu.VMEM((c, DN_CONV_W), F32),
                        pltpu.VMEM((DN_HEADS, c, c), F32),
                        pltpu.VMEM((DN_HEADS, c, c), F32),
                        pltpu.VMEM((DN_HEADS, c, DN_DV + DN_DK), F32),
                        pltpu.VMEM((DN_HEADS, c, c), BF16),
                        pltpu.VMEM((DN_HEADS, c, c), BF16)],
        compiler_params=_params("parallel", "arbitrary"),
        name="delta_prefill",
    )(proj, proj, proj, proj, bat, conv_w, a_log.reshape(DN_HEADS, 1), dt_bias.reshape(DN_HEADS, 1),
      dn_norm_w.reshape(1, DN_DV), s0, c0)


def _log_gamma(h):
    return math.log(1.0 - 2.0 ** (-5.0 - h))


def _rotate(x, cos, sin):
    half = x.shape[-1] // 2
    x1, x2 = x[:, :half], x[:, half:]
    return jnp.concatenate([x1 * cos - x2 * sin, x1 * sin + x2 * cos], axis=-1)


def _group_norm_gate(o, gw, gb, gate):
    mu = jnp.mean(o, axis=-1, keepdims=True)
    var = jnp.mean(jnp.square(o - mu), axis=-1, keepdims=True)
    return ((o - mu) * lax.rsqrt(var + EPS) * gw + gb) * _silu(gate)


def _ret_prefill_kernel(q_ref, k_ref, v_ref, g_ref, inv_ref, gw_ref, gb_ref, s0_ref, o_ref, sout_ref, s_scr):
    step = pl.program_id(1)
    c = CHUNK

    @pl.when(step == 0)
    def _():
        s_scr[...] = s0_ref[...]

    idx_c = lax.broadcasted_iota(jnp.int32, (c, 1), 0)
    pos = (step * c + idx_c).astype(F32)
    ang = pos * inv_ref[...]
    cos, sin = jnp.cos(ang), jnp.sin(ang)
    rel = (lax.broadcasted_iota(jnp.int32, (c, c), 0) - lax.broadcasted_iota(jnp.int32, (c, c), 1)).astype(F32)
    idx = idx_c.astype(F32)

    for h in range(RET_HEADS):
        lg = _log_gamma(h)
        q = _rotate(q_ref[:, h * RET_DK:(h + 1) * RET_DK].astype(F32), cos, sin)
        k = _rotate(k_ref[:, h * RET_DK:(h + 1) * RET_DK].astype(F32), cos, sin) * (RET_DK ** -0.5)
        v = v_ref[:, h * RET_DV:(h + 1) * RET_DV]
        d_mat = jnp.where(rel >= 0, jnp.exp(jnp.maximum(rel, 0.0) * lg), 0.0)
        intra = _mm(_mm(q, k, NT) * d_mat, v)
        s = s_scr[h]
        cross = _mm(q * jnp.exp((idx + 1.0) * lg), s)
        s_scr[h] = s * math.exp(c * lg) + _mm(k * jnp.exp((c - 1.0 - idx) * lg), v, TN)
        sl = slice(h * RET_DV, (h + 1) * RET_DV)
        o_ref[:, sl] = _group_norm_gate(intra + cross, gw_ref[:, sl], gb_ref[:, sl],
                                        g_ref[:, sl].astype(F32)).astype(BF16)

    @pl.when(step == pl.num_programs(1) - 1)
    def _():
        sout_ref[...] = s_scr[...]


def _ret_prefill(proj, inv_freq, gn_w, gn_b, s0):
    nb, l, _ = proj.shape
    c = CHUNK
    full = lambda shape: pl.BlockSpec(shape, lambda b, i: (0,) * len(shape))
    state_spec = pl.BlockSpec((None, RET_HEADS, RET_DK, RET_DV), lambda b, i: (b, 0, 0, 0))
    return pl.pallas_call(
        _ret_prefill_kernel,
        out_shape=(jax.ShapeDtypeStruct((nb, l, RET_VW), BF16),
                   jax.ShapeDtypeStruct((nb, RET_HEADS, RET_DK, RET_DV), F32)),
        grid=(nb, l // c),
        in_specs=[pl.BlockSpec((None, c, RET_QK), lambda b, i: (b, i, COL_Q_B)),
                  pl.BlockSpec((None, c, RET_QK), lambda b, i: (b, i, COL_K_B)),
                  pl.BlockSpec((None, c, RET_VW), lambda b, i: (b, i, COL_V_B)),
                  pl.BlockSpec((None, c, RET_VW), lambda b, i: (b, i, COL_G_B)),
                  full((1, RET_DK // 2)), full((1, RET_VW)), full((1, RET_VW)),
                  pl.BlockSpec((None, RET_HEADS, RET_DK, RET_DV), lambda b, i: (b % s0.shape[0], 0, 0, 0))],
        out_specs=(pl.BlockSpec((None, c, RET_VW), lambda b, i: (b, i, 0)), state_spec),
        scratch_shapes=[pltpu.VMEM((RET_HEADS, RET_DK, RET_DV), F32)],
        compiler_params=_params("parallel", "arbitrary"),
        name="retention_prefill",
    )(proj, proj, proj, proj, inv_freq, gn_w.reshape(1, RET_VW), gn_b.reshape(1, RET_VW), s0)


ROWS = 8
SEQ_PER_STEP = 4
RET_SEQ_PER_STEP = 2


def _row_mm(a, s):
    return _mm(jnp.broadcast_to(a, (ROWS, a.shape[-1])), s)[0:1, :]


def _outer(a, b):
    first = lax.broadcasted_iota(jnp.int32, (ROWS, a.shape[-1]), 0) == 0
    a8 = jnp.where(first, jnp.broadcast_to(a, (ROWS, a.shape[-1])), 0.0)
    return _mm(a8, jnp.broadcast_to(b, (ROWS, b.shape[-1])), TN)


def _delta_step_kernel(qkv_ref, z_ref, ba_ref, cw_ref, alog_ref, dtb_ref, nw_ref, s_ref, c_ref,
                       o_ref, sout_ref, cout_ref):
    for j in range(qkv_ref.shape[0]):
        u_new = qkv_ref[j].astype(F32)
        buf = c_ref[j]
        acc = u_new * cw_ref[DN_CONV - 1:DN_CONV, :]
        for i in range(DN_CONV - 1):
            acc = acc + buf[i:i + 1, :] * cw_ref[i:i + 1, :]
        qkv = _silu(acc)
        cout_ref[j, 0:DN_CONV - 2, :] = buf[1:, :]
        cout_ref[j, DN_CONV - 2:, :] = u_new
        ba = ba_ref[j]
        beta_all = _sigmoid(ba)
        g_all = -jnp.exp(alog_ref[...]) * _softplus(ba + dtb_ref[...])
        for h in range(DN_HEADS):
            q = qkv[:, h * DN_DK:(h + 1) * DN_DK]
            k = qkv[:, DN_QK + h * DN_DK:DN_QK + (h + 1) * DN_DK]
            v = qkv[:, 2 * DN_QK + h * DN_DV:2 * DN_QK + (h + 1) * DN_DV]
            q = q * lax.rsqrt(jnp.sum(q * q, axis=-1, keepdims=True) + EPS) * (DN_DK ** -0.5)
            k = k * lax.rsqrt(jnp.sum(k * k, axis=-1, keepdims=True) + EPS)
            beta = beta_all[:, h:h + 1]
            eg = jnp.exp(g_all[:, DN_HEADS + h:DN_HEADS + h + 1])
            s = s_ref[j, h]
            kb = k * beta
            v_new = v * beta - _row_mm(kb * eg, s)
            o = _row_mm(q * eg, s) + jnp.sum(q * k, axis=-1, keepdims=True) * v_new
            sout_ref[j, h] = s * eg + _outer(k, v_new)
            z = z_ref[j, :, h * DN_DV:(h + 1) * DN_DV].astype(F32)
            o_ref[j, :, h * DN_DV:(h + 1) * DN_DV] = (_rms(o, nw_ref[...]) * _silu(z)).astype(BF16)


def _delta_step(proj, ba, conv_w, a_log, dt_bias, dn_norm_w, s0, c0):
    n = proj.shape[0]
    ns = math.gcd(n, SEQ_PER_STEP)
    full = lambda shape: pl.BlockSpec(shape, lambda b: (0,) * len(shape))
    state_spec = pl.BlockSpec((ns, DN_HEADS, DN_DK, DN_DV), lambda b: (b, 0, 0, 0))
    conv_spec = pl.BlockSpec((ns, DN_CONV - 1, DN_CONV_W), lambda b: (b, 0, 0))
    return pl.pallas_call(
        _delta_step_kernel,
        out_shape=(jax.ShapeDtypeStruct((n, 1, DN_VW), BF16),
                   jax.ShapeDtypeStruct((n, DN_HEADS, DN_DK, DN_DV), F32),
                   jax.ShapeDtypeStruct((n, DN_CONV - 1, DN_CONV_W), F32)),
        grid=(n // ns,),
        in_specs=[pl.BlockSpec((ns, 1, DN_CONV_W), lambda b: (b, 0, 0)),
                  pl.BlockSpec((ns, 1, DN_VW), lambda b: (b, 0, COL_Z_A)),
                  pl.BlockSpec((ns, 1, BA_W), lambda b: (b, 0, 0)),
                  full((DN_CONV, DN_CONV_W)), full((1, BA_W)), full((1, BA_W)), full((1, DN_DV)),
                  state_spec, conv_spec],
        out_specs=(pl.BlockSpec((ns, 1, DN_VW), lambda b: (b, 0, 0)), state_spec, conv_spec),
        compiler_params=_params("parallel"),
        name="delta_step",
    )(proj, proj, ba, conv_w, _lane_pad(a_log, DN_HEADS), _lane_pad(dt_bias, DN_HEADS),
      dn_norm_w.reshape(1, DN_DV), s0, c0)


def _ret_step_kernel(q_ref, k_ref, v_ref, g_ref, inv_ref, gw_ref, gb_ref, s_ref, o_ref, sout_ref):
    ang = float(PAST_LEN) * inv_ref[...]
    cos, sin = jnp.cos(ang), jnp.sin(ang)
    for j in range(q_ref.shape[0]):
        for h in range(RET_HEADS):
            gamma = math.exp(_log_gamma(h))
            q = _rotate(q_ref[j, :, h * RET_DK:(h + 1) * RET_DK].astype(F32), cos, sin)
            k = _rotate(k_ref[j, :, h * RET_DK:(h + 1) * RET_DK].astype(F32), cos, sin) * (RET_DK ** -0.5)
            v = v_ref[j, :, h * RET_DV:(h + 1) * RET_DV].astype(F32)
            s = s_ref[j, h]
            o = jnp.sum(q * k, axis=-1, keepdims=True) * v + _row_mm(q * gamma, s)
            sout_ref[j, h] = s * gamma + _outer(k, v)
            sl = slice(h * RET_DV, (h + 1) * RET_DV)
            o_ref[j, :, sl] = _group_norm_gate(o, gw_ref[:, sl], gb_ref[:, sl],
                                               g_ref[j, :, sl].astype(F32)).astype(BF16)


def _ret_step(proj, inv_freq, gn_w, gn_b, s0):
    n = proj.shape[0]
    ns = math.gcd(n, RET_SEQ_PER_STEP)
    full = lambda shape: pl.BlockSpec(shape, lambda b: (0,) * len(shape))
    state_spec = pl.BlockSpec((ns, RET_HEADS, RET_DK, RET_DV), lambda b: (b, 0, 0, 0))
    return pl.pallas_call(
        _ret_step_kernel,
        out_shape=(jax.ShapeDtypeStruct((n, 1, RET_VW), BF16),
                   jax.ShapeDtypeStruct((n, RET_HEADS, RET_DK, RET_DV), F32)),
        grid=(n // ns,),
        in_specs=[pl.BlockSpec((ns, 1, RET_QK), lambda b: (b, 0, COL_Q_B)),
                  pl.BlockSpec((ns, 1, RET_QK), lambda b: (b, 0, COL_K_B)),
                  pl.BlockSpec((ns, 1, RET_VW), lambda b: (b, 0, COL_V_B)),
                  pl.BlockSpec((ns, 1, RET_VW), lambda b: (b, 0, COL_G_B)),
                  full((1, RET_DK // 2)), full((1, RET_VW)), full((1, RET_VW)), state_spec],
        out_specs=(pl.BlockSpec((ns, 1, RET_VW), lambda b: (b, 0, 0)), state_spec),
        compiler_params=_params("parallel"),
        name="retention_step",
    )(proj, proj, proj, proj, inv_freq, gn_w.reshape(1, RET_VW), gn_b.reshape(1, RET_VW), s0)


def _post_mixer_kernel(oa_ref, ob_ref, ga_ref, gb_ref, x_ref, g1_ref, sh2_ref, sc2_ref, wda_ref, wdb_ref,
                       wout_ref, nw_ref, rw_ref, x1_ref, h_ref, hf_ref, logit_ref):
    y_a = _mm(oa_ref[...], wda_ref[...])
    y_b = _mm(ob_ref[...], wdb_ref[...])
    merged = _sigmoid(ga_ref[...].astype(F32)) * y_a + _sigmoid(gb_ref[...].astype(F32)) * y_b
    x1 = x_ref[...] + g1_ref[...] * _mm(merged, wout_ref[...])
    x1_ref[...] = x1
    hf = _rms(x1, nw_ref[...]) * (1.0 + sc2_ref[...]) + sh2_ref[...]
    h_ref[...] = hf.astype(BF16)
    hf_ref[...] = hf.reshape(hf_ref.shape)
    logit_ref[...] = _mm_hi(rw_ref[...], hf, NT)


def _post_mixer(o_a, o_b, proj, x, mod, per_token, w_down_a, w_down_b, w_out, norm_w, router_w, tm):
    nb, l, _ = x.shape
    tok = lambda w, j=0: pl.BlockSpec((None, tm, w), lambda b, i: (b, i, j))
    full = lambda shape: pl.BlockSpec(shape, lambda b, i: (0,) * len(shape))
    return pl.pallas_call(
        _post_mixer_kernel,
        out_shape=(jax.ShapeDtypeStruct((nb, l, D_MODEL), F32),
                   jax.ShapeDtypeStruct((nb, l, D_MODEL), BF16),
                   jax.ShapeDtypeStruct((nb, l) + TOKEN_TILE, F32),
                   jax.ShapeDtypeStruct((nb, N_EXPERTS, l), F32)),
        grid=(nb, l // tm),
        in_specs=[tok(DN_VW), tok(RET_VW), tok(D_MODEL, COL_GATE_A), tok(D_MODEL, COL_GATE_B), tok(D_MODEL),
                  _mod_spec(per_token, tm, 2), _mod_spec(per_token, tm, 3), _mod_spec(per_token, tm, 4),
                  full((DN_VW, D_MODEL)), full((RET_VW, D_MODEL)), full((D_MODEL, D_MODEL)),
                  full((1, D_MODEL)), full((N_EXPERTS, D_MODEL))],
        out_specs=(tok(D_MODEL), tok(D_MODEL),
                   pl.BlockSpec((None, tm) + TOKEN_TILE, lambda b, i: (b, i, 0, 0)),
                   pl.BlockSpec((None, N_EXPERTS, tm), lambda b, i: (b, 0, i))),
        compiler_params=_params("parallel", "parallel"),
        name="post_mixer",
    )(o_a, o_b, proj, proj, x, mod, mod, mod, w_down_a, w_down_b, w_out, norm_w, router_w)


def _first_max(x, axis, n):
    m = jnp.max(x, axis=axis, keepdims=True)
    ids = lax.broadcasted_iota(jnp.int32, x.shape, axis)
    first = jnp.min(jnp.where(x == m, ids, n), axis=axis, keepdims=True)
    return m, ids == first


def _eye(n):
    return (lax.broadcasted_iota(jnp.int32, (n, n), 0) == lax.broadcasted_iota(jnp.int32, (n, n), 1)).astype(F32)


class _Streams:
    def __init__(self, nb, l, ls, tm):
        self.tm, self.tps = tm, l // tm
        self.n_p, self.n_s = nb * (l // tm), ls // tm
        self.n = self.n_p + self.n_s

    def prompt(self, s):
        sp = jnp.minimum(s, self.n_p - 1)
        return sp // self.tps, sp % self.tps

    def sample(self, s, n_tiles=None):
        return jnp.clip(s - self.n_p, 0, (self.n_s if n_tiles is None else n_tiles) - 1)


def _route_kernel(n_prompt_tiles, lp_ref, ls_ref, bias_ref, ek_ref, rk_ref, wkt_ref, cnt_ref, run_scr):
    @pl.when(pl.program_id(0) == 0)
    def _():
        run_scr[...] = jnp.zeros_like(run_scr)

    t = lp_ref.shape[-1]
    scores = _sigmoid(jnp.where(pl.program_id(0) < n_prompt_tiles, lp_ref[...], ls_ref[...]))
    biased = scores + bias_ref[...]
    neg = jnp.float32(-jnp.inf)
    b3 = biased.reshape(N_GROUPS, GROUP_SIZE, t)
    m1, hit = _first_max(b3, 1, GROUP_SIZE)
    m2 = jnp.max(jnp.where(hit, neg, b3), axis=1, keepdims=True)
    gscore = (m1 + m2).reshape(N_GROUPS, t)
    gmask = jnp.zeros((N_GROUPS, t), jnp.bool_)
    for _ in range(TOPK_GROUPS):
        _, hit = _first_max(gscore, 0, N_GROUPS)
        gmask = jnp.logical_or(gmask, hit)
        gscore = jnp.where(hit, neg, gscore)
    emask = jnp.broadcast_to(gmask.reshape(N_GROUPS, 1, t), (N_GROUPS, GROUP_SIZE, t)).reshape(N_EXPERTS, t)
    masked = jnp.where(emask, biased, neg)
    eid = lax.broadcasted_iota(jnp.int32, (N_EXPERTS, t), 0).astype(F32)
    hits, e_rows, w_rows = [], [], []
    for _ in range(TOP_K):
        _, hit = _first_max(masked, 0, N_EXPERTS)
        hits.append(hit)
        e_rows.append(jnp.sum(jnp.where(hit, eid, 0.0), axis=0, keepdims=True))
        w_rows.append(jnp.sum(jnp.where(hit, scores, 0.0), axis=0, keepdims=True))
        masked = jnp.where(hit, neg, masked)
    sel = sum(hit.astype(F32) for hit in hits)
    before = lax.broadcasted_iota(jnp.int32, (t, t), 0) < lax.broadcasted_iota(jnp.int32, (t, t), 1)
    rank = run_scr[:, 0:1] + _mm(sel, before.astype(F32))
    rk_rows = [jnp.sum(jnp.where(hit, rank, 0.0), axis=0, keepdims=True) for hit in hits]
    run_scr[...] = run_scr[...] + jnp.sum(sel, axis=1, keepdims=True)
    wk = jnp.concatenate(w_rows, axis=0)
    wk = wk / jnp.sum(wk, axis=0, keepdims=True) * ROUTED_SCALE
    ek_ref[...] = jnp.concatenate(e_rows, axis=0).astype(jnp.int32)
    rk_ref[...] = jnp.concatenate(rk_rows, axis=0).astype(jnp.int32)
    wkt_ref[...] = _mm_hi(wk, _eye(TOP_K), TN)
    cnt_ref[...] = run_scr[...]


def _route(logits_p, logits_s, router_bias, st):
    t_all = st.n * st.tm
    pick = pl.BlockSpec((None, TOP_K, st.tm), lambda s: (0, 0, s))
    return pl.pallas_call(
        functools.partial(_route_kernel, st.n_p),
        out_shape=(jax.ShapeDtypeStruct((1, TOP_K, t_all), jnp.int32),
                   jax.ShapeDtypeStruct((1, TOP_K, t_all), jnp.int32),
                   jax.ShapeDtypeStruct((1, t_all, TOP_K), F32),
                   jax.ShapeDtypeStruct((N_EXPERTS, LANES), F32)),
        grid=(st.n,),
        in_specs=[pl.BlockSpec((None, N_EXPERTS, st.tm), lambda s: (st.prompt(s)[0], 0, st.prompt(s)[1])),
                  pl.BlockSpec((None, N_EXPERTS, st.tm), lambda s: (0, 0, st.sample(s))),
                  pl.BlockSpec((N_EXPERTS, 1), lambda s: (0, 0))],
        out_specs=(pick, pick, pl.BlockSpec((None, st.tm, TOP_K), lambda s: (0, s, 0)),
                   pl.BlockSpec((N_EXPERTS, LANES), lambda s: (0, 0))),
        scratch_shapes=[pltpu.VMEM((N_EXPERTS, LANES), F32)],
        compiler_params=_params("arbitrary"),
        name="route",
    )(logits_p, logits_s, router_bias.reshape(N_EXPERTS, 1))


def _plan_kernel(block, cnt_ref, ek_ref, rk_ref, pos_ref, be_ref, pe_ref, nu_ref, start_scr):
    @pl.when(jnp.logical_and(pl.program_id(0) == 0, pl.program_id(1) == 0))
    def _():
        padded = jnp.ceil(cnt_ref[...] * (1.0 / block)) * block
        r = lax.broadcasted_iota(jnp.int32, (N_EXPERTS, N_EXPERTS), 0)
        c = lax.broadcasted_iota(jnp.int32, (N_EXPERTS, N_EXPERTS), 1)
        pad_end = _mm_hi((r >= c).astype(F32), padded)
        start_scr[...] = pad_end - padded
        pe_ref[...] = pad_end.astype(jnp.int32)
        nu_ref[...] = (pad_end[N_EXPERTS - 1:, :] * (1.0 / block)).astype(jnp.int32)
        nbp = be_ref.shape[-1]
        first_row = lax.broadcasted_iota(jnp.int32, (1, nbp), 1).astype(F32) * block
        owner = jnp.sum((pad_end[:, 0:1] <= first_row).astype(F32), axis=0, keepdims=True)
        be_ref[...] = jnp.minimum(owner, N_EXPERTS - 1.0).astype(jnp.int32)

    t = ek_ref.shape[-1]
    eid = lax.broadcasted_iota(jnp.int32, (N_EXPERTS, t), 0)
    start = start_scr[:, 0:1]
    rows = [jnp.sum(jnp.where(eid == ek_ref[k:k + 1, :], start, 0.0), axis=0, keepdims=True)
            for k in range(TOP_K)]
    pos_ref[...] = jnp.concatenate(rows, axis=0).astype(jnp.int32) + rk_ref[...]


def _plan(counts, ek, rk, tr, n_blocks, block):
    nb, _, l = ek.shape
    nbp = -(-n_blocks // LANES) * LANES
    pick = pl.BlockSpec((None, TOP_K, tr), lambda b, i: (b, 0, i))
    const = lambda shape: pl.BlockSpec(shape, lambda b, i: (0, 0))
    return pl.pallas_call(
        functools.partial(_plan_kernel, block),
        out_shape=(jax.ShapeDtypeStruct((nb, TOP_K, l), jnp.int32),
                   jax.ShapeDtypeStruct((1, nbp), jnp.int32),
                   jax.ShapeDtypeStruct((N_EXPERTS, LANES), jnp.int32),
                   jax.ShapeDtypeStruct((1, LANES), jnp.int32)),
        grid=(nb, l // tr),
        in_specs=[const((N_EXPERTS, LANES)), pick, pick],
        out_specs=(pick, const((1, nbp)), const((N_EXPERTS, LANES)), const((1, LANES))),
        scratch_shapes=[pltpu.VMEM((N_EXPERTS, LANES), F32)],
        compiler_params=_params("arbitrary", "arbitrary"),
        name="plan",
    )(counts, ek, rk)


def _dispatch_kernel(n_prompt_tiles, pe_ref, pos_ref, xp_ref, xq_ref, xs_hbm, zero_scr, sem):
    s = pl.program_id(0)
    tm = pos_ref.shape[-1]

    @pl.when(s == 0)
    def _():
        zero_scr[...] = jnp.zeros_like(zero_scr)

        def tail_copy(e):
            block = zero_scr.shape[0]
            return pltpu.make_async_copy(zero_scr, xs_hbm.at[pl.ds(pe_ref[e] - block, block)], sem)

        def nonempty(e):
            return pe_ref[e] > jnp.where(e > 0, pe_ref[jnp.maximum(e - 1, 0)], 0)

        @pl.loop(0, N_EXPERTS)
        def _(e):
            @pl.when(nonempty(e))
            def _():
                tail_copy(e).start()

        @pl.loop(0, N_EXPERTS)
        def _(e):
            @pl.when(nonempty(e))
            def _():
                tail_copy(e).wait()

    def scatter(x_ref):
        def row(k, t):
            return pltpu.make_async_copy(x_ref.at[t], xs_hbm.at[pos_ref[k, t]], sem)

        @pl.loop(0, tm)
        def _(t):
            for k in range(TOP_K):
                row(k, t).start(priority=k % 2)

        @pl.loop(0, tm)
        def _(t):
            for k in range(TOP_K):
                row(k, t).wait()

    @pl.when(s < n_prompt_tiles)
    def _():
        scatter(xp_ref)

    @pl.when(s >= n_prompt_tiles)
    def _():
        scatter(xq_ref)


def _dispatch(x_p, x_s, pos, pad_end, n_rows, st, block):
    tile = (st.tm,) + TOKEN_TILE
    flat = lambda x: x.reshape((-1,) + TOKEN_TILE)
    return pl.pallas_call(
        functools.partial(_dispatch_kernel, st.n_p),
        out_shape=jax.ShapeDtypeStruct((n_rows,) + TOKEN_TILE, F32),
        grid_spec=pltpu.PrefetchScalarGridSpec(
            num_scalar_prefetch=1,
            grid=(st.n,),
            in_specs=[pl.BlockSpec((None, TOP_K, st.tm), lambda s, pe: (0, 0, s), memory_space=pltpu.SMEM),
                      pl.BlockSpec(tile, lambda s, pe: (jnp.minimum(s, st.n_p - 1), 0, 0)),
                      pl.BlockSpec(tile, lambda s, pe: (st.sample(s), 0, 0))],
            out_specs=pl.BlockSpec(memory_space=pl.ANY),
            scratch_shapes=[pltpu.VMEM((block,) + TOKEN_TILE, F32), pltpu.SemaphoreType.DMA(())]),
        compiler_params=_params("arbitrary"),
        name="dispatch",
    )(pad_end, pos, flat(x_p), flat(x_s))


def _experts_kernel(be_ref, nu_ref, x_ref, wg_ref, wu_ref, wd_ref, y_ref):
    @pl.when(pl.program_id(0) < nu_ref[0])
    def _():
        xb = x_ref[...].reshape(x_ref.shape[0], D_MODEL).astype(BF16)
        y = _mm(_silu(_mm(xb, wg_ref[...])) * _mm(xb, wu_ref[...]), wd_ref[...])
        y_ref[...] = y.reshape(y_ref.shape)


def _experts(xs, block_expert, n_used, w_gate, w_up, w_down, block):
    n_rows, d = xs.shape[0], D_MODEL
    used = lambda i, nu: jnp.minimum(i, nu[0] - 1)
    rows = pl.BlockSpec((block,) + TOKEN_TILE, lambda i, be, nu: (used(i, nu), 0, 0))
    return pl.pallas_call(
        _experts_kernel,
        out_shape=jax.ShapeDtypeStruct((n_rows,) + TOKEN_TILE, F32),
        grid_spec=pltpu.PrefetchScalarGridSpec(
            num_scalar_prefetch=2,
            grid=(n_rows // block,),
            in_specs=[rows,
                      pl.BlockSpec((None, d, D_EXPERT), lambda i, be, nu: (be[used(i, nu)], 0, 0)),
                      pl.BlockSpec((None, d, D_EXPERT), lambda i, be, nu: (be[used(i, nu)], 0, 0)),
                      pl.BlockSpec((None, D_EXPERT, d), lambda i, be, nu: (be[used(i, nu)], 0, 0))],
            out_specs=rows),
        compiler_params=_params("arbitrary"),
        name="experts",
    )(block_expert, n_used, xs, w_gate, w_up, w_down)


def _combine_kernel(n_prompt_tiles, pos_ref, pos_next_ref, wkt_ref, ys_hbm, hp_ref, hq_ref, x1p_ref, x1q_ref,
                    g2p_ref, g2q_ref, wsg_ref, wsu_ref, wsd_ref, fw_ref, yp_ref, yq_ref, rows_scr, sem):
    s = pl.program_id(0)
    tm = pos_ref.shape[-1]
    slot = lax.rem(s, 2)
    in_prompt = s < n_prompt_tiles

    def row(p_ref, sl, k, t):
        return pltpu.make_async_copy(ys_hbm.at[p_ref[k, t]], rows_scr.at[sl, k, t], sem.at[sl])

    def fetch(p_ref, sl):
        @pl.loop(0, tm)
        def _(t):
            for k in range(TOP_K):
                row(p_ref, sl, k, t).start(priority=k % 2)

    @pl.when(s == 0)
    def _():
        fetch(pos_ref, 0)

    @pl.when(s + 1 < pl.num_programs(0))
    def _():
        fetch(pos_next_ref, 1 - slot)

    hb = jnp.where(in_prompt, hp_ref[...], hq_ref[...])
    acc = _mm(_silu(_mm(hb, wsg_ref[...])) * _mm(hb, wsu_ref[...]), wsd_ref[...])

    @pl.loop(0, tm)
    def _(t):
        for k in range(TOP_K):
            row(pos_ref, slot, k, t).wait()

    wkt = wkt_ref[...]
    for k in range(TOP_K):
        acc = acc + rows_scr[slot, k].reshape(tm, D_MODEL) * wkt[:, k:k + 1]

    @pl.when(in_prompt)
    def _():
        yp_ref[...] = _rms(x1p_ref[...] + g2p_ref[...] * acc, fw_ref[...])

    @pl.when(jnp.logical_not(in_prompt))
    def _():
        yq_ref[...] = _rms(x1q_ref[...] + g2q_ref[...] * acc, fw_ref[...])


def _combine(ys, pos, wkt, h_p, h_s, x1_p, x1_s, mod_p, mod_s, ws_gate, ws_up, ws_down, final_w, st):
    nb, l, d = x1_p.shape
    real = x1_s.shape[1] // st.tm
    prompt = lambda w, col=0: pl.BlockSpec((None, st.tm, w), lambda s: (*st.prompt(s), col))
    sample = lambda w, n, col=0: pl.BlockSpec((None, st.tm, w), lambda s: (0, st.sample(s, n), col))
    full = lambda shape: pl.BlockSpec(shape, lambda s: (0,) * len(shape))
    pick = lambda step: pl.BlockSpec((None, TOP_K, st.tm), lambda s: (0, 0, step(s)), memory_space=pltpu.SMEM)
    return pl.pallas_call(
        functools.partial(_combine_kernel, st.n_p),
        out_shape=(jax.ShapeDtypeStruct((nb, l, d), F32), jax.ShapeDtypeStruct((1, st.n_s * st.tm, d), F32)),
        grid=(st.n,),
        in_specs=[pick(lambda s: s), pick(lambda s: jnp.minimum(s + 1, st.n - 1)),
                  pl.BlockSpec((None, st.tm, TOP_K), lambda s: (0, s, 0)), pl.BlockSpec(memory_space=pl.ANY),
                  prompt(d), sample(d, real), prompt(d), sample(d, real),
                  pl.BlockSpec((None, 1, d), lambda s: (st.prompt(s)[0], 0, 5)), sample(d, real, 5),
                  full((d, D_SHARED)), full((d, D_SHARED)), full((D_SHARED, d)), full((1, d))],
        out_specs=(prompt(d), sample(d, None)),
        scratch_shapes=[pltpu.VMEM((2, TOP_K, st.tm) + TOKEN_TILE, F32), pltpu.SemaphoreType.DMA((2,))],
        compiler_params=_params("arbitrary"),
        name="combine",
    )(pos, pos, wkt, ys, h_p, h_s, x1_p, x1_s, mod_p, mod_s, ws_gate, ws_up, ws_down, final_w)


def _moe(pre_p, pre_s, mod_p, mod_s, w):
    x1_p, h_p, hf3_p, logits_p = pre_p
    x1_s, h_s, hf3_s, logits_s = pre_s
    nb, l, _ = x1_p.shape
    ls = x1_s.shape[1]
    tr = min(l, 512)
    ls_pad = -(-ls // tr) * tr
    hf3_s = jnp.pad(hf3_s, ((0, 0), (0, ls_pad - ls), (0, 0), (0, 0)))
    logits_s = jnp.pad(logits_s, ((0, 0), (0, 0), (0, ls_pad - ls)))
    st = _Streams(nb, l, ls_pad, tr)
    st_c = _Streams(nb, l, ls_pad, min(l, ls, 128))
    n_tokens = st.n * st.tm
    n_blocks = n_tokens * TOP_K // MOE_BLOCK + N_EXPERTS
    ek, rk, wkt, counts = _route(logits_p, logits_s, w["router_bias"], st)
    pos, block_expert, pad_end, n_used = _plan(counts, ek, rk, tr, n_blocks, MOE_BLOCK)
    xs = _dispatch(hf3_p, hf3_s, pos, pad_end[:, 0], n_blocks * MOE_BLOCK, st, MOE_BLOCK)
    ys = _experts(xs, block_expert[0, :n_blocks], n_used[0, :1], w["w_gate"], w["w_up"], w["w_down"], MOE_BLOCK)
    y_p, y_s = _combine(ys, pos, wkt, h_p, h_s, x1_p, x1_s, mod_p, mod_s, w["ws_gate"], w["ws_up"], w["ws_down"],
                        w["final_norm"], st_c)
    return y_p, y_s[:, :ls]


def _trunk(x, mod, per_token, states, w):
    nb, l, _ = x.shape
    tm_proj, tn_proj, tm_post = min(l, 1024), 2048, min(l, 512)
    proj, ba, bat = _in_projection(x, mod, per_token, w["norm_mix"], w["w_main"], w["w_ba"], w["w_bat"],
                                   tm_proj, tn_proj)
    if states is None:
        c0 = jnp.zeros((1, DN_CONV - 1, DN_CONV_W), F32)
        sd0 = jnp.zeros((1, DN_HEADS, DN_DK, DN_DV), F32)
        sr0 = jnp.zeros((1, RET_HEADS, RET_DK, RET_DV), F32)
        o_a, sd, cb = _delta_prefill(proj, bat, w["conv_w"], w["a_log"], w["dt_bias"], w["dn_norm"], sd0, c0)
        o_b, sr = _ret_prefill(proj, w["inv_freq"], w["gn_w"], w["gn_b"], sr0)
    else:
        c0, sd0, sr0 = states
        n = nb * l
        o_a, sd, cb = _delta_step(proj.reshape(n, 1, PROJ_W), ba.reshape(n, 1, BA_W), w["conv_w"], w["a_log"],
                                  w["dt_bias"], w["dn_norm"], sd0, c0)
        o_b, sr = _ret_step(proj.reshape(n, 1, PROJ_W), w["inv_freq"], w["gn_w"], w["gn_b"], sr0)
        o_a = o_a.reshape(nb, l, DN_VW)
        o_b = o_b.reshape(nb, l, RET_VW)
    pre_moe = _post_mixer(o_a, o_b, proj, x, mod, per_token, w["w_down_a"], w["w_down_b"], w["w_out"],
                          w["norm_ffn"], w["router_w"], tm_post)
    return pre_moe, cb, sd, sr


def kernel(x_prompt, x_sample, state_conv, state_delta, state_ret, c_prompt, c_sample, w_mod, b_mod, norm_mix_w, w_in, conv_w, a_log, dt_bias, dn_norm_w, ret_gn_w, ret_gn_b, w_down_a, w_down_b, w_out, norm_ffn_w, router_w, router_bias, w_gate, w_up, w_down, ws_gate, ws_up, ws_down, final_norm_w):
    bp, lp, _ = x_prompt.shape
    bs = x_sample.shape[0]
    half = RET_DK // 2
    w_in0 = w_in[0]
    c0, c1 = DN_CONV_W, DN_CONV_W + 2 * DN_HEADS
    w_ba = jnp.pad(w_in0[:, c0:c1], ((0, 0), (0, BA_W - 2 * DN_HEADS))).astype(BF16)
    w = {
        "norm_mix": norm_mix_w[0].reshape(1, D_MODEL),
        "w_main": jnp.concatenate([w_in0[:, :c0], w_in0[:, c1:]], axis=1).astype(BF16),
        "w_ba": w_ba,
        "w_bat": w_in0[:, c0:c1].T.astype(BF16),
        "conv_w": conv_w[0], "a_log": a_log[0], "dt_bias": dt_bias[0], "dn_norm": dn_norm_w[0],
        "inv_freq": (ROPE_BASE ** (-jnp.arange(half, dtype=F32) / half)).reshape(1, half),
        "gn_w": ret_gn_w[0], "gn_b": ret_gn_b[0],
        "w_down_a": w_down_a[0].astype(BF16), "w_down_b": w_down_b[0].astype(BF16), "w_out": w_out[0].astype(BF16),
        "norm_ffn": norm_ffn_w[0].reshape(1, D_MODEL), "router_w": router_w[0], "router_bias": router_bias[0],
        "w_gate": w_gate[0], "w_up": w_up[0], "w_down": w_down[0],
        "ws_gate": ws_gate[0].astype(BF16), "ws_up": ws_up[0].astype(BF16), "ws_down": ws_down[0].astype(BF16),
        "final_norm": final_norm_w.reshape(1, D_MODEL),
    }
    mod = _modulation(jnp.concatenate([c_prompt, c_sample], axis=0), w_mod[0], b_mod[0])
    mod_p = mod[:bp].reshape(bp, 1, MOD_CHUNKS * D_MODEL)
    mod_s = mod[bp:].reshape(1, bs, MOD_CHUNKS * D_MODEL)

    pre_p, conv_p, delta_p, ret_p = _trunk(x_prompt, mod_p, False, None, w)
    pre_s, conv_s, delta_s, ret_s = _trunk(x_sample.reshape(1, bs, D_MODEL), mod_s, True,
                                           (state_conv[0], state_delta[0], state_ret[0]), w)
    y_p, y_s = _moe(pre_p, pre_s, mod_p, mod_s, w)
    return (y_p, y_s.reshape(bs, 1, D_MODEL), conv_p[None], delta_p[None], ret_p[None],
            conv_s[None], delta_s[None], ret_s[None])
```

```python
import functools
import math

import jax
import jax.numpy as jnp
from jax import lax
from jax.experimental import pallas as pl
from jax.experimental.pallas import tpu as pltpu

F32 = jnp.float32
BF16 = jnp.bfloat16

D_MODEL = 1024
DN_HEADS, DN_DK, DN_DV, DN_CONV = 8, 128, 128, 4
DN_QK = DN_HEADS * DN_DK
DN_VW = DN_HEADS * DN_DV
DN_CONV_W = 2 * DN_QK + DN_VW
RET_HEADS, RET_DK, RET_DV = 4, 256, 512
RET_QK = RET_HEADS * RET_DK
RET_VW = RET_HEADS * RET_DV
ROPE_BASE = 10000.0
PAST_LEN = 16384
N_EXPERTS, TOP_K, N_GROUPS, TOPK_GROUPS = 64, 8, 8, 4
GROUP_SIZE = N_EXPERTS // N_GROUPS
D_EXPERT, D_SHARED = 256, 256
ROUTED_SCALE = 2.5
MOD_CHUNKS = 6
EPS = 1e-6

PROJ_W = 12 * D_MODEL
COL_Q_A, COL_K_A, COL_V_A, COL_Z_A = 0, 1, 2, 3
COL_Q_B, COL_K_B = 4, 5
COL_V_B, COL_G_B = 3, 4
COL_GATE_A, COL_GATE_B = 10, 11
BA_W = 128
CHUNK = 128
CONV_HALO = 8
VMEM_LIMIT = 56 * 1024 * 1024
LANES = 128
MOE_BLOCK = 512
TOKEN_TILE = (D_MODEL // LANES, LANES)

NN = (((1,), (0,)), ((), ()))
NT = (((1,), (1,)), ((), ()))
TN = (((0,), (0,)), ((), ()))


def _mm(a, b, dims=NN):
    return lax.dot_general(a.astype(BF16), b.astype(BF16), dims, preferred_element_type=F32)


def _mm_hi(a, b, dims=NN):
    return lax.dot_general(a.astype(F32), b.astype(F32), dims, precision=lax.Precision.HIGHEST,
                           preferred_element_type=F32)


def _hi_lo(x):
    hi = x.astype(BF16)
    return hi, (x - hi.astype(F32)).astype(BF16)


def _split_lhs(a):
    hi, lo = _hi_lo(a)
    return jnp.concatenate([hi, lo, hi], axis=1)


def _split_rhs(b):
    hi, lo = _hi_lo(b)
    return jnp.concatenate([hi, hi, lo], axis=0)


def _mm_split(a3, b3):
    return lax.dot_general(a3, b3, NN, preferred_element_type=F32)


def _sigmoid(x):
    return jax.nn.sigmoid(x)


def _silu(x):
    return x * _sigmoid(x)


def _softplus(x):
    return jnp.maximum(x, 0.0) + jnp.log1p(jnp.exp(-jnp.abs(x)))


def _rms(x, w):
    return x * lax.rsqrt(jnp.mean(x * x, axis=-1, keepdims=True) + EPS) * w


def _params(*sem):
    return pltpu.CompilerParams(dimension_semantics=sem, vmem_limit_bytes=VMEM_LIMIT)


def _mod_kernel(c_ref, w_ref, b_ref, o_ref):
    o_ref[...] = _mm_hi(_silu(c_ref[...]), w_ref[...]) + b_ref[...]


def _modulation(c, w_mod, b_mod):
    n = c.shape[0]
    tn = 1536
    return pl.pallas_call(
        _mod_kernel,
        out_shape=jax.ShapeDtypeStruct((n, MOD_CHUNKS * D_MODEL), F32),
        grid=(MOD_CHUNKS * D_MODEL // tn,),
        in_specs=[pl.BlockSpec((n, D_MODEL), lambda j: (0, 0)),
                  pl.BlockSpec((D_MODEL, tn), lambda j: (0, j)),
                  pl.BlockSpec((1, tn), lambda j: (0, j))],
        out_specs=pl.BlockSpec((n, tn), lambda j: (0, j)),
        compiler_params=_params("parallel"),
        name="modulation",
    )(c, w_mod, b_mod.reshape(1, -1))


def _mod_spec(per_token, tm, col):
    if per_token:
        return pl.BlockSpec((None, tm, D_MODEL), lambda b, i, *_: (b, i, col))
    return pl.BlockSpec((None, 1, D_MODEL), lambda b, i, *_: (b, 0, col))


def _inproj_kernel(x_ref, sh_ref, sc_ref, nw_ref, w_ref, wba_ref, wbat_ref, out_ref, ba_ref, bat_ref, h_scr):
    @pl.when(pl.program_id(2) == 0)
    def _():
        h = _rms(x_ref[...], nw_ref[...]) * (1.0 + sc_ref[...]) + sh_ref[...]
        hb = h.astype(BF16)
        h_scr[...] = hb
        ba_ref[...] = _mm(hb, wba_ref[...])
        bat_ref[...] = _mm(wbat_ref[...], hb, NT)

    out_ref[...] = _mm(h_scr[...], w_ref[...]).astype(BF16)


def _in_projection(x, mod, per_token, norm_w, w_main, w_ba, w_bat, tm, tn):
    nb, l, _ = x.shape
    return pl.pallas_call(
        _inproj_kernel,
        out_shape=(jax.ShapeDtypeStruct((nb, l, PROJ_W), BF16),
                   jax.ShapeDtypeStruct((nb, l, BA_W), F32),
                   jax.ShapeDtypeStruct((nb, 2 * DN_HEADS, l), F32)),
        grid=(nb, l // tm, PROJ_W // tn),
        in_specs=[pl.BlockSpec((None, tm, D_MODEL), lambda b, i, j: (b, i, 0)),
                  _mod_spec(per_token, tm, 0),
                  _mod_spec(per_token, tm, 1),
                  pl.BlockSpec((1, D_MODEL), lambda b, i, j: (0, 0)),
                  pl.BlockSpec((D_MODEL, tn), lambda b, i, j: (0, j)),
                  pl.BlockSpec((D_MODEL, BA_W), lambda b, i, j: (0, 0)),
                  pl.BlockSpec((2 * DN_HEADS, D_MODEL), lambda b, i, j: (0, 0))],
        out_specs=(pl.BlockSpec((None, tm, tn), lambda b, i, j: (b, i, j)),
                   pl.BlockSpec((None, tm, BA_W), lambda b, i, j: (b, i, 0)),
                   pl.BlockSpec((None, 2 * DN_HEADS, tm), lambda b, i, j: (b, 0, i))),
        scratch_shapes=[pltpu.VMEM((tm, D_MODEL), BF16)],
        compiler_params=_params("parallel", "parallel", "arbitrary"),
        name="in_projection",
    )(x, mod, mod, norm_w, w_main, w_ba, w_bat)


def _delta_prefill_kernel(q_ref, k_ref, v_ref, z_ref, bat_ref, cw_ref, alog_c_ref, dtb_c_ref, nw_ref, s0_ref, c0_ref,
                          o_ref, sout_ref, cout_ref,
                          ext_scr, s_scr, qkv_scr, pow_scr, inv_scr, uw_scr, phi_scr, plo_scr):
    step = pl.program_id(1)
    c = CHUNK
    lo = CONV_HALO - (DN_CONV - 1)

    @pl.when(step == 0)
    def _():
        s_scr[...] = s0_ref[...]
        ext_scr[lo:CONV_HALO, :] = c0_ref[...]

    ext_scr[CONV_HALO:, 0:DN_QK] = q_ref[...].astype(F32)
    ext_scr[CONV_HALO:, DN_QK:2 * DN_QK] = k_ref[...].astype(F32)
    ext_scr[CONV_HALO:, 2 * DN_QK:] = v_ref[...].astype(F32)

    row = lax.broadcasted_iota(jnp.int32, (c, c), 0)
    col = lax.broadcasted_iota(jnp.int32, (c, c), 1)
    causal = row >= col
    strict = row > col
    eye = (row == col).astype(F32)
    utri = (row <= col).astype(F32)

    bat = bat_ref[...]
    g_t = -jnp.exp(alog_c_ref[...]) * _softplus(bat[DN_HEADS:, :] + dtb_c_ref[...])
    gc_t = _mm_hi(g_t, utri)
    gates = _mm_hi(jnp.concatenate([_sigmoid(bat[:DN_HEADS, :]), gc_t], axis=0), _eye(2 * DN_HEADS), TN)
    beta_tok, gc_tok = gates[:, :DN_HEADS], gates

    def conv(c0):
        acc = ext_scr[lo:lo + c, c0:c0 + DN_DK] * cw_ref[0:1, c0:c0 + DN_DK]
        for i in range(1, DN_CONV):
            acc = acc + ext_scr[lo + i:lo + i + c, c0:c0 + DN_DK] * cw_ref[i:i + 1, c0:c0 + DN_DK]
        return _silu(acc)

    def decay_of(h):
        gc = gc_tok[:, DN_HEADS + h:DN_HEADS + h + 1]
        return gc, jnp.where(causal, jnp.exp(jnp.where(causal, gc - gc_t[h:h + 1, :], 0.0)), 0.0)

    for h in range(DN_HEADS):
        q = conv(h * DN_DK)
        k = conv(DN_QK + h * DN_DK)
        q = q * lax.rsqrt(jnp.sum(q * q, axis=-1, keepdims=True) + EPS) * (DN_DK ** -0.5)
        k = k * lax.rsqrt(jnp.sum(k * k, axis=-1, keepdims=True) + EPS)
        qkv_scr[:, h * DN_DK:(h + 1) * DN_DK] = q
        qkv_scr[:, DN_QK + h * DN_DK:DN_QK + (h + 1) * DN_DK] = k
        qkv_scr[:, 2 * DN_QK + h * DN_DV:2 * DN_QK + (h + 1) * DN_DV] = conv(2 * DN_QK + h * DN_DV)
        _, decay = decay_of(h)
        a_mat = jnp.where(strict, _mm(k * beta_tok[:, h:h + 1], k, NT) * decay, 0.0)
        phi_scr[h], plo_scr[h] = _hi_lo(a_mat)
        inv_scr[h] = eye - a_mat

    for _ in range(int(math.log2(c)) - 1):
        for h in range(DN_HEADS):
            p_hi, p_lo = phi_scr[h], plo_scr[h]
            phi_scr[h], plo_scr[h] = _hi_lo(_mm_split(jnp.concatenate([p_hi, p_lo, p_hi], axis=1),
                                                      jnp.concatenate([p_hi, p_hi, p_lo], axis=0)))
        for h in range(DN_HEADS):
            inv = inv_scr[h]
            p_hi, p_lo = phi_scr[h], plo_scr[h]
            inv_scr[h] = inv + _mm_split(_split_lhs(inv), jnp.concatenate([p_hi, p_hi, p_lo], axis=0))

    def qkv_of(h):
        return (qkv_scr[:, h * DN_DK:(h + 1) * DN_DK], qkv_scr[:, DN_QK + h * DN_DK:DN_QK + (h + 1) * DN_DK],
                qkv_scr[:, 2 * DN_QK + h * DN_DV:2 * DN_QK + (h + 1) * DN_DV])

    for h in range(DN_HEADS):
        q, k, v = qkv_of(h)
        beta = beta_tok[:, h:h + 1]
        gc, decay = decay_of(h)
        rhs = jnp.concatenate([v * beta, k * beta * jnp.exp(gc)], axis=-1)
        uw_scr[h] = _mm_split(_split_lhs(inv_scr[h]), _split_rhs(rhs))
        pow_scr[h] = jnp.where(causal, _mm(q, k, NT) * decay, 0.0)
    for h in range(DN_HEADS):
        inv_scr[h] = uw_scr[h, :, 0:DN_DV] - _mm(uw_scr[h, :, DN_DV:], s_scr[h])
    for h in range(DN_HEADS):
        q, k, _ = qkv_of(h)
        gc = gc_tok[:, DN_HEADS + h:DN_HEADS + h + 1]
        g_last = gc[c - 1:c, :]
        s = s_scr[h]
        v_new = inv_scr[h]
        o = _mm(q * jnp.exp(gc), s) + _mm(pow_scr[h], v_new)
        s_scr[h] = s * jnp.exp(g_last) + _mm(k * jnp.exp(g_last - gc), v_new, TN)
        z = z_ref[:, h * DN_DV:(h + 1) * DN_DV].astype(F32)
        o_ref[:, h * DN_DV:(h + 1) * DN_DV] = (_rms(o, nw_ref[...]) * _silu(z)).astype(BF16)

    ext_scr[0:CONV_HALO, :] = ext_scr[c:c + CONV_HALO, :]

    @pl.when(step == pl.num_programs(1) - 1)
    def _():
        sout_ref[...] = s_scr[...]
        cout_ref[...] = ext_scr[lo:CONV_HALO, :]


def _lane_pad(v, offset):
    return jnp.zeros((1, BA_W), F32).at[0, offset:offset + v.shape[0]].set(v.astype(F32))


def _delta_prefill(proj, bat, conv_w, a_log, dt_bias, dn_norm_w, s0, c0):
    nb, l, _ = proj.shape
    c = CHUNK
    col_spec = lambda j: pl.BlockSpec((None, c, D_MODEL), lambda b, i: (b, i, j))
    full = lambda shape: pl.BlockSpec(shape, lambda b, i: (0,) * len(shape))
    return pl.pallas_call(
        _delta_prefill_kernel,
        out_shape=(jax.ShapeDtypeStruct((nb, l, DN_VW), BF16),
                   jax.ShapeDtypeStruct((nb, DN_HEADS, DN_DK, DN_DV), F32),
                   jax.ShapeDtypeStruct((nb, DN_CONV - 1, DN_CONV_W), F32)),
        grid=(nb, l // c),
        in_specs=[col_spec(COL_Q_A), col_spec(COL_K_A), col_spec(COL_V_A), col_spec(COL_Z_A),
                  pl.BlockSpec((None, 2 * DN_HEADS, c), lambda b, i: (b, 0, i)),
                  full((DN_CONV, DN_CONV_W)), full((DN_HEADS, 1)), full((DN_HEADS, 1)), full((1, DN_DV)),
                  pl.BlockSpec((None, DN_HEADS, DN_DK, DN_DV), lambda b, i: (b % s0.shape[0], 0, 0, 0)),
                  pl.BlockSpec((None, DN_CONV - 1, DN_CONV_W), lambda b, i: (b % c0.shape[0], 0, 0))],
        out_specs=(pl.BlockSpec((None, c, DN_VW), lambda b, i: (b, i, 0)),
                   pl.BlockSpec((None, DN_HEADS, DN_DK, DN_DV), lambda b, i: (b, 0, 0, 0)),
                   pl.BlockSpec((None, DN_CONV - 1, DN_CONV_W), lambda b, i: (b, 0, 0))),
        scratch_shapes=[pltpu.VMEM((c + CONV_HALO, DN_CONV_W), F32),
                        pltpu.VMEM((DN_HEADS, DN_DK, DN_DV), F32),
                        pltpu.VMEM((c, DN_CONV_W), F32),
                        pltpu.VMEM((DN_HEADS, c, c), F32),
                        pltpu.VMEM((DN_HEADS, c, c), F32),
                        pltpu.VMEM((DN_HEADS, c, DN_DV + DN_DK), F32),
                        pltpu.VMEM((DN_HEADS, c, c), BF16),
                        pltpu.VMEM((DN_HEADS, c, c), BF16)],
        compiler_params=_params("parallel", "arbitrary"),
        name="delta_prefill",
    )(proj, proj, proj, proj, bat, conv_w, a_log.reshape(DN_HEADS, 1), dt_bias.reshape(DN_HEADS, 1),
      dn_norm_w.reshape(1, DN_DV), s0, c0)


def _log_gamma(h):
    return math.log(1.0 - 2.0 ** (-5.0 - h))


def _rotate(x, cos, sin):
    half = x.shape[-1] // 2
    x1, x2 = x[:, :half], x[:, half:]
    return jnp.concatenate([x1 * cos - x2 * sin, x1 * sin + x2 * cos], axis=-1)


def _group_norm_gate(o, gw, gb, gate):
    mu = jnp.mean(o, axis=-1, keepdims=True)
    var = jnp.mean(jnp.square(o - mu), axis=-1, keepdims=True)
    return ((o - mu) * lax.rsqrt(var + EPS) * gw + gb) * _silu(gate)


def _ret_prefill_kernel(q_ref, k_ref, v_ref, g_ref, inv_ref, gw_ref, gb_ref, s0_ref, o_ref, sout_ref, s_scr):
    step = pl.program_id(1)
    c = CHUNK

    @pl.when(step == 0)
    def _():
        s_scr[...] = s0_ref[...]

    idx_c = lax.broadcasted_iota(jnp.int32, (c, 1), 0)
    pos = (step * c + idx_c).astype(F32)
    ang = pos * inv_ref[...]
    cos, sin = jnp.cos(ang), jnp.sin(ang)
    rel = (lax.broadcasted_iota(jnp.int32, (c, c), 0) - lax.broadcasted_iota(jnp.int32, (c, c), 1)).astype(F32)
    idx = idx_c.astype(F32)

    for h in range(RET_HEADS):
        lg = _log_gamma(h)
        q = _rotate(q_ref[:, h * RET_DK:(h + 1) * RET_DK].astype(F32), cos, sin)
        k = _rotate(k_ref[:, h * RET_DK:(h + 1) * RET_DK].astype(F32), cos, sin) * (RET_DK ** -0.5)
        v = v_ref[:, h * RET_DV:(h + 1) * RET_DV]
        d_mat = jnp.where(rel >= 0, jnp.exp(jnp.maximum(rel, 0.0) * lg), 0.0)
        intra = _mm(_mm(q, k, NT) * d_mat, v)
        s = s_scr[h]
        cross = _mm(q * jnp.exp((idx + 1.0) * lg), s)
        s_scr[h] = s * math.exp(c * lg) + _mm(k * jnp.exp((c - 1.0 - idx) * lg), v, TN)
        sl = slice(h * RET_DV, (h + 1) * RET_DV)
        o_ref[:, sl] = _group_norm_gate(intra + cross, gw_ref[:, sl], gb_ref[:, sl],
                                        g_ref[:, sl].astype(F32)).astype(BF16)

    @pl.when(step == pl.num_programs(1) - 1)
    def _():
        sout_ref[...] = s_scr[...]


def _ret_prefill(proj, inv_freq, gn_w, gn_b, s0):
    nb, l, _ = proj.shape
    c = CHUNK
    full = lambda shape: pl.BlockSpec(shape, lambda b, i: (0,) * len(shape))
    state_spec = pl.BlockSpec((None, RET_HEADS, RET_DK, RET_DV), lambda b, i: (b, 0, 0, 0))
    return pl.pallas_call(
        _ret_prefill_kernel,
        out_shape=(jax.ShapeDtypeStruct((nb, l, RET_VW), BF16),
                   jax.ShapeDtypeStruct((nb, RET_HEADS, RET_DK, RET_DV), F32)),
        grid=(nb, l // c),
        in_specs=[pl.BlockSpec((None, c, RET_QK), lambda b, i: (b, i, COL_Q_B)),
                  pl.BlockSpec((None, c, RET_QK), lambda b, i: (b, i, COL_K_B)),
                  pl.BlockSpec((None, c, RET_VW), lambda b, i: (b, i, COL_V_B)),
                  pl.BlockSpec((None, c, RET_VW), lambda b, i: (b, i, COL_G_B)),
                  full((1, RET_DK // 2)), full((1, RET_VW)), full((1, RET_VW)),
                  pl.BlockSpec((None, RET_HEADS, RET_DK, RET_DV), lambda b, i: (b % s0.shape[0], 0, 0, 0))],
        out_specs=(pl.BlockSpec((None, c, RET_VW), lambda b, i: (b, i, 0)), state_spec),
        scratch_shapes=[pltpu.VMEM((RET_HEADS, RET_DK, RET_DV), F32)],
        compiler_params=_params("parallel", "arbitrary"),
        name="retention_prefill",
    )(proj, proj, proj, proj, inv_freq, gn_w.reshape(1, RET_VW), gn_b.reshape(1, RET_VW), s0)


ROWS = 8
SEQ_PER_STEP = 4
RET_SEQ_PER_STEP = 4


def _row_mm(a, s):
    return _mm(jnp.broadcast_to(a, (ROWS, a.shape[-1])), s)[0:1, :]


def _outer(a, b):
    first = lax.broadcasted_iota(jnp.int32, (ROWS, a.shape[-1]), 0) == 0
    a8 = jnp.where(first, jnp.broadcast_to(a, (ROWS, a.shape[-1])), 0.0)
    return _mm(a8, jnp.broadcast_to(b, (ROWS, b.shape[-1])), TN)


def _delta_step_kernel(qkv_ref, z_ref, ba_ref, cw_ref, alog_ref, dtb_ref, nw_ref, s_ref, c_ref,
                       o_ref, sout_ref, cout_ref):
    for j in range(qkv_ref.shape[0]):
        u_new = qkv_ref[j].astype(F32)
        buf = c_ref[j]
        acc = u_new * cw_ref[DN_CONV - 1:DN_CONV, :]
        for i in range(DN_CONV - 1):
            acc = acc + buf[i:i + 1, :] * cw_ref[i:i + 1, :]
        qkv = _silu(acc)
        cout_ref[j, 0:DN_CONV - 2, :] = buf[1:, :]
        cout_ref[j, DN_CONV - 2:, :] = u_new
        ba = ba_ref[j]
        beta_all = _sigmoid(ba)
        g_all = -jnp.exp(alog_ref[...]) * _softplus(ba + dtb_ref[...])
        for h in range(DN_HEADS):
            q = qkv[:, h * DN_DK:(h + 1) * DN_DK]
            k = qkv[:, DN_QK + h * DN_DK:DN_QK + (h + 1) * DN_DK]
            v = qkv[:, 2 * DN_QK + h * DN_DV:2 * DN_QK + (h + 1) * DN_DV]
            q = q * lax.rsqrt(jnp.sum(q * q, axis=-1, keepdims=True) + EPS) * (DN_DK ** -0.5)
            k = k * lax.rsqrt(jnp.sum(k * k, axis=-1, keepdims=True) + EPS)
            beta = beta_all[:, h:h + 1]
            eg = jnp.exp(g_all[:, DN_HEADS + h:DN_HEADS + h + 1])
            s = s_ref[j, h]
            kb = k * beta
            v_new = v * beta - _row_mm(kb * eg, s)
            o = _row_mm(q * eg, s) + jnp.sum(q * k, axis=-1, keepdims=True) * v_new
            sout_ref[j, h] = s * eg + _outer(k, v_new)
            z = z_ref[j, :, h * DN_DV:(h + 1) * DN_DV].astype(F32)
            o_ref[j, :, h * DN_DV:(h + 1) * DN_DV] = (_rms(o, nw_ref[...]) * _silu(z)).astype(BF16)


def _delta_step(proj, ba, conv_w, a_log, dt_bias, dn_norm_w, s0, c0):
    n = proj.shape[0]
    ns = math.gcd(n, SEQ_PER_STEP)
    full = lambda shape: pl.BlockSpec(shape, lambda b: (0,) * len(shape))
    state_spec = pl.BlockSpec((ns, DN_HEADS, DN_DK, DN_DV), lambda b: (b, 0, 0, 0))
    conv_spec = pl.BlockSpec((ns, DN_CONV - 1, DN_CONV_W), lambda b: (b, 0, 0))
    return pl.pallas_call(
        _delta_step_kernel,
        out_shape=(jax.ShapeDtypeStruct((n, 1, DN_VW), BF16),
                   jax.ShapeDtypeStruct((n, DN_HEADS, DN_DK, DN_DV), F32),
                   jax.ShapeDtypeStruct((n, DN_CONV - 1, DN_CONV_W), F32)),
        grid=(n // ns,),
        in_specs=[pl.BlockSpec((ns, 1, DN_CONV_W), lambda b: (b, 0, 0)),
                  pl.BlockSpec((ns, 1, DN_VW), lambda b: (b, 0, COL_Z_A)),
                  pl.BlockSpec((ns, 1, BA_W), lambda b: (b, 0, 0)),
                  full((DN_CONV, DN_CONV_W)), full((1, BA_W)), full((1, BA_W)), full((1, DN_DV)),
                  state_spec, conv_spec],
        out_specs=(pl.BlockSpec((ns, 1, DN_VW), lambda b: (b, 0, 0)), state_spec, conv_spec),
        compiler_params=_params("parallel"),
        name="delta_step",
    )(proj, proj, ba, conv_w, _lane_pad(a_log, DN_HEADS), _lane_pad(dt_bias, DN_HEADS),
      dn_norm_w.reshape(1, DN_DV), s0, c0)


def _ret_step_kernel(q_ref, k_ref, v_ref, g_ref, inv_ref, gw_ref, gb_ref, s_ref, o_ref, sout_ref):
    ang = float(PAST_LEN) * inv_ref[...]
    cos, sin = jnp.cos(ang), jnp.sin(ang)
    for j in range(q_ref.shape[0]):
        for h in range(RET_HEADS):
            gamma = math.exp(_log_gamma(h))
            q = _rotate(q_ref[j, :, h * RET_DK:(h + 1) * RET_DK].astype(F32), cos, sin)
            k = _rotate(k_ref[j, :, h * RET_DK:(h + 1) * RET_DK].astype(F32), cos, sin) * (RET_DK ** -0.5)
            v = v_ref[j, :, h * RET_DV:(h + 1) * RET_DV].astype(F32)
            s = s_ref[j, h]
            o = jnp.sum(q * k, axis=-1, keepdims=True) * v + _row_mm(q * gamma, s)
            sout_ref[j, h] = s * gamma + _outer(k, v)
            sl = slice(h * RET_DV, (h + 1) * RET_DV)
            o_ref[j, :, sl] = _group_norm_gate(o, gw_ref[:, sl], gb_ref[:, sl],
                                               g_ref[j, :, sl].astype(F32)).astype(BF16)


def _ret_step(proj, inv_freq, gn_w, gn_b, s0):
    n = proj.shape[0]
    ns = math.gcd(n, RET_SEQ_PER_STEP)
    full = lambda shape: pl.BlockSpec(shape, lambda b: (0,) * len(shape))
    state_spec = pl.BlockSpec((ns, RET_HEADS, RET_DK, RET_DV), lambda b: (b, 0, 0, 0))
    return pl.pallas_call(
        _ret_step_kernel,
        out_shape=(jax.ShapeDtypeStruct((n, 1, RET_VW), BF16),
                   jax.ShapeDtypeStruct((n, RET_HEADS, RET_DK, RET_DV), F32)),
        grid=(n // ns,),
        in_specs=[pl.BlockSpec((ns, 1, RET_QK), lambda b: (b, 0, COL_Q_B)),
                  pl.BlockSpec((ns, 1, RET_QK), lambda b: (b, 0, COL_K_B)),
                  pl.BlockSpec((ns, 1, RET_VW), lambda b: (b, 0, COL_V_B)),
                  pl.BlockSpec((ns, 1, RET_VW), lambda b: (b, 0, COL_G_B)),
                  full((1, RET_DK // 2)), full((1, RET_VW)), full((1, RET_VW)), state_spec],
        out_specs=(pl.BlockSpec((ns, 1, RET_VW), lambda b: (b, 0, 0)), state_spec),
        compiler_params=_params("parallel"),
        name="retention_step",
    )(proj, proj, proj, proj, inv_freq, gn_w.reshape(1, RET_VW), gn_b.reshape(1, RET_VW), s0)


def _post_mixer_kernel(oa_ref, ob_ref, ga_ref, gb_ref, x_ref, g1_ref, sh2_ref, sc2_ref, wda_ref, wdb_ref,
                       wout_ref, nw_ref, rw_ref, x1_ref, h_ref, hf_ref, logit_ref):
    y_a = _mm(oa_ref[...], wda_ref[...])
    y_b = _mm(ob_ref[...], wdb_ref[...])
    merged = _sigmoid(ga_ref[...].astype(F32)) * y_a + _sigmoid(gb_ref[...].astype(F32)) * y_b
    x1 = x_ref[...] + g1_ref[...] * _mm(merged, wout_ref[...])
    x1_ref[...] = x1
    hf = _rms(x1, nw_ref[...]) * (1.0 + sc2_ref[...]) + sh2_ref[...]
    h_ref[...] = hf.astype(BF16)
    hf_ref[...] = hf.reshape(hf_ref.shape)
    logit_ref[...] = _mm_hi(rw_ref[...], hf, NT)


def _post_mixer(o_a, o_b, proj, x, mod, per_token, w_down_a, w_down_b, w_out, norm_w, router_w, tm):
    nb, l, _ = x.shape
    tok = lambda w, j=0: pl.BlockSpec((None, tm, w), lambda b, i: (b, i, j))
    full = lambda shape: pl.BlockSpec(shape, lambda b, i: (0,) * len(shape))
    return pl.pallas_call(
        _post_mixer_kernel,
        out_shape=(jax.ShapeDtypeStruct((nb, l, D_MODEL), F32),
                   jax.ShapeDtypeStruct((nb, l, D_MODEL), BF16),
                   jax.ShapeDtypeStruct((nb, l) + TOKEN_TILE, F32),
                   jax.ShapeDtypeStruct((nb, N_EXPERTS, l), F32)),
        grid=(nb, l // tm),
        in_specs=[tok(DN_VW), tok(RET_VW), tok(D_MODEL, COL_GATE_A), tok(D_MODEL, COL_GATE_B), tok(D_MODEL),
                  _mod_spec(per_token, tm, 2), _mod_spec(per_token, tm, 3), _mod_spec(per_token, tm, 4),
                  full((DN_VW, D_MODEL)), full((RET_VW, D_MODEL)), full((D_MODEL, D_MODEL)),
                  full((1, D_MODEL)), full((N_EXPERTS, D_MODEL))],
        out_specs=(tok(D_MODEL), tok(D_MODEL),
                   pl.BlockSpec((None, tm) + TOKEN_TILE, lambda b, i: (b, i, 0, 0)),
                   pl.BlockSpec((None, N_EXPERTS, tm), lambda b, i: (b, 0, i))),
        compiler_params=_params("parallel", "parallel"),
        name="post_mixer",
    )(o_a, o_b, proj, proj, x, mod, mod, mod, w_down_a, w_down_b, w_out, norm_w, router_w)


def _first_max(x, axis, n):
    m = jnp.max(x, axis=axis, keepdims=True)
    ids = lax.broadcasted_iota(jnp.int32, x.shape, axis)
    first = jnp.min(jnp.where(x == m, ids, n), axis=axis, keepdims=True)
    return m, ids == first


def _eye(n):
    return (lax.broadcasted_iota(jnp.int32, (n, n), 0) == lax.broadcasted_iota(jnp.int32, (n, n), 1)).astype(F32)


class _Streams:
    def __init__(self, nb, l, ls, tm):
        self.tm, self.tps = tm, l // tm
        self.n_p, self.n_s = nb * (l // tm), ls // tm
        self.n = self.n_p + self.n_s

    def prompt(self, s):
        sp = jnp.minimum(s, self.n_p - 1)
        return sp // self.tps, sp % self.tps

    def sample(self, s, n_tiles=None):
        return jnp.clip(s - self.n_p, 0, (self.n_s if n_tiles is None else n_tiles) - 1)


def _route_kernel(n_prompt_tiles, lp_ref, ls_ref, bias_ref, ek_ref, rk_ref, wkt_ref, cnt_ref, run_scr):
    @pl.when(pl.program_id(0) == 0)
    def _():
        run_scr[...] = jnp.zeros_like(run_scr)

    t = lp_ref.shape[-1]
    scores = _sigmoid(jnp.where(pl.program_id(0) < n_prompt_tiles, lp_ref[...], ls_ref[...]))
    biased = scores + bias_ref[...]
    neg = jnp.float32(-jnp.inf)
    b3 = biased.reshape(N_GROUPS, GROUP_SIZE, t)
    m1, hit = _first_max(b3, 1, GROUP_SIZE)
    m2 = jnp.max(jnp.where(hit, neg, b3), axis=1, keepdims=True)
    gscore = (m1 + m2).reshape(N_GROUPS, t)
    gmask = jnp.zeros((N_GROUPS, t), jnp.bool_)
    for _ in range(TOPK_GROUPS):
        _, hit = _first_max(gscore, 0, N_GROUPS)
        gmask = jnp.logical_or(gmask, hit)
        gscore = jnp.where(hit, neg, gscore)
    emask = jnp.broadcast_to(gmask.reshape(N_GROUPS, 1, t), (N_GROUPS, GROUP_SIZE, t)).reshape(N_EXPERTS, t)
    masked = jnp.where(emask, biased, neg)
    eid = lax.broadcasted_iota(jnp.int32, (N_EXPERTS, t), 0).astype(F32)
    hits, e_rows, w_rows = [], [], []
    for _ in range(TOP_K):
        _, hit = _first_max(masked, 0, N_EXPERTS)
        hits.append(hit)
        e_rows.append(jnp.sum(jnp.where(hit, eid, 0.0), axis=0, keepdims=True))
        w_rows.append(jnp.sum(jnp.where(hit, scores, 0.0), axis=0, keepdims=True))
        masked = jnp.where(hit, neg, masked)
    sel = sum(hit.astype(F32) for hit in hits)
    before = lax.broadcasted_iota(jnp.int32, (t, t), 0) < lax.broadcasted_iota(jnp.int32, (t, t), 1)
    rank = run_scr[:, 0:1] + _mm(sel, before.astype(F32))
    rk_rows = [jnp.sum(jnp.where(hit, rank, 0.0), axis=0, keepdims=True) for hit in hits]
    run_scr[...] = run_scr[...] + jnp.sum(sel, axis=1, keepdims=True)
    wk = jnp.concatenate(w_rows, axis=0)
    wk = wk / jnp.sum(wk, axis=0, keepdims=True) * ROUTED_SCALE
    ek_ref[...] = jnp.concatenate(e_rows, axis=0).astype(jnp.int32)
    rk_ref[...] = jnp.concatenate(rk_rows, axis=0).astype(jnp.int32)
    wkt_ref[...] = _mm_hi(wk, _eye(TOP_K), TN)
    cnt_ref[...] = run_scr[...]


def _route(logits_p, logits_s, router_bias, st):
    t_all = st.n * st.tm
    pick = pl.BlockSpec((None, TOP_K, st.tm), lambda s: (0, 0, s))
    return pl.pallas_call(
        functools.partial(_route_kernel, st.n_p),
        out_shape=(jax.ShapeDtypeStruct((1, TOP_K, t_all), jnp.int32),
                   jax.ShapeDtypeStruct((1, TOP_K, t_all), jnp.int32),
                   jax.ShapeDtypeStruct((1, t_all, TOP_K), F32),
                   jax.ShapeDtypeStruct((N_EXPERTS, LANES), F32)),
        grid=(st.n,),
        in_specs=[pl.BlockSpec((None, N_EXPERTS, st.tm), lambda s: (st.prompt(s)[0], 0, st.prompt(s)[1])),
                  pl.BlockSpec((None, N_EXPERTS, st.tm), lambda s: (0, 0, st.sample(s))),
                  pl.BlockSpec((N_EXPERTS, 1), lambda s: (0, 0))],
        out_specs=(pick, pick, pl.BlockSpec((None, st.tm, TOP_K), lambda s: (0, s, 0)),
                   pl.BlockSpec((N_EXPERTS, LANES), lambda s: (0, 0))),
        scratch_shapes=[pltpu.VMEM((N_EXPERTS, LANES), F32)],
        compiler_params=_params("arbitrary"),
        name="route",
    )(logits_p, logits_s, router_bias.reshape(N_EXPERTS, 1))


def _plan_kernel(block, cnt_ref, ek_ref, rk_ref, pos_ref, be_ref, pe_ref, nu_ref, start_scr):
    @pl.when(jnp.logical_and(pl.program_id(0) == 0, pl.program_id(1) == 0))
    def _():
        padded = jnp.ceil(cnt_ref[...] * (1.0 / block)) * block
        r = lax.broadcasted_iota(jnp.int32, (N_EXPERTS, N_EXPERTS), 0)
        c = lax.broadcasted_iota(jnp.int32, (N_EXPERTS, N_EXPERTS), 1)
        pad_end = _mm_hi((r >= c).astype(F32), padded)
        start_scr[...] = pad_end - padded
        pe_ref[...] = pad_end.astype(jnp.int32)
        nu_ref[...] = (pad_end[N_EXPERTS - 1:, :] * (1.0 / block)).astype(jnp.int32)
        nbp = be_ref.shape[-1]
        first_row = lax.broadcasted_iota(jnp.int32, (1, nbp), 1).astype(F32) * block
        owner = jnp.sum((pad_end[:, 0:1] <= first_row).astype(F32), axis=0, keepdims=True)
        be_ref[...] = jnp.minimum(owner, N_EXPERTS - 1.0).astype(jnp.int32)

    t = ek_ref.shape[-1]
    eid = lax.broadcasted_iota(jnp.int32, (N_EXPERTS, t), 0)
    start = start_scr[:, 0:1]
    rows = [jnp.sum(jnp.where(eid == ek_ref[k:k + 1, :], start, 0.0), axis=0, keepdims=True)
            for k in range(TOP_K)]
    pos_ref[...] = jnp.concatenate(rows, axis=0).astype(jnp.int32) + rk_ref[...]


def _plan(counts, ek, rk, tr, n_blocks, block):
    nb, _, l = ek.shape
    nbp = -(-n_blocks // LANES) * LANES
    pick = pl.BlockSpec((None, TOP_K, tr), lambda b, i: (b, 0, i))
    const = lambda shape: pl.BlockSpec(shape, lambda b, i: (0, 0))
    return pl.pallas_call(
        functools.partial(_plan_kernel, block),
        out_shape=(jax.ShapeDtypeStruct((nb, TOP_K, l), jnp.int32),
                   jax.ShapeDtypeStruct((1, nbp), jnp.int32),
                   jax.ShapeDtypeStruct((N_EXPERTS, LANES), jnp.int32),
                   jax.ShapeDtypeStruct((1, LANES), jnp.int32)),
        grid=(nb, l // tr),
        in_specs=[const((N_EXPERTS, LANES)), pick, pick],
        out_specs=(pick, const((1, nbp)), const((N_EXPERTS, LANES)), const((1, LANES))),
        scratch_shapes=[pltpu.VMEM((N_EXPERTS, LANES), F32)],
        compiler_params=_params("arbitrary", "arbitrary"),
        name="plan",
    )(counts, ek, rk)


def _dispatch_kernel(n_prompt_tiles, pe_ref, pos_ref, xp_ref, xq_ref, xs_hbm, zero_scr, sem):
    s = pl.program_id(0)
    tm = pos_ref.shape[-1]

    @pl.when(s == 0)
    def _():
        zero_scr[...] = jnp.zeros_like(zero_scr)

        def tail_copy(e):
            block = zero_scr.shape[0]
            return pltpu.make_async_copy(zero_scr, xs_hbm.at[pl.ds(pe_ref[e] - block, block)], sem)

        def nonempty(e):
            return pe_ref[e] > jnp.where(e > 0, pe_ref[jnp.maximum(e - 1, 0)], 0)

        @pl.loop(0, N_EXPERTS)
        def _(e):
            @pl.when(nonempty(e))
            def _():
                tail_copy(e).start()

        @pl.loop(0, N_EXPERTS)
        def _(e):
            @pl.when(nonempty(e))
            def _():
                tail_copy(e).wait()

    def scatter(x_ref):
        def row(k, t):
            return pltpu.make_async_copy(x_ref.at[t], xs_hbm.at[pos_ref[k, t]], sem)

        @pl.loop(0, tm)
        def _(t):
            for k in range(TOP_K):
                row(k, t).start(priority=k % 2)

        @pl.loop(0, tm)
        def _(t):
            for k in range(TOP_K):
                row(k, t).wait()

    @pl.when(s < n_prompt_tiles)
    def _():
        scatter(xp_ref)

    @pl.when(s >= n_prompt_tiles)
    def _():
        scatter(xq_ref)


def _dispatch(x_p, x_s, pos, pad_end, n_rows, st, block):
    tile = (st.tm,) + TOKEN_TILE
    flat = lambda x: x.reshape((-1,) + TOKEN_TILE)
    return pl.pallas_call(
        functools.partial(_dispatch_kernel, st.n_p),
        out_shape=jax.ShapeDtypeStruct((n_rows,) + TOKEN_TILE, F32),
        grid_spec=pltpu.PrefetchScalarGridSpec(
            num_scalar_prefetch=1,
            grid=(st.n,),
            in_specs=[pl.BlockSpec((None, TOP_K, st.tm), lambda s, pe: (0, 0, s), memory_space=pltpu.SMEM),
                      pl.BlockSpec(tile, lambda s, pe: (jnp.minimum(s, st.n_p - 1), 0, 0)),
                      pl.BlockSpec(tile, lambda s, pe: (st.sample(s), 0, 0))],
            out_specs=pl.BlockSpec(memory_space=pl.ANY),
            scratch_shapes=[pltpu.VMEM((block,) + TOKEN_TILE, F32), pltpu.SemaphoreType.DMA(())]),
        compiler_params=_params("arbitrary"),
        name="dispatch",
    )(pad_end, pos, flat(x_p), flat(x_s))


def _experts_kernel(be_ref, nu_ref, x_ref, wg_ref, wu_ref, wd_ref, y_ref):
    @pl.when(pl.program_id(0) < nu_ref[0])
    def _():
        xb = x_ref[...].reshape(x_ref.shape[0], D_MODEL).astype(BF16)
        y = _mm(_silu(_mm(xb, wg_ref[...])) * _mm(xb, wu_ref[...]), wd_ref[...])
        y_ref[...] = y.reshape(y_ref.shape)


def _experts(xs, block_expert, n_used, w_gate, w_up, w_down, block):
    n_rows, d = xs.shape[0], D_MODEL
    used = lambda i, nu: jnp.minimum(i, nu[0] - 1)
    rows = pl.BlockSpec((block,) + TOKEN_TILE, lambda i, be, nu: (used(i, nu), 0, 0))
    return pl.pallas_call(
        _experts_kernel,
        out_shape=jax.ShapeDtypeStruct((n_rows,) + TOKEN_TILE, F32),
        grid_spec=pltpu.PrefetchScalarGridSpec(
            num_scalar_prefetch=2,
            grid=(n_rows // block,),
            in_specs=[rows,
                      pl.BlockSpec((None, d, D_EXPERT), lambda i, be, nu: (be[used(i, nu)], 0, 0)),
                      pl.BlockSpec((None, d, D_EXPERT), lambda i, be, nu: (be[used(i, nu)], 0, 0)),
                      pl.BlockSpec((None, D_EXPERT, d), lambda i, be, nu: (be[used(i, nu)], 0, 0))],
            out_specs=rows),
        compiler_params=_params("arbitrary"),
        name="experts",
    )(block_expert, n_used, xs, w_gate, w_up, w_down)


def _combine_kernel(n_prompt_tiles, pos_ref, pos_next_ref, wkt_ref, ys_hbm, hp_ref, hq_ref, x1p_ref, x1q_ref,
                    g2p_ref, g2q_ref, wsg_ref, wsu_ref, wsd_ref, fw_ref, yp_ref, yq_ref, rows_scr, sem):
    s = pl.program_id(0)
    tm = pos_ref.shape[-1]
    slot = lax.rem(s, 2)
    in_prompt = s < n_prompt_tiles

    def row(p_ref, sl, k, t):
        return pltpu.make_async_copy(ys_hbm.at[p_ref[k, t]], rows_scr.at[sl, k, t], sem.at[sl])

    def fetch(p_ref, sl):
        @pl.loop(0, tm)
        def _(t):
            for k in range(TOP_K):
                row(p_ref, sl, k, t).start(priority=k % 2)

    @pl.when(s == 0)
    def _():
        fetch(pos_ref, 0)

    @pl.when(s + 1 < pl.num_programs(0))
    def _():
        fetch(pos_next_ref, 1 - slot)

    hb = jnp.where(in_prompt, hp_ref[...], hq_ref[...])
    acc = _mm(_silu(_mm(hb, wsg_ref[...])) * _mm(hb, wsu_ref[...]), wsd_ref[...])

    @pl.loop(0, tm)
    def _(t):
        for k in range(TOP_K):
            row(pos_ref, slot, k, t).wait()

    wkt = wkt_ref[...]
    for k in range(TOP_K):
        acc = acc + rows_scr[slot, k].reshape(tm, D_MODEL) * wkt[:, k:k + 1]

    @pl.when(in_prompt)
    def _():
        yp_ref[...] = _rms(x1p_ref[...] + g2p_ref[...] * acc, fw_ref[...])

    @pl.when(jnp.logical_not(in_prompt))
    def _():
        yq_ref[...] = _rms(x1q_ref[...] + g2q_ref[...] * acc, fw_ref[...])


def _combine(ys, pos, wkt, h_p, h_s, x1_p, x1_s, mod_p, mod_s, ws_gate, ws_up, ws_down, final_w, st):
    nb, l, d = x1_p.shape
    real = x1_s.shape[1] // st.tm
    prompt = lambda w, col=0: pl.BlockSpec((None, st.tm, w), lambda s: (*st.prompt(s), col))
    sample = lambda w, n, col=0: pl.BlockSpec((None, st.tm, w), lambda s: (0, st.sample(s, n), col))
    full = lambda shape: pl.BlockSpec(shape, lambda s: (0,) * len(shape))
    pick = lambda step: pl.BlockSpec((None, TOP_K, st.tm), lambda s: (0, 0, step(s)), memory_space=pltpu.SMEM)
    return pl.pallas_call(
        functools.partial(_combine_kernel, st.n_p),
        out_shape=(jax.ShapeDtypeStruct((nb, l, d), F32), jax.ShapeDtypeStruct((1, st.n_s * st.tm, d), F32)),
        grid=(st.n,),
        in_specs=[pick(lambda s: s), pick(lambda s: jnp.minimum(s + 1, st.n - 1)),
                  pl.BlockSpec((None, st.tm, TOP_K), lambda s: (0, s, 0)), pl.BlockSpec(memory_space=pl.ANY),
                  prompt(d), sample(d, real), prompt(d), sample(d, real),
                  pl.BlockSpec((None, 1, d), lambda s: (st.prompt(s)[0], 0, 5)), sample(d, real, 5),
                  full((d, D_SHARED)), full((d, D_SHARED)), full((D_SHARED, d)), full((1, d))],
        out_specs=(prompt(d), sample(d, None)),
        scratch_shapes=[pltpu.VMEM((2, TOP_K, st.tm) + TOKEN_TILE, F32), pltpu.SemaphoreType.DMA((2,))],
        compiler_params=_params("arbitrary"),
        name="combine",
    )(pos, pos, wkt, ys, h_p, h_s, x1_p, x1_s, mod_p, mod_s, ws_gate, ws_up, ws_down, final_w)


def _moe(pre_p, pre_s, mod_p, mod_s, w):
    x1_p, h_p, hf3_p, logits_p = pre_p
    x1_s, h_s, hf3_s, logits_s = pre_s
    nb, l, _ = x1_p.shape
    ls = x1_s.shape[1]
    tr = min(l, 512)
    ls_pad = -(-ls // tr) * tr
    hf3_s = jnp.pad(hf3_s, ((0, 0), (0, ls_pad - ls), (0, 0), (0, 0)))
    logits_s = jnp.pad(logits_s, ((0, 0), (0, 0), (0, ls_pad - ls)))
    st = _Streams(nb, l, ls_pad, tr)
    st_c = _Streams(nb, l, ls_pad, min(l, ls, 128))
    n_tokens = st.n * st.tm
    n_blocks = n_tokens * TOP_K // MOE_BLOCK + N_EXPERTS
    ek, rk, wkt, counts = _route(logits_p, logits_s, w["router_bias"], st)
    pos, block_expert, pad_end, n_used = _plan(counts, ek, rk, tr, n_blocks, MOE_BLOCK)
    xs = _dispatch(hf3_p, hf3_s, pos, pad_end[:, 0], n_blocks * MOE_BLOCK, st, MOE_BLOCK)
    ys = _experts(xs, block_expert[0, :n_blocks], n_used[0, :1], w["w_gate"], w["w_up"], w["w_down"], MOE_BLOCK)
    y_p, y_s = _combine(ys, pos, wkt, h_p, h_s, x1_p, x1_s, mod_p, mod_s, w["ws_gate"], w["ws_up"], w["ws_down"],
                        w["final_norm"], st_c)
    return y_p, y_s[:, :ls]


def _trunk(x, mod, per_token, states, w):
    nb, l, _ = x.shape
    tm_proj, tn_proj, tm_post = min(l, 1024), 2048, min(l, 512)
    proj, ba, bat = _in_projection(x, mod, per_token, w["norm_mix"], w["w_main"], w["w_ba"], w["w_bat"],
                                   tm_proj, tn_proj)
    if states is None:
        c0 = jnp.zeros((1, DN_CONV - 1, DN_CONV_W), F32)
        sd0 = jnp.zeros((1, DN_HEADS, DN_DK, DN_DV), F32)
        sr0 = jnp.zeros((1, RET_HEADS, RET_DK, RET_DV), F32)
        o_a, sd, cb = _delta_prefill(proj, bat, w["conv_w"], w["a_log"], w["dt_bias"], w["dn_norm"], sd0, c0)
        o_b, sr = _ret_prefill(proj, w["inv_freq"], w["gn_w"], w["gn_b"], sr0)
    else:
        c0, sd0, sr0 = states
        n = nb * l
        o_a, sd, cb = _delta_step(proj.reshape(n, 1, PROJ_W), ba.reshape(n, 1, BA_W), w["conv_w"], w["a_log"],
                                  w["dt_bias"], w["dn_norm"], sd0, c0)
        o_b, sr = _ret_step(proj.reshape(n, 1, PROJ_W), w["inv_freq"], w["gn_w"], w["gn_b"], sr0)
        o_a = o_a.reshape(nb, l, DN_VW)
        o_b = o_b.reshape(nb, l, RET_VW)
    pre_moe = _post_mixer(o_a, o_b, proj, x, mod, per_token, w["w_down_a"], w["w_down_b"], w["w_out"],
                          w["norm_ffn"], w["router_w"], tm_post)
    return pre_moe, cb, sd, sr


def kernel(x_prompt, x_sample, state_conv, state_delta, state_ret, c_prompt, c_sample, w_mod, b_mod, norm_mix_w, w_in, conv_w, a_log, dt_bias, dn_norm_w, ret_gn_w, ret_gn_b, w_down_a, w_down_b, w_out, norm_ffn_w, router_w, router_bias, w_gate, w_up, w_down, ws_gate, ws_up, ws_down, final_norm_w):
    bp, lp, _ = x_prompt.shape
    bs = x_sample.shape[0]
    half = RET_DK // 2
    w_in0 = w_in[0]
    c0, c1 = DN_CONV_W, DN_CONV_W + 2 * DN_HEADS
    w_ba = jnp.pad(w_in0[:, c0:c1], ((0, 0), (0, BA_W - 2 * DN_HEADS))).astype(BF16)
    w = {
        "norm_mix": norm_mix_w[0].reshape(1, D_MODEL),
        "w_main": jnp.concatenate([w_in0[:, :c0].astype(BF16), w_in0[:, c1:].astype(BF16)], axis=1),
        "w_ba": w_ba,
        "w_bat": w_in0[:, c0:c1].T.astype(BF16),
        "conv_w": conv_w[0], "a_log": a_log[0], "dt_bias": dt_bias[0], "dn_norm": dn_norm_w[0],
        "inv_freq": (ROPE_BASE ** (-jnp.arange(half, dtype=F32) / half)).reshape(1, half),
        "gn_w": ret_gn_w[0], "gn_b": ret_gn_b[0],
        "w_down_a": w_down_a[0].astype(BF16), "w_down_b": w_down_b[0].astype(BF16), "w_out": w_out[0].astype(BF16),
        "norm_ffn": norm_ffn_w[0].reshape(1, D_MODEL), "router_w": router_w[0], "router_bias": router_bias[0],
        "w_gate": w_gate[0], "w_up": w_up[0], "w_down": w_down[0],
        "ws_gate": ws_gate[0].astype(BF16), "ws_up": ws_up[0].astype(BF16), "ws_down": ws_down[0].astype(BF16),
        "final_norm": final_norm_w.reshape(1, D_MODEL),
    }
    mod = _modulation(jnp.concatenate([c_prompt, c_sample], axis=0), w_mod[0], b_mod[0])
    mod_p = mod[:bp].reshape(bp, 1, MOD_CHUNKS * D_MODEL)
    mod_s = mod[bp:].reshape(1, bs, MOD_CHUNKS * D_MODEL)

    pre_p, conv_p, delta_p, ret_p = _trunk(x_prompt, mod_p, False, None, w)
    pre_s, conv_s, delta_s, ret_s = _trunk(x_sample.reshape(1, bs, D_MODEL), mod_s, True,
                                           (state_conv[0], state_delta[0], state_ret[0]), w)
    y_p, y_s = _moe(pre_p, pre_s, mod_p, mod_s, w)
    return (y_p, y_s.reshape(bs, 1, D_MODEL), conv_p[None], delta_p[None], ret_p[None],
            conv_s[None], delta_s[None], ret_s[None])
```

```python
import functools
import math

import jax
import jax.numpy as jnp
from jax import lax
from jax.experimental import pallas as pl
from jax.experimental.pallas import tpu as pltpu

F32 = jnp.float32
BF16 = jnp.bfloat16

D_MODEL = 1024
DN_HEADS, DN_DK, DN_DV, DN_CONV = 8, 128, 128, 4
DN_QK = DN_HEADS * DN_DK
DN_VW = DN_HEADS * DN_DV
DN_CONV_W = 2 * DN_QK + DN_VW
RET_HEADS, RET_DK, RET_DV = 4, 256, 512
RET_QK = RET_HEADS * RET_DK
RET_VW = RET_HEADS * RET_DV
ROPE_BASE = 10000.0
PAST_LEN = 16384
N_EXPERTS, TOP_K, N_GROUPS, TOPK_GROUPS = 64, 8, 8, 4
GROUP_SIZE = N_EXPERTS // N_GROUPS
D_EXPERT, D_SHARED = 256, 256
ROUTED_SCALE = 2.5
MOD_CHUNKS = 6
EPS = 1e-6

PROJ_W = 12 * D_MODEL
COL_Q_A, COL_K_A, COL_V_A, COL_Z_A = 0, 1, 2, 3
COL_Q_B, COL_K_B = 4, 5
COL_V_B, COL_G_B = 3, 4
COL_GATE_A, COL_GATE_B = 10, 11
BA_W = 128
CHUNK = 128
CONV_HALO = 8
VMEM_LIMIT = 56 * 1024 * 1024
LANES = 128
MOE_BLOCK = 512
TOKEN_TILE = (D_MODEL // LANES, LANES)

NN = (((1,), (0,)), ((), ()))
NT = (((1,), (1,)), ((), ()))
TN = (((0,), (0,)), ((), ()))


def _mm(a, b, dims=NN):
    return lax.dot_general(a.astype(BF16), b.astype(BF16), dims, preferred_element_type=F32)


def _mm_hi(a, b, dims=NN):
    return lax.dot_general(a.astype(F32), b.astype(F32), dims, precision=lax.Precision.HIGHEST,
                           preferred_element_type=F32)


def _hi_lo(x):
    hi = x.astype(BF16)
    return hi, (x - hi.astype(F32)).astype(BF16)


def _split_lhs(a):
    hi, lo = _hi_lo(a)
    return jnp.concatenate([hi, lo, hi], axis=1)


def _split_rhs(b):
    hi, lo = _hi_lo(b)
    return jnp.concatenate([hi, hi, lo], axis=0)


def _mm_split(a3, b3):
    return lax.dot_general(a3, b3, NN, preferred_element_type=F32)


def _sigmoid(x):
    return jax.nn.sigmoid(x)


def _silu(x):
    return x * _sigmoid(x)


def _softplus(x):
    return jnp.maximum(x, 0.0) + jnp.log1p(jnp.exp(-jnp.abs(x)))


def _rms(x, w):
    return x * lax.rsqrt(jnp.mean(x * x, axis=-1, keepdims=True) + EPS) * w


def _params(*sem):
    return pltpu.CompilerParams(dimension_semantics=sem, vmem_limit_bytes=VMEM_LIMIT)


def _mod_kernel(c_ref, w_ref, b_ref, o_ref):
    o_ref[...] = _mm_hi(_silu(c_ref[...]), w_ref[...]) + b_ref[...]


def _modulation(c, w_mod, b_mod):
    n = c.shape[0]
    tn = 1536
    return pl.pallas_call(
        _mod_kernel,
        out_shape=jax.ShapeDtypeStruct((n, MOD_CHUNKS * D_MODEL), F32),
        grid=(MOD_CHUNKS * D_MODEL // tn,),
        in_specs=[pl.BlockSpec((n, D_MODEL), lambda j: (0, 0)),
                  pl.BlockSpec((D_MODEL, tn), lambda j: (0, j)),
                  pl.BlockSpec((1, tn), lambda j: (0, j))],
        out_specs=pl.BlockSpec((n, tn), lambda j: (0, j)),
        compiler_params=_params("parallel"),
        name="modulation",
    )(c, w_mod, b_mod.reshape(1, -1))


def _mod_spec(per_token, tm, col):
    if per_token:
        return pl.BlockSpec((None, tm, D_MODEL), lambda b, i, *_: (b, i, col))
    return pl.BlockSpec((None, 1, D_MODEL), lambda b, i, *_: (b, 0, col))


def _inproj_kernel(x_ref, sh_ref, sc_ref, nw_ref, w_ref, wba_ref, out_ref, ba_ref, bat_ref, h_scr):
    @pl.when(pl.program_id(2) == 0)
    def _():
        h = _rms(x_ref[...], nw_ref[...]) * (1.0 + sc_ref[...]) + sh_ref[...]
        hb = h.astype(BF16)
        h_scr[...] = hb
        wb = wba_ref[...]
        ba_ref[...] = _mm(hb, wb)
        bat_ref[...] = _mm(wb.T, hb, NT)[0:2 * DN_HEADS]

    out_ref[...] = _mm(h_scr[...], w_ref[...]).astype(BF16)


def _in_projection(x, mod, per_token, norm_w, w_main, w_in, tm, tn):
    nb, l, _ = x.shape
    assert DN_CONV_W % BA_W == 0
    return pl.pallas_call(
        _inproj_kernel,
        out_shape=(jax.ShapeDtypeStruct((nb, l, PROJ_W), BF16),
                   jax.ShapeDtypeStruct((nb, l, BA_W), F32),
                   jax.ShapeDtypeStruct((nb, 2 * DN_HEADS, l), F32)),
        grid=(nb, l // tm, PROJ_W // tn),
        in_specs=[pl.BlockSpec((None, tm, D_MODEL), lambda b, i, j: (b, i, 0)),
                  _mod_spec(per_token, tm, 0),
                  _mod_spec(per_token, tm, 1),
                  pl.BlockSpec((1, D_MODEL), lambda b, i, j: (0, 0)),
                  pl.BlockSpec((D_MODEL, tn), lambda b, i, j: (0, j)),
                  pl.BlockSpec((D_MODEL, BA_W), lambda b, i, j: (0, DN_CONV_W // BA_W))],
        out_specs=(pl.BlockSpec((None, tm, tn), lambda b, i, j: (b, i, j)),
                   pl.BlockSpec((None, tm, BA_W), lambda b, i, j: (b, i, 0)),
                   pl.BlockSpec((None, 2 * DN_HEADS, tm), lambda b, i, j: (b, 0, i))),
        scratch_shapes=[pltpu.VMEM((tm, D_MODEL), BF16)],
        compiler_params=_params("parallel", "parallel", "arbitrary"),
        name="in_projection",
    )(x, mod, mod, norm_w, w_main, w_in)


def _delta_prefill_kernel(q_ref, k_ref, v_ref, z_ref, bat_ref, cw_ref, alog_c_ref, dtb_c_ref, nw_ref, s0_ref, c0_ref,
                          o_ref, sout_ref, cout_ref,
                          ext_scr, s_scr, qkv_scr, pow_scr, inv_scr, uw_scr, phi_scr, plo_scr):
    step = pl.program_id(1)
    c = CHUNK
    lo = CONV_HALO - (DN_CONV - 1)

    @pl.when(step == 0)
    def _():
        s_scr[...] = s0_ref[...]
        ext_scr[lo:CONV_HALO, :] = c0_ref[...]

    ext_scr[CONV_HALO:, 0:DN_QK] = q_ref[...].astype(F32)
    ext_scr[CONV_HALO:, DN_QK:2 * DN_QK] = k_ref[...].astype(F32)
    ext_scr[CONV_HALO:, 2 * DN_QK:] = v_ref[...].astype(F32)

    row = lax.broadcasted_iota(jnp.int32, (c, c), 0)
    col = lax.broadcasted_iota(jnp.int32, (c, c), 1)
    causal = row >= col
    strict = row > col
    eye = (row == col).astype(F32)
    utri = (row <= col).astype(F32)

    bat = bat_ref[...]
    g_t = -jnp.exp(alog_c_ref[...]) * _softplus(bat[DN_HEADS:, :] + dtb_c_ref[...])
    gc_t = _mm_hi(g_t, utri)
    gates = _mm_hi(jnp.concatenate([_sigmoid(bat[:DN_HEADS, :]), gc_t], axis=0), _eye(2 * DN_HEADS), TN)
    beta_tok, gc_tok = gates[:, :DN_HEADS], gates

    def conv(c0):
        acc = ext_scr[lo:lo + c, c0:c0 + DN_DK] * cw_ref[0:1, c0:c0 + DN_DK]
        for i in range(1, DN_CONV):
            acc = acc + ext_scr[lo + i:lo + i + c, c0:c0 + DN_DK] * cw_ref[i:i + 1, c0:c0 + DN_DK]
        return _silu(acc)

    def decay_of(h):
        gc = gc_tok[:, DN_HEADS + h:DN_HEADS + h + 1]
        return gc, jnp.where(causal, jnp.exp(jnp.where(causal, gc - gc_t[h:h + 1, :], 0.0)), 0.0)

    for h in range(DN_HEADS):
        q = conv(h * DN_DK)
        k = conv(DN_QK + h * DN_DK)
        q = q * lax.rsqrt(jnp.sum(q * q, axis=-1, keepdims=True) + EPS) * (DN_DK ** -0.5)
        k = k * lax.rsqrt(jnp.sum(k * k, axis=-1, keepdims=True) + EPS)
        qkv_scr[:, h * DN_DK:(h + 1) * DN_DK] = q
        qkv_scr[:, DN_QK + h * DN_DK:DN_QK + (h + 1) * DN_DK] = k
        qkv_scr[:, 2 * DN_QK + h * DN_DV:2 * DN_QK + (h + 1) * DN_DV] = conv(2 * DN_QK + h * DN_DV)
        _, decay = decay_of(h)
        a_mat = jnp.where(strict, _mm(k * beta_tok[:, h:h + 1], k, NT) * decay, 0.0)
        phi_scr[h], plo_scr[h] = _hi_lo(a_mat)
        inv_scr[h] = eye - a_mat

    for _ in range(int(math.log2(c)) - 1):
        for h in range(DN_HEADS):
            p_hi, p_lo = phi_scr[h], plo_scr[h]
            phi_scr[h], plo_scr[h] = _hi_lo(_mm_split(jnp.concatenate([p_hi, p_lo, p_hi], axis=1),
                                                      jnp.concatenate([p_hi, p_hi, p_lo], axis=0)))
        for h in range(DN_HEADS):
            inv = inv_scr[h]
            p_hi, p_lo = phi_scr[h], plo_scr[h]
            inv_scr[h] = inv + _mm_split(_split_lhs(inv), jnp.concatenate([p_hi, p_hi, p_lo], axis=0))

    def qkv_of(h):
        return (qkv_scr[:, h * DN_DK:(h + 1) * DN_DK], qkv_scr[:, DN_QK + h * DN_DK:DN_QK + (h + 1) * DN_DK],
                qkv_scr[:, 2 * DN_QK + h * DN_DV:2 * DN_QK + (h + 1) * DN_DV])

    for h in range(DN_HEADS):
        q, k, v = qkv_of(h)
        beta = beta_tok[:, h:h + 1]
        gc, decay = decay_of(h)
        rhs = jnp.concatenate([v * beta, k * beta * jnp.exp(gc)], axis=-1)
        uw_scr[h] = _mm_split(_split_lhs(inv_scr[h]), _split_rhs(rhs))
        pow_scr[h] = jnp.where(causal, _mm(q, k, NT) * decay, 0.0)
    for h in range(DN_HEADS):
        inv_scr[h] = uw_scr[h, :, 0:DN_DV] - _mm(uw_scr[h, :, DN_DV:], s_scr[h])
    for h in range(DN_HEADS):
        q, k, _ = qkv_of(h)
        gc = gc_tok[:, DN_HEADS + h:DN_HEADS + h + 1]
        g_last = gc[c - 1:c, :]
        s = s_scr[h]
        v_new = inv_scr[h]
        o = _mm(q * jnp.exp(gc), s) + _mm(pow_scr[h], v_new)
        s_scr[h] = s * jnp.exp(g_last) + _mm(k * jnp.exp(g_last - gc), v_new, TN)
        z = z_ref[:, h * DN_DV:(h + 1) * DN_DV].astype(F32)
        o_ref[:, h * DN_DV:(h + 1) * DN_DV] = (_rms(o, nw_ref[...]) * _silu(z)).astype(BF16)

    ext_scr[0:CONV_HALO, :] = ext_scr[c:c + CONV_HALO, :]

    @pl.when(step == pl.num_programs(1) - 1)
    def _():
        sout_ref[...] = s_scr[...]
        cout_ref[...] = ext_scr[lo:CONV_HALO, :]


def _lane_pad(v, offset):
    return jnp.zeros((1, BA_W), F32).at[0, offset:offset + v.shape[0]].set(v.astype(F32))


def _delta_prefill(proj, bat, conv_w, a_log, dt_bias, dn_norm_w, s0, c0):
    nb, l, _ = proj.shape
    c = CHUNK
    col_spec = lambda j: pl.BlockSpec((None, c, D_MODEL), lambda b, i: (b, i, j))
    full = lambda shape: pl.BlockSpec(shape, lambda b, i: (0,) * len(shape))
    return pl.pallas_call(
        _delta_prefill_kernel,
        out_shape=(jax.ShapeDtypeStruct((nb, l, DN_VW), BF16),
                   jax.ShapeDtypeStruct((nb, DN_HEADS, DN_DK, DN_DV), F32),
                   jax.ShapeDtypeStruct((nb, DN_CONV - 1, DN_CONV_W), F32)),
        grid=(nb, l // c),
        in_specs=[col_spec(COL_Q_A), col_spec(COL_K_A), col_spec(COL_V_A), col_spec(COL_Z_A),
                  pl.BlockSpec((None, 2 * DN_HEADS, c), lambda b, i: (b, 0, i)),
                  full((DN_CONV, DN_CONV_W)), full((DN_HEADS, 1)), full((DN_HEADS, 1)), full((1, DN_DV)),
                  pl.BlockSpec((None, DN_HEADS, DN_DK, DN_DV), lambda b, i: (b % s0.shape[0], 0, 0, 0)),
                  pl.BlockSpec((None, DN_CONV - 1, DN_CONV_W), lambda b, i: (b % c0.shape[0], 0, 0))],
        out_specs=(pl.BlockSpec((None, c, DN_VW), lambda b, i: (b, i, 0)),
                   pl.BlockSpec((None, DN_HEADS, DN_DK, DN_DV), lambda b, i: (b, 0, 0, 0)),
                   pl.BlockSpec((None, DN_CONV - 1, DN_CONV_W), lambda b, i: (b, 0, 0))),
        scratch_shapes=[pltpu.VMEM((c + CONV_HALO, DN_CONV_W), F32),
                        pltpu.VMEM((DN_HEADS, DN_DK, DN_DV), F32),
                        pltpu.VMEM((c, DN_CONV_W), F32),
                        pltpu.VMEM((DN_HEADS, c, c), F32),
                        pltpu.VMEM((DN_HEADS, c, c), F32),
                        pltpu.VMEM((DN_HEADS, c, DN_DV + DN_DK), F32),
                        pltpu.VMEM((DN_HEADS, c, c), BF16),
                        pltpu.VMEM((DN_HEADS, c, c), BF16)],
        compiler_params=_params("parallel", "arbitrary"),
        name="delta_prefill",
    )(proj, proj, proj, proj, bat, conv_w, a_log.reshape(DN_HEADS, 1), dt_bias.reshape(DN_HEADS, 1),
      dn_norm_w.reshape(1, DN_DV), s0, c0)


def _log_gamma(h):
    return math.log(1.0 - 2.0 ** (-5.0 - h))


def _rotate(x, cos, sin):
    half = x.shape[-1] // 2
    x1, x2 = x[:, :half], x[:, half:]
    return jnp.concatenate([x1 * cos - x2 * sin, x1 * sin + x2 * cos], axis=-1)


def _group_norm_gate(o, gw, gb, gate):
    mu = jnp.mean(o, axis=-1, keepdims=True)
    var = jnp.mean(jnp.square(o - mu), axis=-1, keepdims=True)
    return ((o - mu) * lax.rsqrt(var + EPS) * gw + gb) * _silu(gate)


def _ret_prefill_kernel(q_ref, k_ref, v_ref, g_ref, inv_ref, gw_ref, gb_ref, s0_ref, o_ref, sout_ref, s_scr):
    step = pl.program_id(1)
    c = CHUNK

    @pl.when(step == 0)
    def _():
        s_scr[...] = s0_ref[...]

    idx_c = lax.broadcasted_iota(jnp.int32, (c, 1), 0)
    pos = (step * c + idx_c).astype(F32)
    ang = pos * inv_ref[...]
    cos, sin = jnp.cos(ang), jnp.sin(ang)
    rel = (lax.broadcasted_iota(jnp.int32, (c, c), 0) - lax.broadcasted_iota(jnp.int32, (c, c), 1)).astype(F32)
    idx = idx_c.astype(F32)

    for h in range(RET_HEADS):
        lg = _log_gamma(h)
        q = _rotate(q_ref[:, h * RET_DK:(h + 1) * RET_DK].astype(F32), cos, sin)
        k = _rotate(k_ref[:, h * RET_DK:(h + 1) * RET_DK].astype(F32), cos, sin) * (RET_DK ** -0.5)
        v = v_ref[:, h * RET_DV:(h + 1) * RET_DV]
        d_mat = jnp.where(rel >= 0, jnp.exp(jnp.maximum(rel, 0.0) * lg), 0.0)
        intra = _mm(_mm(q, k, NT) * d_mat, v)
        s = s_scr[h]
        cross = _mm(q * jnp.exp((idx + 1.0) * lg), s)
        s_scr[h] = s * math.exp(c * lg) + _mm(k * jnp.exp((c - 1.0 - idx) * lg), v, TN)
        sl = slice(h * RET_DV, (h + 1) * RET_DV)
        o_ref[:, sl] = _group_norm_gate(intra + cross, gw_ref[:, sl], gb_ref[:, sl],
                                        g_ref[:, sl].astype(F32)).astype(BF16)

    @pl.when(step == pl.num_programs(1) - 1)
    def _():
        sout_ref[...] = s_scr[...]


def _ret_prefill(proj, inv_freq, gn_w, gn_b, s0):
    nb, l, _ = proj.shape
    c = CHUNK
    full = lambda shape: pl.BlockSpec(shape, lambda b, i: (0,) * len(shape))
    state_spec = pl.BlockSpec((None, RET_HEADS, RET_DK, RET_DV), lambda b, i: (b, 0, 0, 0))
    return pl.pallas_call(
        _ret_prefill_kernel,
        out_shape=(jax.ShapeDtypeStruct((nb, l, RET_VW), BF16),
                   jax.ShapeDtypeStruct((nb, RET_HEADS, RET_DK, RET_DV), F32)),
        grid=(nb, l // c),
        in_specs=[pl.BlockSpec((None, c, RET_QK), lambda b, i: (b, i, COL_Q_B)),
                  pl.BlockSpec((None, c, RET_QK), lambda b, i: (b, i, COL_K_B)),
                  pl.BlockSpec((None, c, RET_VW), lambda b, i: (b, i, COL_V_B)),
                  pl.BlockSpec((None, c, RET_VW), lambda b, i: (b, i, COL_G_B)),
                  full((1, RET_DK // 2)), full((1, RET_VW)), full((1, RET_VW)),
                  pl.BlockSpec((None, RET_HEADS, RET_DK, RET_DV), lambda b, i: (b % s0.shape[0], 0, 0, 0))],
        out_specs=(pl.BlockSpec((None, c, RET_VW), lambda b, i: (b, i, 0)), state_spec),
        scratch_shapes=[pltpu.VMEM((RET_HEADS, RET_DK, RET_DV), F32)],
        compiler_params=_params("parallel", "arbitrary"),
        name="retention_prefill",
    )(proj, proj, proj, proj, inv_freq, gn_w.reshape(1, RET_VW), gn_b.reshape(1, RET_VW), s0)


ROWS = 8
SEQ_PER_STEP = 4
RET_SEQ_PER_STEP = 4


def _row_mm(a, s):
    return _mm(jnp.broadcast_to(a, (ROWS, a.shape[-1])), s)[0:1, :]


def _outer(a, b):
    first = lax.broadcasted_iota(jnp.int32, (ROWS, a.shape[-1]), 0) == 0
    a8 = jnp.where(first, jnp.broadcast_to(a, (ROWS, a.shape[-1])), 0.0)
    return _mm(a8, jnp.broadcast_to(b, (ROWS, b.shape[-1])), TN)


def _delta_step_kernel(qkv_ref, z_ref, ba_ref, cw_ref, alog_ref, dtb_ref, nw_ref, s_ref, c_ref,
                       o_ref, sout_ref, cout_ref):
    for j in range(qkv_ref.shape[0]):
        u_new = qkv_ref[j].astype(F32)
        buf = c_ref[j]
        acc = u_new * cw_ref[DN_CONV - 1:DN_CONV, :]
        for i in range(DN_CONV - 1):
            acc = acc + buf[i:i + 1, :] * cw_ref[i:i + 1, :]
        qkv = _silu(acc)
        cout_ref[j, 0:DN_CONV - 2, :] = buf[1:, :]
        cout_ref[j, DN_CONV - 2:, :] = u_new
        ba = ba_ref[j]
        beta_all = _sigmoid(ba)
        g_all = -jnp.exp(alog_ref[...]) * _softplus(ba + dtb_ref[...])
        for h in range(DN_HEADS):
            q = qkv[:, h * DN_DK:(h + 1) * DN_DK]
            k = qkv[:, DN_QK + h * DN_DK:DN_QK + (h + 1) * DN_DK]
            v = qkv[:, 2 * DN_QK + h * DN_DV:2 * DN_QK + (h + 1) * DN_DV]
            q = q * lax.rsqrt(jnp.sum(q * q, axis=-1, keepdims=True) + EPS) * (DN_DK ** -0.5)
            k = k * lax.rsqrt(jnp.sum(k * k, axis=-1, keepdims=True) + EPS)
            beta = beta_all[:, h:h + 1]
            eg = jnp.exp(g_all[:, DN_HEADS + h:DN_HEADS + h + 1])
            s = s_ref[j, h]
            kb = k * beta
            v_new = v * beta - _row_mm(kb * eg, s)
            o = _row_mm(q * eg, s) + jnp.sum(q * k, axis=-1, keepdims=True) * v_new
            sout_ref[j, h] = s * eg + _outer(k, v_new)
            z = z_ref[j, :, h * DN_DV:(h + 1) * DN_DV].astype(F32)
            o_ref[j, :, h * DN_DV:(h + 1) * DN_DV] = (_rms(o, nw_ref[...]) * _silu(z)).astype(BF16)


def _delta_step(proj, ba, conv_w, a_log, dt_bias, dn_norm_w, s0, c0):
    n = proj.shape[0]
    ns = math.gcd(n, SEQ_PER_STEP)
    full = lambda shape: pl.BlockSpec(shape, lambda b: (0,) * len(shape))
    state_spec = pl.BlockSpec((ns, DN_HEADS, DN_DK, DN_DV), lambda b: (b, 0, 0, 0))
    conv_spec = pl.BlockSpec((ns, DN_CONV - 1, DN_CONV_W), lambda b: (b, 0, 0))
    return pl.pallas_call(
        _delta_step_kernel,
        out_shape=(jax.ShapeDtypeStruct((n, 1, DN_VW), BF16),
                   jax.ShapeDtypeStruct((n, DN_HEADS, DN_DK, DN_DV), F32),
                   jax.ShapeDtypeStruct((n, DN_CONV - 1, DN_CONV_W), F32)),
        grid=(n // ns,),
        in_specs=[pl.BlockSpec((ns, 1, DN_CONV_W), lambda b: (b, 0, 0)),
                  pl.BlockSpec((ns, 1, DN_VW), lambda b: (b, 0, COL_Z_A)),
                  pl.BlockSpec((ns, 1, BA_W), lambda b: (b, 0, 0)),
                  full((DN_CONV, DN_CONV_W)), full((1, BA_W)), full((1, BA_W)), full((1, DN_DV)),
                  state_spec, conv_spec],
        out_specs=(pl.BlockSpec((ns, 1, DN_VW), lambda b: (b, 0, 0)), state_spec, conv_spec),
        compiler_params=_params("parallel"),
        name="delta_step",
    )(proj, proj, ba, conv_w, _lane_pad(a_log, DN_HEADS), _lane_pad(dt_bias, DN_HEADS),
      dn_norm_w.reshape(1, DN_DV), s0, c0)


def _ret_step_kernel(q_ref, k_ref, v_ref, g_ref, inv_ref, gw_ref, gb_ref, s_ref, o_ref, sout_ref):
    ang = float(PAST_LEN) * inv_ref[...]
    cos, sin = jnp.cos(ang), jnp.sin(ang)
    for j in range(q_ref.shape[0]):
        for h in range(RET_HEADS):
            gamma = math.exp(_log_gamma(h))
            q = _rotate(q_ref[j, :, h * RET_DK:(h + 1) * RET_DK].astype(F32), cos, sin)
            k = _rotate(k_ref[j, :, h * RET_DK:(h + 1) * RET_DK].astype(F32), cos, sin) * (RET_DK ** -0.5)
            v = v_ref[j, :, h * RET_DV:(h + 1) * RET_DV].astype(F32)
            s = s_ref[j, h]
            o = jnp.sum(q * k, axis=-1, keepdims=True) * v + _row_mm(q * gamma, s)
            sout_ref[j, h] = s * gamma + _outer(k, v)
            sl = slice(h * RET_DV, (h + 1) * RET_DV)
            o_ref[j, :, sl] = _group_norm_gate(o, gw_ref[:, sl], gb_ref[:, sl],
                                               g_ref[j, :, sl].astype(F32)).astype(BF16)


def _ret_step(proj, inv_freq, gn_w, gn_b, s0):
    n = proj.shape[0]
    ns = math.gcd(n, RET_SEQ_PER_STEP)
    full = lambda shape: pl.BlockSpec(shape, lambda b: (0,) * len(shape))
    state_spec = pl.BlockSpec((ns, RET_HEADS, RET_DK, RET_DV), lambda b: (b, 0, 0, 0))
    return pl.pallas_call(
        _ret_step_kernel,
        out_shape=(jax.ShapeDtypeStruct((n, 1, RET_VW), BF16),
                   jax.ShapeDtypeStruct((n, RET_HEADS, RET_DK, RET_DV), F32)),
        grid=(n // ns,),
        in_specs=[pl.BlockSpec((ns, 1, RET_QK), lambda b: (b, 0, COL_Q_B)),
                  pl.BlockSpec((ns, 1, RET_QK), lambda b: (b, 0, COL_K_B)),
                  pl.BlockSpec((ns, 1, RET_VW), lambda b: (b, 0, COL_V_B)),
                  pl.BlockSpec((ns, 1, RET_VW), lambda b: (b, 0, COL_G_B)),
                  full((1, RET_DK // 2)), full((1, RET_VW)), full((1, RET_VW)), state_spec],
        out_specs=(pl.BlockSpec((ns, 1, RET_VW), lambda b: (b, 0, 0)), state_spec),
        compiler_params=_params("parallel"),
        name="retention_step",
    )(proj, proj, proj, proj, inv_freq, gn_w.reshape(1, RET_VW), gn_b.reshape(1, RET_VW), s0)


def _post_mixer_kernel(oa_ref, ob_ref, ga_ref, gb_ref, x_ref, g1_ref, sh2_ref, sc2_ref, wda_ref, wdb_ref,
                       wout_ref, nw_ref, rw_ref, x1_ref, h_ref, hf_ref, logit_ref):
    y_a = _mm(oa_ref[...], wda_ref[...])
    y_b = _mm(ob_ref[...], wdb_ref[...])
    merged = _sigmoid(ga_ref[...].astype(F32)) * y_a + _sigmoid(gb_ref[...].astype(F32)) * y_b
    x1 = x_ref[...] + g1_ref[...] * _mm(merged, wout_ref[...])
    x1_ref[...] = x1
    hf = _rms(x1, nw_ref[...]) * (1.0 + sc2_ref[...]) + sh2_ref[...]
    h_ref[...] = hf.astype(BF16)
    hf_ref[...] = hf.reshape(hf_ref.shape)
    logit_ref[...] = _mm_hi(rw_ref[...], hf, NT)


def _post_mixer(o_a, o_b, proj, x, mod, per_token, w_down_a, w_down_b, w_out, norm_w, router_w, tm):
    nb, l, _ = x.shape
    tok = lambda w, j=0: pl.BlockSpec((None, tm, w), lambda b, i: (b, i, j))
    full = lambda shape: pl.BlockSpec(shape, lambda b, i: (0,) * len(shape))
    return pl.pallas_call(
        _post_mixer_kernel,
        out_shape=(jax.ShapeDtypeStruct((nb, l, D_MODEL), F32),
                   jax.ShapeDtypeStruct((nb, l, D_MODEL), BF16),
                   jax.ShapeDtypeStruct((nb, l) + TOKEN_TILE, F32),
                   jax.ShapeDtypeStruct((nb, N_EXPERTS, l), F32)),
        grid=(nb, l // tm),
        in_specs=[tok(DN_VW), tok(RET_VW), tok(D_MODEL, COL_GATE_A), tok(D_MODEL, COL_GATE_B), tok(D_MODEL),
                  _mod_spec(per_token, tm, 2), _mod_spec(per_token, tm, 3), _mod_spec(per_token, tm, 4),
                  full((DN_VW, D_MODEL)), full((RET_VW, D_MODEL)), full((D_MODEL, D_MODEL)),
                  full((1, D_MODEL)), full((N_EXPERTS, D_MODEL))],
        out_specs=(tok(D_MODEL), tok(D_MODEL),
                   pl.BlockSpec((None, tm) + TOKEN_TILE, lambda b, i: (b, i, 0, 0)),
                   pl.BlockSpec((None, N_EXPERTS, tm), lambda b, i: (b, 0, i))),
        compiler_params=_params("parallel", "parallel"),
        name="post_mixer",
    )(o_a, o_b, proj, proj, x, mod, mod, mod, w_down_a, w_down_b, w_out, norm_w, router_w)


def _first_max(x, axis, n):
    m = jnp.max(x, axis=axis, keepdims=True)
    ids = lax.broadcasted_iota(jnp.int32, x.shape, axis)
    first = jnp.min(jnp.where(x == m, ids, n), axis=axis, keepdims=True)
    return m, ids == first


def _eye(n):
    return (lax.broadcasted_iota(jnp.int32, (n, n), 0) == lax.broadcasted_iota(jnp.int32, (n, n), 1)).astype(F32)


class _Streams:
    def __init__(self, nb, l, ls, tm):
        self.tm, self.tps = tm, l // tm
        self.n_p, self.n_s = nb * (l // tm), ls // tm
        self.n = self.n_p + self.n_s

    def prompt(self, s):
        sp = jnp.minimum(s, self.n_p - 1)
        return sp // self.tps, sp % self.tps

    def sample(self, s, n_tiles=None):
        return jnp.clip(s - self.n_p, 0, (self.n_s if n_tiles is None else n_tiles) - 1)


def _route_kernel(n_prompt_tiles, lp_ref, ls_ref, bias_ref, ek_ref, rk_ref, wkt_ref, cnt_ref, run_scr):
    @pl.when(pl.program_id(0) == 0)
    def _():
        run_scr[...] = jnp.zeros_like(run_scr)

    t = lp_ref.shape[-1]
    scores = _sigmoid(jnp.where(pl.program_id(0) < n_prompt_tiles, lp_ref[...], ls_ref[...]))
    biased = scores + bias_ref[...]
    neg = jnp.float32(-jnp.inf)
    b3 = biased.reshape(N_GROUPS, GROUP_SIZE, t)
    m1, hit = _first_max(b3, 1, GROUP_SIZE)
    m2 = jnp.max(jnp.where(hit, neg, b3), axis=1, keepdims=True)
    gscore = (m1 + m2).reshape(N_GROUPS, t)
    gmask = jnp.zeros((N_GROUPS, t), jnp.bool_)
    for _ in range(TOPK_GROUPS):
        _, hit = _first_max(gscore, 0, N_GROUPS)
        gmask = jnp.logical_or(gmask, hit)
        gscore = jnp.where(hit, neg, gscore)
    emask = jnp.broadcast_to(gmask.reshape(N_GROUPS, 1, t), (N_GROUPS, GROUP_SIZE, t)).reshape(N_EXPERTS, t)
    masked = jnp.where(emask, biased, neg)
    eid = lax.broadcasted_iota(jnp.int32, (N_EXPERTS, t), 0).astype(F32)
    hits, e_rows, w_rows = [], [], []
    for _ in range(TOP_K):
        _, hit = _first_max(masked, 0, N_EXPERTS)
        hits.append(hit)
        e_rows.append(jnp.sum(jnp.where(hit, eid, 0.0), axis=0, keepdims=True))
        w_rows.append(jnp.sum(jnp.where(hit, scores, 0.0), axis=0, keepdims=True))
        masked = jnp.where(hit, neg, masked)
    sel = sum(hit.astype(F32) for hit in hits)
    before = lax.broadcasted_iota(jnp.int32, (t, t), 0) < lax.broadcasted_iota(jnp.int32, (t, t), 1)
    rank = run_scr[:, 0:1] + _mm(sel, before.astype(F32))
    rk_rows = [jnp.sum(jnp.where(hit, rank, 0.0), axis=0, keepdims=True) for hit in hits]
    run_scr[...] = run_scr[...] + jnp.sum(sel, axis=1, keepdims=True)
    wk = jnp.concatenate(w_rows, axis=0)
    wk = wk / jnp.sum(wk, axis=0, keepdims=True) * ROUTED_SCALE
    ek_ref[...] = jnp.concatenate(e_rows, axis=0).astype(jnp.int32)
    rk_ref[...] = jnp.concatenate(rk_rows, axis=0).astype(jnp.int32)
    wkt_ref[...] = _mm_hi(wk, _eye(TOP_K), TN)
    cnt_ref[...] = run_scr[...]


def _route(logits_p, logits_s, router_bias, st):
    t_all = st.n * st.tm
    pick = pl.BlockSpec((None, TOP_K, st.tm), lambda s: (0, 0, s))
    return pl.pallas_call(
        functools.partial(_route_kernel, st.n_p),
        out_shape=(jax.ShapeDtypeStruct((1, TOP_K, t_all), jnp.int32),
                   jax.ShapeDtypeStruct((1, TOP_K, t_all), jnp.int32),
                   jax.ShapeDtypeStruct((1, t_all, TOP_K), F32),
                   jax.ShapeDtypeStruct((N_EXPERTS, LANES), F32)),
        grid=(st.n,),
        in_specs=[pl.BlockSpec((None, N_EXPERTS, st.tm), lambda s: (st.prompt(s)[0], 0, st.prompt(s)[1])),
                  pl.BlockSpec((None, N_EXPERTS, st.tm), lambda s: (0, 0, st.sample(s))),
                  pl.BlockSpec((N_EXPERTS, 1), lambda s: (0, 0))],
        out_specs=(pick, pick, pl.BlockSpec((None, st.tm, TOP_K), lambda s: (0, s, 0)),
                   pl.BlockSpec((N_EXPERTS, LANES), lambda s: (0, 0))),
        scratch_shapes=[pltpu.VMEM((N_EXPERTS, LANES), F32)],
        compiler_params=_params("arbitrary"),
        name="route",
    )(logits_p, logits_s, router_bias.reshape(N_EXPERTS, 1))


def _plan_kernel(block, cnt_ref, ek_ref, rk_ref, pos_ref, be_ref, pe_ref, nu_ref, start_scr):
    @pl.when(jnp.logical_and(pl.program_id(0) == 0, pl.program_id(1) == 0))
    def _():
        padded = jnp.ceil(cnt_ref[...] * (1.0 / block)) * block
        r = lax.broadcasted_iota(jnp.int32, (N_EXPERTS, N_EXPERTS), 0)
        c = lax.broadcasted_iota(jnp.int32, (N_EXPERTS, N_EXPERTS), 1)
        pad_end = _mm_hi((r >= c).astype(F32), padded)
        start_scr[...] = pad_end - padded
        pe_ref[...] = pad_end.astype(jnp.int32)
        nu_ref[...] = (pad_end[N_EXPERTS - 1:, :] * (1.0 / block)).astype(jnp.int32)
        nbp = be_ref.shape[-1]
        first_row = lax.broadcasted_iota(jnp.int32, (1, nbp), 1).astype(F32) * block
        owner = jnp.sum((pad_end[:, 0:1] <= first_row).astype(F32), axis=0, keepdims=True)
        be_ref[...] = jnp.minimum(owner, N_EXPERTS - 1.0).astype(jnp.int32)

    t = ek_ref.shape[-1]
    eid = lax.broadcasted_iota(jnp.int32, (N_EXPERTS, t), 0)
    start = start_scr[:, 0:1]
    rows = [jnp.sum(jnp.where(eid == ek_ref[k:k + 1, :], start, 0.0), axis=0, keepdims=True)
            for k in range(TOP_K)]
    pos_ref[...] = jnp.concatenate(rows, axis=0).astype(jnp.int32) + rk_ref[...]


def _plan(counts, ek, rk, tr, n_blocks, block):
    nb, _, l = ek.shape
    nbp = -(-n_blocks // LANES) * LANES
    pick = pl.BlockSpec((None, TOP_K, tr), lambda b, i: (b, 0, i))
    const = lambda shape: pl.BlockSpec(shape, lambda b, i: (0, 0))
    return pl.pallas_call(
        functools.partial(_plan_kernel, block),
        out_shape=(jax.ShapeDtypeStruct((nb, TOP_K, l), jnp.int32),
                   jax.ShapeDtypeStruct((1, nbp), jnp.int32),
                   jax.ShapeDtypeStruct((N_EXPERTS, LANES), jnp.int32),
                   jax.ShapeDtypeStruct((1, LANES), jnp.int32)),
        grid=(nb, l // tr),
        in_specs=[const((N_EXPERTS, LANES)), pick, pick],
        out_specs=(pick, const((1, nbp)), const((N_EXPERTS, LANES)), const((1, LANES))),
        scratch_shapes=[pltpu.VMEM((N_EXPERTS, LANES), F32)],
        compiler_params=_params("arbitrary", "arbitrary"),
        name="plan",
    )(counts, ek, rk)


def _dispatch_kernel(n_prompt_tiles, pe_ref, pos_ref, xp_ref, xq_ref, xs_hbm, zero_scr, sem):
    s = pl.program_id(0)
    tm = pos_ref.shape[-1]

    @pl.when(s == 0)
    def _():
        zero_scr[...] = jnp.zeros_like(zero_scr)

        def tail_copy(e):
            block = zero_scr.shape[0]
            return pltpu.make_async_copy(zero_scr, xs_hbm.at[pl.ds(pe_ref[e] - block, block)], sem)

        def nonempty(e):
            return pe_ref[e] > jnp.where(e > 0, pe_ref[jnp.maximum(e - 1, 0)], 0)

        @pl.loop(0, N_EXPERTS)
        def _(e):
            @pl.when(nonempty(e))
            def _():
                tail_copy(e).start()

        @pl.loop(0, N_EXPERTS)
        def _(e):
            @pl.when(nonempty(e))
            def _():
                tail_copy(e).wait()

    def scatter(x_ref):
        def row(k, t):
            return pltpu.make_async_copy(x_ref.at[t], xs_hbm.at[pos_ref[k, t]], sem)

        @pl.loop(0, tm)
        def _(t):
            for k in range(TOP_K):
                row(k, t).start(priority=k % 2)

        @pl.loop(0, tm)
        def _(t):
            for k in range(TOP_K):
                row(k, t).wait()

    @pl.when(s < n_prompt_tiles)
    def _():
        scatter(xp_ref)

    @pl.when(s >= n_prompt_tiles)
    def _():
        scatter(xq_ref)


def _dispatch(x_p, x_s, pos, pad_end, n_rows, st, block):
    tile = (st.tm,) + TOKEN_TILE
    flat = lambda x: x.reshape((-1,) + TOKEN_TILE)
    return pl.pallas_call(
        functools.partial(_dispatch_kernel, st.n_p),
        out_shape=jax.ShapeDtypeStruct((n_rows,) + TOKEN_TILE, F32),
        grid_spec=pltpu.PrefetchScalarGridSpec(
            num_scalar_prefetch=1,
            grid=(st.n,),
            in_specs=[pl.BlockSpec((None, TOP_K, st.tm), lambda s, pe: (0, 0, s), memory_space=pltpu.SMEM),
                      pl.BlockSpec(tile, lambda s, pe: (jnp.minimum(s, st.n_p - 1), 0, 0)),
                      pl.BlockSpec(tile, lambda s, pe: (st.sample(s), 0, 0))],
            out_specs=pl.BlockSpec(memory_space=pl.ANY),
            scratch_shapes=[pltpu.VMEM((block,) + TOKEN_TILE, F32), pltpu.SemaphoreType.DMA(())]),
        compiler_params=_params("arbitrary"),
        name="dispatch",
    )(pad_end, pos, flat(x_p), flat(x_s))


def _experts_kernel(be_ref, nu_ref, x_ref, wg_ref, wu_ref, wd_ref, y_ref):
    @pl.when(pl.program_id(0) < nu_ref[0])
    def _():
        xb = x_ref[...].reshape(x_ref.shape[0], D_MODEL).astype(BF16)
        y = _mm(_silu(_mm(xb, wg_ref[...])) * _mm(xb, wu_ref[...]), wd_ref[...])
        y_ref[...] = y.reshape(y_ref.shape)


def _experts(xs, block_expert, n_used, w_gate, w_up, w_down, block):
    n_rows, d = xs.shape[0], D_MODEL
    used = lambda i, nu: jnp.minimum(i, nu[0] - 1)
    rows = pl.BlockSpec((block,) + TOKEN_TILE, lambda i, be, nu: (used(i, nu), 0, 0))
    return pl.pallas_call(
        _experts_kernel,
        out_shape=jax.ShapeDtypeStruct((n_rows,) + TOKEN_TILE, F32),
        grid_spec=pltpu.PrefetchScalarGridSpec(
            num_scalar_prefetch=2,
            grid=(n_rows // block,),
            in_specs=[rows,
                      pl.BlockSpec((None, d, D_EXPERT), lambda i, be, nu: (be[used(i, nu)], 0, 0)),
                      pl.BlockSpec((None, d, D_EXPERT), lambda i, be, nu: (be[used(i, nu)], 0, 0)),
                      pl.BlockSpec((None, D_EXPERT, d), lambda i, be, nu: (be[used(i, nu)], 0, 0))],
            out_specs=rows),
        compiler_params=_params("arbitrary"),
        name="experts",
    )(block_expert, n_used, xs, w_gate, w_up, w_down)


def _combine_kernel(n_prompt_tiles, pos_ref, pos_next_ref, wkt_ref, ys_hbm, hp_ref, hq_ref, x1p_ref, x1q_ref,
                    g2p_ref, g2q_ref, wsg_ref, wsu_ref, wsd_ref, fw_ref, yp_ref, yq_ref, rows_scr, sem):
    s = pl.program_id(0)
    tm = pos_ref.shape[-1]
    slot = lax.rem(s, 2)
    in_prompt = s < n_prompt_tiles

    def row(p_ref, sl, k, t):
        return pltpu.make_async_copy(ys_hbm.at[p_ref[k, t]], rows_scr.at[sl, k, t], sem.at[sl])

    def fetch(p_ref, sl):
        @pl.loop(0, tm)
        def _(t):
            for k in range(TOP_K):
                row(p_ref, sl, k, t).start(priority=k % 2)

    @pl.when(s == 0)
    def _():
        fetch(pos_ref, 0)

    @pl.when(s + 1 < pl.num_programs(0))
    def _():
        fetch(pos_next_ref, 1 - slot)

    hb = jnp.where(in_prompt, hp_ref[...], hq_ref[...])
    acc = _mm(_silu(_mm(hb, wsg_ref[...])) * _mm(hb, wsu_ref[...]), wsd_ref[...])

    @pl.loop(0, tm)
    def _(t):
        for k in range(TOP_K):
            row(pos_ref, slot, k, t).wait()

    wkt = wkt_ref[...]
    for k in range(TOP_K):
        acc = acc + rows_scr[slot, k].reshape(tm, D_MODEL) * wkt[:, k:k + 1]

    @pl.when(in_prompt)
    def _():
        yp_ref[...] = _rms(x1p_ref[...] + g2p_ref[...] * acc, fw_ref[...])

    @pl.when(jnp.logical_not(in_prompt))
    def _():
        yq_ref[...] = _rms(x1q_ref[...] + g2q_ref[...] * acc, fw_ref[...])


def _combine(ys, pos, wkt, h_p, h_s, x1_p, x1_s, mod_p, mod_s, ws_gate, ws_up, ws_down, final_w, st):
    nb, l, d = x1_p.shape
    real = x1_s.shape[1] // st.tm
    prompt = lambda w, col=0: pl.BlockSpec((None, st.tm, w), lambda s: (*st.prompt(s), col))
    sample = lambda w, n, col=0: pl.BlockSpec((None, st.tm, w), lambda s: (0, st.sample(s, n), col))
    full = lambda shape: pl.BlockSpec(shape, lambda s: (0,) * len(shape))
    pick = lambda step: pl.BlockSpec((None, TOP_K, st.tm), lambda s: (0, 0, step(s)), memory_space=pltpu.SMEM)
    return pl.pallas_call(
        functools.partial(_combine_kernel, st.n_p),
        out_shape=(jax.ShapeDtypeStruct((nb, l, d), F32), jax.ShapeDtypeStruct((1, st.n_s * st.tm, d), F32)),
        grid=(st.n,),
        in_specs=[pick(lambda s: s), pick(lambda s: jnp.minimum(s + 1, st.n - 1)),
                  pl.BlockSpec((None, st.tm, TOP_K), lambda s: (0, s, 0)), pl.BlockSpec(memory_space=pl.ANY),
                  prompt(d), sample(d, real), prompt(d), sample(d, real),
                  pl.BlockSpec((None, 1, d), lambda s: (st.prompt(s)[0], 0, 5)), sample(d, real, 5),
                  full((d, D_SHARED)), full((d, D_SHARED)), full((D_SHARED, d)), full((1, d))],
        out_specs=(prompt(d), sample(d, None)),
        scratch_shapes=[pltpu.VMEM((2, TOP_K, st.tm) + TOKEN_TILE, F32), pltpu.SemaphoreType.DMA((2,))],
        compiler_params=_params("arbitrary"),
        name="combine",
    )(pos, pos, wkt, ys, h_p, h_s, x1_p, x1_s, mod_p, mod_s, ws_gate, ws_up, ws_down, final_w)


def _moe(pre_p, pre_s, mod_p, mod_s, w):
    x1_p, h_p, hf3_p, logits_p = pre_p
    x1_s, h_s, hf3_s, logits_s = pre_s
    nb, l, _ = x1_p.shape
    ls = x1_s.shape[1]
    tr = min(l, 512)
    ls_pad = -(-ls // tr) * tr
    hf3_s = jnp.pad(hf3_s, ((0, 0), (0, ls_pad - ls), (0, 0), (0, 0)))
    logits_s = jnp.pad(logits_s, ((0, 0), (0, 0), (0, ls_pad - ls)))
    st = _Streams(nb, l, ls_pad, tr)
    st_c = _Streams(nb, l, ls_pad, min(l, ls, 128))
    n_tokens = st.n * st.tm
    n_blocks = n_tokens * TOP_K // MOE_BLOCK + N_EXPERTS
    ek, rk, wkt, counts = _route(logits_p, logits_s, w["router_bias"], st)
    pos, block_expert, pad_end, n_used = _plan(counts, ek, rk, tr, n_blocks, MOE_BLOCK)
    xs = _dispatch(hf3_p, hf3_s, pos, pad_end[:, 0], n_blocks * MOE_BLOCK, st, MOE_BLOCK)
    ys = _experts(xs, block_expert[0, :n_blocks], n_used[0, :1], w["w_gate"], w["w_up"], w["w_down"], MOE_BLOCK)
    y_p, y_s = _combine(ys, pos, wkt, h_p, h_s, x1_p, x1_s, mod_p, mod_s, w["ws_gate"], w["ws_up"], w["ws_down"],
                        w["final_norm"], st_c)
    return y_p, y_s[:, :ls]


def _trunk(x, mod, per_token, states, w):
    nb, l, _ = x.shape
    tm_proj, tn_proj, tm_post = min(l, 1024), 2048, min(l, 512)
    proj, ba, bat = _in_projection(x, mod, per_token, w["norm_mix"], w["w_main"], w["w_in"], tm_proj, tn_proj)
    if states is None:
        c0 = jnp.zeros((1, DN_CONV - 1, DN_CONV_W), F32)
        sd0 = jnp.zeros((1, DN_HEADS, DN_DK, DN_DV), F32)
        sr0 = jnp.zeros((1, RET_HEADS, RET_DK, RET_DV), F32)
        o_a, sd, cb = _delta_prefill(proj, bat, w["conv_w"], w["a_log"], w["dt_bias"], w["dn_norm"], sd0, c0)
        o_b, sr = _ret_prefill(proj, w["inv_freq"], w["gn_w"], w["gn_b"], sr0)
    else:
        c0, sd0, sr0 = states
        n = nb * l
        o_a, sd, cb = _delta_step(proj.reshape(n, 1, PROJ_W), ba.reshape(n, 1, BA_W), w["conv_w"], w["a_log"],
                                  w["dt_bias"], w["dn_norm"], sd0, c0)
        o_b, sr = _ret_step(proj.reshape(n, 1, PROJ_W), w["inv_freq"], w["gn_w"], w["gn_b"], sr0)
        o_a = o_a.reshape(nb, l, DN_VW)
        o_b = o_b.reshape(nb, l, RET_VW)
    pre_moe = _post_mixer(o_a, o_b, proj, x, mod, per_token, w["w_down_a"], w["w_down_b"], w["w_out"],
                          w["norm_ffn"], w["router_w"], tm_post)
    return pre_moe, cb, sd, sr


def kernel(x_prompt, x_sample, state_conv, state_delta, state_ret, c_prompt, c_sample, w_mod, b_mod, norm_mix_w, w_in, conv_w, a_log, dt_bias, dn_norm_w, ret_gn_w, ret_gn_b, w_down_a, w_down_b, w_out, norm_ffn_w, router_w, router_bias, w_gate, w_up, w_down, ws_gate, ws_up, ws_down, final_norm_w):
    bp, lp, _ = x_prompt.shape
    bs = x_sample.shape[0]
    half = RET_DK // 2
    w_in0 = w_in[0]
    c0, c1 = DN_CONV_W, DN_CONV_W + 2 * DN_HEADS
    w = {
        "norm_mix": norm_mix_w[0].reshape(1, D_MODEL),
        "w_main": jnp.concatenate([w_in0[:, :c0].astype(BF16), w_in0[:, c1:].astype(BF16)], axis=1),
        "w_in": w_in0,
        "conv_w": conv_w[0], "a_log": a_log[0], "dt_bias": dt_bias[0], "dn_norm": dn_norm_w[0],
        "inv_freq": (ROPE_BASE ** (-jnp.arange(half, dtype=F32) / half)).reshape(1, half),
        "gn_w": ret_gn_w[0], "gn_b": ret_gn_b[0],
        "w_down_a": w_down_a[0].astype(BF16), "w_down_b": w_down_b[0].astype(BF16), "w_out": w_out[0].astype(BF16),
        "norm_ffn": norm_ffn_w[0].reshape(1, D_MODEL), "router_w": router_w[0], "router_bias": router_bias[0],
        "w_gate": w_gate[0], "w_up": w_up[0], "w_down": w_down[0],
        "ws_gate": ws_gate[0].astype(BF16), "ws_up": ws_up[0].astype(BF16), "ws_down": ws_down[0].astype(BF16),
        "final_norm": final_norm_w.reshape(1, D_MODEL),
    }
    mod = _modulation(jnp.concatenate([c_prompt, c_sample], axis=0), w_mod[0], b_mod[0])
    mod_p = mod[:bp].reshape(bp, 1, MOD_CHUNKS * D_MODEL)
    mod_s = mod[bp:].reshape(1, bs, MOD_CHUNKS * D_MODEL)

    pre_p, conv_p, delta_p, ret_p = _trunk(x_prompt, mod_p, False, None, w)
    pre_s, conv_s, delta_s, ret_s = _trunk(x_sample.reshape(1, bs, D_MODEL), mod_s, True,
                                           (state_conv[0], state_delta[0], state_ret[0]), w)
    y_p, y_s = _moe(pre_p, pre_s, mod_p, mod_s, w)
    return (y_p, y_s.reshape(bs, 1, D_MODEL), conv_p[None], delta_p[None], ret_p[None],
            conv_s[None], delta_s[None], ret_s[None])
```

```python
import functools
import math

import jax
import jax.numpy as jnp
from jax import lax
from jax.experimental import pallas as pl
from jax.experimental.pallas import tpu as pltpu

F32 = jnp.float32
BF16 = jnp.bfloat16

D_MODEL = 1024
DN_HEADS, DN_DK, DN_DV, DN_CONV = 8, 128, 128, 4
DN_QK = DN_HEADS * DN_DK
DN_VW = DN_HEADS * DN_DV
DN_CONV_W = 2 * DN_QK + DN_VW
RET_HEADS, RET_DK, RET_DV = 4, 256, 512
RET_QK = RET_HEADS * RET_DK
RET_VW = RET_HEADS * RET_DV
ROPE_BASE = 10000.0
PAST_LEN = 16384
N_EXPERTS, TOP_K, N_GROUPS, TOPK_GROUPS = 64, 8, 8, 4
GROUP_SIZE = N_EXPERTS // N_GROUPS
D_EXPERT, D_SHARED = 256, 256
ROUTED_SCALE = 2.5
MOD_CHUNKS = 6
EPS = 1e-6

PROJ_W = 12 * D_MODEL
COL_Q_A, COL_K_A, COL_V_A, COL_Z_A = 0, 1, 2, 3
COL_Q_B, COL_K_B = 4, 5
COL_V_B, COL_G_B = 3, 4
COL_GATE_A, COL_GATE_B = 10, 11
BA_W = 128
CHUNK = 128
CONV_HALO = 8
VMEM_LIMIT = 56 * 1024 * 1024
LANES = 128
MOE_BLOCK = 512
U32 = jnp.uint32
HALF_D = D_MODEL // 2
TOKEN_TILE = (HALF_D // LANES, LANES)

NN = (((1,), (0,)), ((), ()))
NT = (((1,), (1,)), ((), ()))
TN = (((0,), (0,)), ((), ()))


def _mm(a, b, dims=NN):
    return lax.dot_general(a.astype(BF16), b.astype(BF16), dims, preferred_element_type=F32)


def _mm_hi(a, b, dims=NN):
    return lax.dot_general(a.astype(F32), b.astype(F32), dims, precision=lax.Precision.HIGHEST,
                           preferred_element_type=F32)


def _hi_lo(x):
    hi = x.astype(BF16)
    return hi, (x - hi.astype(F32)).astype(BF16)


def _split_lhs(a):
    hi, lo = _hi_lo(a)
    return jnp.concatenate([hi, lo, hi], axis=1)


def _split_rhs(b):
    hi, lo = _hi_lo(b)
    return jnp.concatenate([hi, hi, lo], axis=0)


def _mm_split(a3, b3):
    return lax.dot_general(a3, b3, NN, preferred_element_type=F32)


def _sigmoid(x):
    return jax.nn.sigmoid(x)


def _silu(x):
    return x * _sigmoid(x)


def _softplus(x):
    return jnp.maximum(x, 0.0) + jnp.log1p(jnp.exp(-jnp.abs(x)))


def _rms(x, w):
    return x * lax.rsqrt(jnp.mean(x * x, axis=-1, keepdims=True) + EPS) * w


def _pack_rows(x):
    lo = lax.bitcast_convert_type(x[:, :HALF_D].astype(BF16).astype(F32), U32) >> U32(16)
    hi = lax.bitcast_convert_type(x[:, HALF_D:].astype(BF16).astype(F32), U32) & U32(0xFFFF0000)
    return (lo | hi).reshape((x.shape[0],) + TOKEN_TILE)


def _unpack_rows(w):
    w = w.reshape(w.shape[0], HALF_D)
    return jnp.concatenate([lax.bitcast_convert_type(w << U32(16), F32),
                            lax.bitcast_convert_type(w & U32(0xFFFF0000), F32)], axis=1)


def _params(*sem):
    return pltpu.CompilerParams(dimension_semantics=sem, vmem_limit_bytes=VMEM_LIMIT)


def _mod_kernel(c_ref, w_ref, b_ref, o_ref):
    o_ref[...] = _mm_hi(_silu(c_ref[...]), w_ref[...]) + b_ref[...]


def _modulation(c, w_mod, b_mod):
    n = c.shape[0]
    tn = 1536
    return pl.pallas_call(
        _mod_kernel,
        out_shape=jax.ShapeDtypeStruct((n, MOD_CHUNKS * D_MODEL), F32),
        grid=(MOD_CHUNKS * D_MODEL // tn,),
        in_specs=[pl.BlockSpec((n, D_MODEL), lambda j: (0, 0)),
                  pl.BlockSpec((D_MODEL, tn), lambda j: (0, j)),
                  pl.BlockSpec((1, tn), lambda j: (0, j))],
        out_specs=pl.BlockSpec((n, tn), lambda j: (0, j)),
        compiler_params=_params("parallel"),
        name="modulation",
    )(c, w_mod, b_mod.reshape(1, -1))


def _mod_spec(per_token, tm, col):
    if per_token:
        return pl.BlockSpec((None, tm, D_MODEL), lambda b, i, *_: (b, i, col))
    return pl.BlockSpec((None, 1, D_MODEL), lambda b, i, *_: (b, 0, col))


def _inproj_kernel(x_ref, sh_ref, sc_ref, nw_ref, w_ref, wba_ref, out_ref, ba_ref, bat_ref, h_scr):
    @pl.when(pl.program_id(2) == 0)
    def _():
        h = _rms(x_ref[...], nw_ref[...]) * (1.0 + sc_ref[...]) + sh_ref[...]
        hb = h.astype(BF16)
        h_scr[...] = hb
        wb = wba_ref[...]
        ba_ref[...] = _mm(hb, wb)
        bat_ref[...] = _mm(wb.T, hb, NT)[0:2 * DN_HEADS]

    out_ref[...] = _mm(h_scr[...], w_ref[...]).astype(BF16)


def _in_projection(x, mod, per_token, norm_w, w_main, w_in, tm, tn):
    nb, l, _ = x.shape
    assert DN_CONV_W % BA_W == 0
    return pl.pallas_call(
        _inproj_kernel,
        out_shape=(jax.ShapeDtypeStruct((nb, l, PROJ_W), BF16),
                   jax.ShapeDtypeStruct((nb, l, BA_W), F32),
                   jax.ShapeDtypeStruct((nb, 2 * DN_HEADS, l), F32)),
        grid=(nb, l // tm, PROJ_W // tn),
        in_specs=[pl.BlockSpec((None, tm, D_MODEL), lambda b, i, j: (b, i, 0)),
                  _mod_spec(per_token, tm, 0),
                  _mod_spec(per_token, tm, 1),
                  pl.BlockSpec((1, D_MODEL), lambda b, i, j: (0, 0)),
                  pl.BlockSpec((D_MODEL, tn), lambda b, i, j: (0, j)),
                  pl.BlockSpec((D_MODEL, BA_W), lambda b, i, j: (0, DN_CONV_W // BA_W))],
        out_specs=(pl.BlockSpec((None, tm, tn), lambda b, i, j: (b, i, j)),
                   pl.BlockSpec((None, tm, BA_W), lambda b, i, j: (b, i, 0)),
                   pl.BlockSpec((None, 2 * DN_HEADS, tm), lambda b, i, j: (b, 0, i))),
        scratch_shapes=[pltpu.VMEM((tm, D_MODEL), BF16)],
        compiler_params=_params("parallel", "parallel", "arbitrary"),
        name="in_projection",
    )(x, mod, mod, norm_w, w_main, w_in)


def _delta_prefill_kernel(q_ref, k_ref, v_ref, z_ref, bat_ref, cw_ref, alog_c_ref, dtb_c_ref, nw_ref, s0_ref, c0_ref,
                          o_ref, sout_ref, cout_ref,
                          ext_scr, s_scr, qkv_scr, pow_scr, inv_scr, uw_scr, phi_scr, plo_scr):
    step = pl.program_id(1)
    c = CHUNK
    lo = CONV_HALO - (DN_CONV - 1)

    @pl.when(step == 0)
    def _():
        s_scr[...] = s0_ref[...]
        ext_scr[lo:CONV_HALO, :] = c0_ref[...]

    ext_scr[CONV_HALO:, 0:DN_QK] = q_ref[...].astype(F32)
    ext_scr[CONV_HALO:, DN_QK:2 * DN_QK] = k_ref[...].astype(F32)
    ext_scr[CONV_HALO:, 2 * DN_QK:] = v_ref[...].astype(F32)

    row = lax.broadcasted_iota(jnp.int32, (c, c), 0)
    col = lax.broadcasted_iota(jnp.int32, (c, c), 1)
    causal = row >= col
    strict = row > col
    eye = (row == col).astype(F32)
    utri = (row <= col).astype(F32)

    bat = bat_ref[...]
    g_t = -jnp.exp(alog_c_ref[...]) * _softplus(bat[DN_HEADS:, :] + dtb_c_ref[...])
    gc_t = _mm_hi(g_t, utri)
    gates = _mm_hi(jnp.concatenate([_sigmoid(bat[:DN_HEADS, :]), gc_t], axis=0), _eye(2 * DN_HEADS), TN)
    beta_tok, gc_tok = gates[:, :DN_HEADS], gates

    def conv(c0):
        acc = ext_scr[lo:lo + c, c0:c0 + DN_DK] * cw_ref[0:1, c0:c0 + DN_DK]
        for i in range(1, DN_CONV):
            acc = acc + ext_scr[lo + i:lo + i + c, c0:c0 + DN_DK] * cw_ref[i:i + 1, c0:c0 + DN_DK]
        return _silu(acc)

    def decay_of(h):
        gc = gc_tok[:, DN_HEADS + h:DN_HEADS + h + 1]
        return gc, jnp.where(causal, jnp.exp(jnp.where(causal, gc - gc_t[h:h + 1, :], 0.0)), 0.0)

    for h in range(DN_HEADS):
        q = conv(h * DN_DK)
        k = conv(DN_QK + h * DN_DK)
        q = q * lax.rsqrt(jnp.sum(q * q, axis=-1, keepdims=True) + EPS) * (DN_DK ** -0.5)
        k = k * lax.rsqrt(jnp.sum(k * k, axis=-1, keepdims=True) + EPS)
        qkv_scr[:, h * DN_DK:(h + 1) * DN_DK] = q
        qkv_scr[:, DN_QK + h * DN_DK:DN_QK + (h + 1) * DN_DK] = k
        qkv_scr[:, 2 * DN_QK + h * DN_DV:2 * DN_QK + (h + 1) * DN_DV] = conv(2 * DN_QK + h * DN_DV)
        _, decay = decay_of(h)
        a_mat = jnp.where(strict, _mm(k * beta_tok[:, h:h + 1], k, NT) * decay, 0.0)
        phi_scr[h], plo_scr[h] = _hi_lo(a_mat)
        inv_scr[h] = eye - a_mat

    for _ in range(int(math.log2(c)) - 1):
        for h in range(DN_HEADS):
            p_hi, p_lo = phi_scr[h], plo_scr[h]
            phi_scr[h], plo_scr[h] = _hi_lo(_mm_split(jnp.concatenate([p_hi, p_lo, p_hi], axis=1),
                                                      jnp.concatenate([p_hi, p_hi, p_lo], axis=0)))
        for h in range(DN_HEADS):
            inv = inv_scr[h]
            p_hi, p_lo = phi_scr[h], plo_scr[h]
            inv_scr[h] = inv + _mm_split(_split_lhs(inv), jnp.concatenate([p_hi, p_hi, p_lo], axis=0))

    def qkv_of(h):
        return (qkv_scr[:, h * DN_DK:(h + 1) * DN_DK], qkv_scr[:, DN_QK + h * DN_DK:DN_QK + (h + 1) * DN_DK],
                qkv_scr[:, 2 * DN_QK + h * DN_DV:2 * DN_QK + (h + 1) * DN_DV])

    for h in range(DN_HEADS):
        q, k, v = qkv_of(h)
        beta = beta_tok[:, h:h + 1]
        gc, decay = decay_of(h)
        rhs = jnp.concatenate([v * beta, k * beta * jnp.exp(gc)], axis=-1)
        uw_scr[h] = _mm_split(_split_lhs(inv_scr[h]), _split_rhs(rhs))
        pow_scr[h] = jnp.where(causal, _mm(q, k, NT) * decay, 0.0)
    for h in range(DN_HEADS):
        inv_scr[h] = uw_scr[h, :, 0:DN_DV] - _mm(uw_scr[h, :, DN_DV:], s_scr[h])
    for h in range(DN_HEADS):
        q, k, _ = qkv_of(h)
        gc = gc_tok[:, DN_HEADS + h:DN_HEADS + h + 1]
        g_last = gc[c - 1:c, :]
        s = s_scr[h]
        v_new = inv_scr[h]
        o = _mm(q * jnp.exp(gc), s) + _mm(pow_scr[h], v_new)
        s_scr[h] = s * jnp.exp(g_last) + _mm(k * jnp.exp(g_last - gc), v_new, TN)
        z = z_ref[:, h * DN_DV:(h + 1) * DN_DV].astype(F32)
        o_ref[:, h * DN_DV:(h + 1) * DN_DV] = (_rms(o, nw_ref[...]) * _silu(z)).astype(BF16)

    ext_scr[0:CONV_HALO, :] = ext_scr[c:c + CONV_HALO, :]

    @pl.when(step == pl.num_programs(1) - 1)
    def _():
        sout_ref[...] = s_scr[...]
        cout_ref[...] = ext_scr[lo:CONV_HALO, :]


def _lane_pad(v, offset):
    return jnp.zeros((1, BA_W), F32).at[0, offset:offset + v.shape[0]].set(v.astype(F32))


def _delta_prefill(proj, bat, conv_w, a_log, dt_bias, dn_norm_w, s0, c0):
    nb, l, _ = proj.shape
    c = CHUNK
    col_spec = lambda j: pl.BlockSpec((None, c, D_MODEL), lambda b, i: (b, i, j))
    full = lambda shape: pl.BlockSpec(shape, lambda b, i: (0,) * len(shape))
    return pl.pallas_call(
        _delta_prefill_kernel,
        out_shape=(jax.ShapeDtypeStruct((nb, l, DN_VW), BF16),
                   jax.ShapeDtypeStruct((nb, DN_HEADS, DN_DK, DN_DV), F32),
                   jax.ShapeDtypeStruct((nb, DN_CONV - 1, DN_CONV_W), F32)),
        grid=(nb, l // c),
        in_specs=[col_spec(COL_Q_A), col_spec(COL_K_A), col_spec(COL_V_A), col_spec(COL_Z_A),
                  pl.BlockSpec((None, 2 * DN_HEADS, c), lambda b, i: (b, 0, i)),
                  full((DN_CONV, DN_CONV_W)), full((DN_HEADS, 1)), full((DN_HEADS, 1)), full((1, DN_DV)),
                  pl.BlockSpec((None, DN_HEADS, DN_DK, DN_DV), lambda b, i: (b % s0.shape[0], 0, 0, 0)),
                  pl.BlockSpec((None, DN_CONV - 1, DN_CONV_W), lambda b, i: (b % c0.shape[0], 0, 0))],
        out_specs=(pl.BlockSpec((None, c, DN_VW), lambda b, i: (b, i, 0)),
                   pl.BlockSpec((None, DN_HEADS, DN_DK, DN_DV), lambda b, i: (b, 0, 0, 0)),
                   pl.BlockSpec((None, DN_CONV - 1, DN_CONV_W), lambda b, i: (b, 0, 0))),
        scratch_shapes=[pltpu.VMEM((c + CONV_HALO, DN_CONV_W), F32),
                        pltpu.VMEM((DN_HEADS, DN_DK, DN_DV), F32),
                        pltpu.VMEM((c, DN_CONV_W), F32),
                        pltpu.VMEM((DN_HEADS, c, c), F32),
                        pltpu.VMEM((DN_HEADS, c, c), F32),
                        pltpu.VMEM((DN_HEADS, c, DN_DV + DN_DK), F32),
                        pltpu.VMEM((DN_HEADS, c, c), BF16),
                        pltpu.VMEM((DN_HEADS, c, c), BF16)],
        compiler_params=_params("parallel", "arbitrary"),
        name="delta_prefill",
    )(proj, proj, proj, proj, bat, conv_w, a_log.reshape(DN_HEADS, 1), dt_bias.reshape(DN_HEADS, 1),
      dn_norm_w.reshape(1, DN_DV), s0, c0)


def _log_gamma(h):
    return math.log(1.0 - 2.0 ** (-5.0 - h))


def _rotate(x, cos, sin):
    half = x.shape[-1] // 2
    x1, x2 = x[:, :half], x[:, half:]
    return jnp.concatenate([x1 * cos - x2 * sin, x1 * sin + x2 * cos], axis=-1)


def _group_norm_gate(o, gw, gb, gate):
    mu = jnp.mean(o, axis=-1, keepdims=True)
    var = jnp.mean(jnp.square(o - mu), axis=-1, keepdims=True)
    return ((o - mu) * lax.rsqrt(var + EPS) * gw + gb) * _silu(gate)


def _ret_prefill_kernel(q_ref, k_ref, v_ref, g_ref, inv_ref, gw_ref, gb_ref, s0_ref, o_ref, sout_ref, s_scr):
    step = pl.program_id(1)
    c = CHUNK

    @pl.when(step == 0)
    def _():
        s_scr[...] = s0_ref[...]

    idx_c = lax.broadcasted_iota(jnp.int32, (c, 1), 0)
    pos = (step * c + idx_c).astype(F32)
    ang = pos * inv_ref[...]
    cos, sin = jnp.cos(ang), jnp.sin(ang)
    rel = (lax.broadcasted_iota(jnp.int32, (c, c), 0) - lax.broadcasted_iota(jnp.int32, (c, c), 1)).astype(F32)
    idx = idx_c.astype(F32)

    for h in range(RET_HEADS):
        lg = _log_gamma(h)
        q = _rotate(q_ref[:, h * RET_DK:(h + 1) * RET_DK].astype(F32), cos, sin)
        k = _rotate(k_ref[:, h * RET_DK:(h + 1) * RET_DK].astype(F32), cos, sin) * (RET_DK ** -0.5)
        v = v_ref[:, h * RET_DV:(h + 1) * RET_DV]
        d_mat = jnp.where(rel >= 0, jnp.exp(jnp.maximum(rel, 0.0) * lg), 0.0)
        intra = _mm(_mm(q, k, NT) * d_mat, v)
        s = s_scr[h]
        cross = _mm(q * jnp.exp((idx + 1.0) * lg), s)
        s_scr[h] = s * math.exp(c * lg) + _mm(k * jnp.exp((c - 1.0 - idx) * lg), v, TN)
        sl = slice(h * RET_DV, (h + 1) * RET_DV)
        o_ref[:, sl] = _group_norm_gate(intra + cross, gw_ref[:, sl], gb_ref[:, sl],
                                        g_ref[:, sl].astype(F32)).astype(BF16)

    @pl.when(step == pl.num_programs(1) - 1)
    def _():
        sout_ref[...] = s_scr[...]


def _ret_prefill(proj, inv_freq, gn_w, gn_b, s0):
    nb, l, _ = proj.shape
    c = CHUNK
    full = lambda shape: pl.BlockSpec(shape, lambda b, i: (0,) * len(shape))
    state_spec = pl.BlockSpec((None, RET_HEADS, RET_DK, RET_DV), lambda b, i: (b, 0, 0, 0))
    return pl.pallas_call(
        _ret_prefill_kernel,
        out_shape=(jax.ShapeDtypeStruct((nb, l, RET_VW), BF16),
                   jax.ShapeDtypeStruct((nb, RET_HEADS, RET_DK, RET_DV), F32)),
        grid=(nb, l // c),
        in_specs=[pl.BlockSpec((None, c, RET_QK), lambda b, i: (b, i, COL_Q_B)),
                  pl.BlockSpec((None, c, RET_QK), lambda b, i: (b, i, COL_K_B)),
                  pl.BlockSpec((None, c, RET_VW), lambda b, i: (b, i, COL_V_B)),
                  pl.BlockSpec((None, c, RET_VW), lambda b, i: (b, i, COL_G_B)),
                  full((1, RET_DK // 2)), full((1, RET_VW)), full((1, RET_VW)),
                  pl.BlockSpec((None, RET_HEADS, RET_DK, RET_DV), lambda b, i: (b % s0.shape[0], 0, 0, 0))],
        out_specs=(pl.BlockSpec((None, c, RET_VW), lambda b, i: (b, i, 0)), state_spec),
        scratch_shapes=[pltpu.VMEM((RET_HEADS, RET_DK, RET_DV), F32)],
        compiler_params=_params("parallel", "arbitrary"),
        name="retention_prefill",
    )(proj, proj, proj, proj, inv_freq, gn_w.reshape(1, RET_VW), gn_b.reshape(1, RET_VW), s0)


ROWS = 8
SEQ_PER_STEP = 4
RET_SEQ_PER_STEP = 4


def _row_mm(a, s):
    return _mm(jnp.broadcast_to(a, (ROWS, a.shape[-1])), s)[0:1, :]


def _outer(a, b):
    first = lax.broadcasted_iota(jnp.int32, (ROWS, a.shape[-1]), 0) == 0
    a8 = jnp.where(first, jnp.broadcast_to(a, (ROWS, a.shape[-1])), 0.0)
    return _mm(a8, jnp.broadcast_to(b, (ROWS, b.shape[-1])), TN)


def _delta_step_kernel(qkv_ref, z_ref, ba_ref, cw_ref, alog_ref, dtb_ref, nw_ref, s_ref, c_ref,
                       o_ref, sout_ref, cout_ref):
    for j in range(qkv_ref.shape[0]):
        u_new = qkv_ref[j].astype(F32)
        buf = c_ref[j]
        acc = u_new * cw_ref[DN_CONV - 1:DN_CONV, :]
        for i in range(DN_CONV - 1):
            acc = acc + buf[i:i + 1, :] * cw_ref[i:i + 1, :]
        qkv = _silu(acc)
        cout_ref[j, 0:DN_CONV - 2, :] = buf[1:, :]
        cout_ref[j, DN_CONV - 2:, :] = u_new
        ba = ba_ref[j]
        beta_all = _sigmoid(ba)
        g_all = -jnp.exp(alog_ref[...]) * _softplus(ba + dtb_ref[...])
        for h in range(DN_HEADS):
            q = qkv[:, h * DN_DK:(h + 1) * DN_DK]
            k = qkv[:, DN_QK + h * DN_DK:DN_QK + (h + 1) * DN_DK]
            v = qkv[:, 2 * DN_QK + h * DN_DV:2 * DN_QK + (h + 1) * DN_DV]
            q = q * lax.rsqrt(jnp.sum(q * q, axis=-1, keepdims=True) + EPS) * (DN_DK ** -0.5)
            k = k * lax.rsqrt(jnp.sum(k * k, axis=-1, keepdims=True) + EPS)
            beta = beta_all[:, h:h + 1]
            eg = jnp.exp(g_all[:, DN_HEADS + h:DN_HEADS + h + 1])
            s = s_ref[j, h]
            kb = k * beta
            v_new = v * beta - _row_mm(kb * eg, s)
            o = _row_mm(q * eg, s) + jnp.sum(q * k, axis=-1, keepdims=True) * v_new
            sout_ref[j, h] = s * eg + _outer(k, v_new)
            z = z_ref[j, :, h * DN_DV:(h + 1) * DN_DV].astype(F32)
            o_ref[j, :, h * DN_DV:(h + 1) * DN_DV] = (_rms(o, nw_ref[...]) * _silu(z)).astype(BF16)


def _delta_step(proj, ba, conv_w, a_log, dt_bias, dn_norm_w, s0, c0):
    n = proj.shape[0]
    ns = math.gcd(n, SEQ_PER_STEP)
    full = lambda shape: pl.BlockSpec(shape, lambda b: (0,) * len(shape))
    state_spec = pl.BlockSpec((ns, DN_HEADS, DN_DK, DN_DV), lambda b: (b, 0, 0, 0))
    conv_spec = pl.BlockSpec((ns, DN_CONV - 1, DN_CONV_W), lambda b: (b, 0, 0))
    return pl.pallas_call(
        _delta_step_kernel,
        out_shape=(jax.ShapeDtypeStruct((n, 1, DN_VW), BF16),
                   jax.ShapeDtypeStruct((n, DN_HEADS, DN_DK, DN_DV), F32),
                   jax.ShapeDtypeStruct((n, DN_CONV - 1, DN_CONV_W), F32)),
        grid=(n // ns,),
        in_specs=[pl.BlockSpec((ns, 1, DN_CONV_W), lambda b: (b, 0, 0)),
                  pl.BlockSpec((ns, 1, DN_VW), lambda b: (b, 0, COL_Z_A)),
                  pl.BlockSpec((ns, 1, BA_W), lambda b: (b, 0, 0)),
                  full((DN_CONV, DN_CONV_W)), full((1, BA_W)), full((1, BA_W)), full((1, DN_DV)),
                  state_spec, conv_spec],
        out_specs=(pl.BlockSpec((ns, 1, DN_VW), lambda b: (b, 0, 0)), state_spec, conv_spec),
        compiler_params=_params("parallel"),
        name="delta_step",
    )(proj, proj, ba, conv_w, _lane_pad(a_log, DN_HEADS), _lane_pad(dt_bias, DN_HEADS),
      dn_norm_w.reshape(1, DN_DV), s0, c0)


def _ret_step_kernel(q_ref, k_ref, v_ref, g_ref, inv_ref, gw_ref, gb_ref, s_ref, o_ref, sout_ref):
    ang = float(PAST_LEN) * inv_ref[...]
    cos, sin = jnp.cos(ang), jnp.sin(ang)
    for j in range(q_ref.shape[0]):
        for h in range(RET_HEADS):
            gamma = math.exp(_log_gamma(h))
            q = _rotate(q_ref[j, :, h * RET_DK:(h + 1) * RET_DK].astype(F32), cos, sin)
            k = _rotate(k_ref[j, :, h * RET_DK:(h + 1) * RET_DK].astype(F32), cos, sin) * (RET_DK ** -0.5)
            v = v_ref[j, :, h * RET_DV:(h + 1) * RET_DV].astype(F32)
            s = s_ref[j, h]
            o = jnp.sum(q * k, axis=-1, keepdims=True) * v + _row_mm(q * gamma, s)
            sout_ref[j, h] = s * gamma + _outer(k, v)
            sl = slice(h * RET_DV, (h + 1) * RET_DV)
            o_ref[j, :, sl] = _group_norm_gate(o, gw_ref[:, sl], gb_ref[:, sl],
                                               g_ref[j, :, sl].astype(F32)).astype(BF16)


def _ret_step(proj, inv_freq, gn_w, gn_b, s0):
    n = proj.shape[0]
    ns = math.gcd(n, RET_SEQ_PER_STEP)
    full = lambda shape: pl.BlockSpec(shape, lambda b: (0,) * len(shape))
    state_spec = pl.BlockSpec((ns, RET_HEADS, RET_DK, RET_DV), lambda b: (b, 0, 0, 0))
    return pl.pallas_call(
        _ret_step_kernel,
        out_shape=(jax.ShapeDtypeStruct((n, 1, RET_VW), BF16),
                   jax.ShapeDtypeStruct((n, RET_HEADS, RET_DK, RET_DV), F32)),
        grid=(n // ns,),
        in_specs=[pl.BlockSpec((ns, 1, RET_QK), lambda b: (b, 0, COL_Q_B)),
                  pl.BlockSpec((ns, 1, RET_QK), lambda b: (b, 0, COL_K_B)),
                  pl.BlockSpec((ns, 1, RET_VW), lambda b: (b, 0, COL_V_B)),
                  pl.BlockSpec((ns, 1, RET_VW), lambda b: (b, 0, COL_G_B)),
                  full((1, RET_DK // 2)), full((1, RET_VW)), full((1, RET_VW)), state_spec],
        out_specs=(pl.BlockSpec((ns, 1, RET_VW), lambda b: (b, 0, 0)), state_spec),
        compiler_params=_params("parallel"),
        name="retention_step",
    )(proj, proj, proj, proj, inv_freq, gn_w.reshape(1, RET_VW), gn_b.reshape(1, RET_VW), s0)


def _post_mixer_kernel(oa_ref, ob_ref, ga_ref, gb_ref, x_ref, g1_ref, sh2_ref, sc2_ref, wda_ref, wdb_ref,
                       wout_ref, nw_ref, rw_ref, x1_ref, h_ref, hf_ref, logit_ref):
    y_a = _mm(oa_ref[...], wda_ref[...])
    y_b = _mm(ob_ref[...], wdb_ref[...])
    merged = _sigmoid(ga_ref[...].astype(F32)) * y_a + _sigmoid(gb_ref[...].astype(F32)) * y_b
    x1 = x_ref[...] + g1_ref[...] * _mm(merged, wout_ref[...])
    x1_ref[...] = x1
    hf = _rms(x1, nw_ref[...]) * (1.0 + sc2_ref[...]) + sh2_ref[...]
    h_ref[...] = hf.astype(BF16)
    hf_ref[...] = _pack_rows(hf)
    logit_ref[...] = _mm_hi(rw_ref[...], hf, NT)


def _post_mixer(o_a, o_b, proj, x, mod, per_token, w_down_a, w_down_b, w_out, norm_w, router_w, tm):
    nb, l, _ = x.shape
    tok = lambda w, j=0: pl.BlockSpec((None, tm, w), lambda b, i: (b, i, j))
    full = lambda shape: pl.BlockSpec(shape, lambda b, i: (0,) * len(shape))
    return pl.pallas_call(
        _post_mixer_kernel,
        out_shape=(jax.ShapeDtypeStruct((nb, l, D_MODEL), F32),
                   jax.ShapeDtypeStruct((nb, l, D_MODEL), BF16),
                   jax.ShapeDtypeStruct((nb, l) + TOKEN_TILE, U32),
                   jax.ShapeDtypeStruct((nb, N_EXPERTS, l), F32)),
        grid=(nb, l // tm),
        in_specs=[tok(DN_VW), tok(RET_VW), tok(D_MODEL, COL_GATE_A), tok(D_MODEL, COL_GATE_B), tok(D_MODEL),
                  _mod_spec(per_token, tm, 2), _mod_spec(per_token, tm, 3), _mod_spec(per_token, tm, 4),
                  full((DN_VW, D_MODEL)), full((RET_VW, D_MODEL)), full((D_MODEL, D_MODEL)),
                  full((1, D_MODEL)), full((N_EXPERTS, D_MODEL))],
        out_specs=(tok(D_MODEL), tok(D_MODEL),
                   pl.BlockSpec((None, tm) + TOKEN_TILE, lambda b, i: (b, i, 0, 0)),
                   pl.BlockSpec((None, N_EXPERTS, tm), lambda b, i: (b, 0, i))),
        compiler_params=_params("parallel", "parallel"),
        name="post_mixer",
    )(o_a, o_b, proj, proj, x, mod, mod, mod, w_down_a, w_down_b, w_out, norm_w, router_w)


def _first_max(x, axis, n):
    m = jnp.max(x, axis=axis, keepdims=True)
    ids = lax.broadcasted_iota(jnp.int32, x.shape, axis)
    first = jnp.min(jnp.where(x == m, ids, n), axis=axis, keepdims=True)
    return m, ids == first


def _eye(n):
    return (lax.broadcasted_iota(jnp.int32, (n, n), 0) == lax.broadcasted_iota(jnp.int32, (n, n), 1)).astype(F32)


class _Streams:
    def __init__(self, nb, l, ls, tm):
        self.tm, self.tps = tm, l // tm
        self.n_p, self.n_s = nb * (l // tm), ls // tm
        self.n = self.n_p + self.n_s

    def prompt(self, s):
        sp = jnp.minimum(s, self.n_p - 1)
        return sp // self.tps, sp % self.tps

    def sample(self, s, n_tiles=None):
        return jnp.clip(s - self.n_p, 0, (self.n_s if n_tiles is None else n_tiles) - 1)


def _route_kernel(n_prompt_tiles, lp_ref, ls_ref, bias_ref, ek_ref, rk_ref, wkt_ref, cnt_ref, run_scr):
    @pl.when(pl.program_id(0) == 0)
    def _():
        run_scr[...] = jnp.zeros_like(run_scr)

    t = lp_ref.shape[-1]
    scores = _sigmoid(jnp.where(pl.program_id(0) < n_prompt_tiles, lp_ref[...], ls_ref[...]))
    biased = scores + bias_ref[...]
    neg = jnp.float32(-jnp.inf)
    b3 = biased.reshape(N_GROUPS, GROUP_SIZE, t)
    m1, hit = _first_max(b3, 1, GROUP_SIZE)
    m2 = jnp.max(jnp.where(hit, neg, b3), axis=1, keepdims=True)
    gscore = (m1 + m2).reshape(N_GROUPS, t)
    gmask = jnp.zeros((N_GROUPS, t), jnp.bool_)
    for _ in range(TOPK_GROUPS):
        _, hit = _first_max(gscore, 0, N_GROUPS)
        gmask = jnp.logical_or(gmask, hit)
        gscore = jnp.where(hit, neg, gscore)
    emask = jnp.broadcast_to(gmask.reshape(N_GROUPS, 1, t), (N_GROUPS, GROUP_SIZE, t)).reshape(N_EXPERTS, t)
    masked = jnp.where(emask, biased, neg)
    eid = lax.broadcasted_iota(jnp.int32, (N_EXPERTS, t), 0).astype(F32)
    hits, e_rows, w_rows = [], [], []
    for _ in range(TOP_K):
        _, hit = _first_max(masked, 0, N_EXPERTS)
        hits.append(hit)
        e_rows.append(jnp.sum(jnp.where(hit, eid, 0.0), axis=0, keepdims=True))
        w_rows.append(jnp.sum(jnp.where(hit, scores, 0.0), axis=0, keepdims=True))
        masked = jnp.where(hit, neg, masked)
    sel = sum(hit.astype(F32) for hit in hits)
    before = lax.broadcasted_iota(jnp.int32, (t, t), 0) < lax.broadcasted_iota(jnp.int32, (t, t), 1)
    rank = run_scr[:, 0:1] + _mm(sel, before.astype(F32))
    rk_rows = [jnp.sum(jnp.where(hit, rank, 0.0), axis=0, keepdims=True) for hit in hits]
    run_scr[...] = run_scr[...] + jnp.sum(sel, axis=1, keepdims=True)
    wk = jnp.concatenate(w_rows, axis=0)
    wk = wk / jnp.sum(wk, axis=0, keepdims=True) * ROUTED_SCALE
    ek_ref[...] = jnp.concatenate(e_rows, axis=0).astype(jnp.int32)
    rk_ref[...] = jnp.concatenate(rk_rows, axis=0).astype(jnp.int32)
    wkt_ref[...] = _mm_hi(wk, _eye(TOP_K), TN)
    cnt_ref[...] = run_scr[...]


def _route(logits_p, logits_s, router_bias, st):
    t_all = st.n * st.tm
    pick = pl.BlockSpec((None, TOP_K, st.tm), lambda s: (0, 0, s))
    return pl.pallas_call(
        functools.partial(_route_kernel, st.n_p),
        out_shape=(jax.ShapeDtypeStruct((1, TOP_K, t_all), jnp.int32),
                   jax.ShapeDtypeStruct((1, TOP_K, t_all), jnp.int32),
                   jax.ShapeDtypeStruct((1, t_all, TOP_K), F32),
                   jax.ShapeDtypeStruct((N_EXPERTS, LANES), F32)),
        grid=(st.n,),
        in_specs=[pl.BlockSpec((None, N_EXPERTS, st.tm), lambda s: (st.prompt(s)[0], 0, st.prompt(s)[1])),
                  pl.BlockSpec((None, N_EXPERTS, st.tm), lambda s: (0, 0, st.sample(s))),
                  pl.BlockSpec((N_EXPERTS, 1), lambda s: (0, 0))],
        out_specs=(pick, pick, pl.BlockSpec((None, st.tm, TOP_K), lambda s: (0, s, 0)),
                   pl.BlockSpec((N_EXPERTS, LANES), lambda s: (0, 0))),
        scratch_shapes=[pltpu.VMEM((N_EXPERTS, LANES), F32)],
        compiler_params=_params("arbitrary"),
        name="route",
    )(logits_p, logits_s, router_bias.reshape(N_EXPERTS, 1))


def _plan_kernel(block, cnt_ref, ek_ref, rk_ref, pos_ref, be_ref, pe_ref, nu_ref, start_scr):
    @pl.when(jnp.logical_and(pl.program_id(0) == 0, pl.program_id(1) == 0))
    def _():
        padded = jnp.ceil(cnt_ref[...] * (1.0 / block)) * block
        r = lax.broadcasted_iota(jnp.int32, (N_EXPERTS, N_EXPERTS), 0)
        c = lax.broadcasted_iota(jnp.int32, (N_EXPERTS, N_EXPERTS), 1)
        pad_end = _mm_hi((r >= c).astype(F32), padded)
        start_scr[...] = pad_end - padded
        pe_ref[...] = pad_end.astype(jnp.int32)
        nu_ref[...] = (pad_end[N_EXPERTS - 1:, :] * (1.0 / block)).astype(jnp.int32)
        nbp = be_ref.shape[-1]
        first_row = lax.broadcasted_iota(jnp.int32, (1, nbp), 1).astype(F32) * block
        owner = jnp.sum((pad_end[:, 0:1] <= first_row).astype(F32), axis=0, keepdims=True)
        be_ref[...] = jnp.minimum(owner, N_EXPERTS - 1.0).astype(jnp.int32)

    t = ek_ref.shape[-1]
    eid = lax.broadcasted_iota(jnp.int32, (N_EXPERTS, t), 0)
    start = start_scr[:, 0:1]
    rows = [jnp.sum(jnp.where(eid == ek_ref[k:k + 1, :], start, 0.0), axis=0, keepdims=True)
            for k in range(TOP_K)]
    pos_ref[...] = jnp.concatenate(rows, axis=0).astype(jnp.int32) + rk_ref[...]


def _plan(counts, ek, rk, tr, n_blocks, block):
    nb, _, l = ek.shape
    nbp = -(-n_blocks // LANES) * LANES
    pick = pl.BlockSpec((None, TOP_K, tr), lambda b, i: (b, 0, i))
    const = lambda shape: pl.BlockSpec(shape, lambda b, i: (0, 0))
    return pl.pallas_call(
        functools.partial(_plan_kernel, block),
        out_shape=(jax.ShapeDtypeStruct((nb, TOP_K, l), jnp.int32),
                   jax.ShapeDtypeStruct((1, nbp), jnp.int32),
                   jax.ShapeDtypeStruct((N_EXPERTS, LANES), jnp.int32),
                   jax.ShapeDtypeStruct((1, LANES), jnp.int32)),
        grid=(nb, l // tr),
        in_specs=[const((N_EXPERTS, LANES)), pick, pick],
        out_specs=(pick, const((1, nbp)), const((N_EXPERTS, LANES)), const((1, LANES))),
        scratch_shapes=[pltpu.VMEM((N_EXPERTS, LANES), F32)],
        compiler_params=_params("arbitrary", "arbitrary"),
        name="plan",
    )(counts, ek, rk)


def _dispatch_kernel(n_prompt_tiles, pe_ref, pos_ref, xp_ref, xq_ref, xs_hbm, zero_scr, sem):
    s = pl.program_id(0)
    tm = pos_ref.shape[-1]

    @pl.when(s == 0)
    def _():
        zero_scr[...] = jnp.zeros_like(zero_scr)

        def tail_copy(e):
            block = zero_scr.shape[0]
            return pltpu.make_async_copy(zero_scr, xs_hbm.at[pl.ds(pe_ref[e] - block, block)], sem)

        def nonempty(e):
            return pe_ref[e] > jnp.where(e > 0, pe_ref[jnp.maximum(e - 1, 0)], 0)

        @pl.loop(0, N_EXPERTS)
        def _(e):
            @pl.when(nonempty(e))
            def _():
                tail_copy(e).start()

        @pl.loop(0, N_EXPERTS)
        def _(e):
            @pl.when(nonempty(e))
            def _():
                tail_copy(e).wait()

    def scatter(x_ref):
        def row(k, t):
            return pltpu.make_async_copy(x_ref.at[t], xs_hbm.at[pos_ref[k, t]], sem)

        @pl.loop(0, tm)
        def _(t):
            for k in range(TOP_K):
                row(k, t).start(priority=k % 2)

        @pl.loop(0, tm)
        def _(t):
            for k in range(TOP_K):
                row(k, t).wait()

    @pl.when(s < n_prompt_tiles)
    def _():
        scatter(xp_ref)

    @pl.when(s >= n_prompt_tiles)
    def _():
        scatter(xq_ref)


def _dispatch(x_p, x_s, pos, pad_end, n_rows, st, block):
    tile = (st.tm,) + TOKEN_TILE
    flat = lambda x: x.reshape((-1,) + TOKEN_TILE)
    return pl.pallas_call(
        functools.partial(_dispatch_kernel, st.n_p),
        out_shape=jax.ShapeDtypeStruct((n_rows,) + TOKEN_TILE, U32),
        grid_spec=pltpu.PrefetchScalarGridSpec(
            num_scalar_prefetch=1,
            grid=(st.n,),
            in_specs=[pl.BlockSpec((None, TOP_K, st.tm), lambda s, pe: (0, 0, s), memory_space=pltpu.SMEM),
                      pl.BlockSpec(tile, lambda s, pe: (jnp.minimum(s, st.n_p - 1), 0, 0)),
                      pl.BlockSpec(tile, lambda s, pe: (st.sample(s), 0, 0))],
            out_specs=pl.BlockSpec(memory_space=pl.ANY),
            scratch_shapes=[pltpu.VMEM((block,) + TOKEN_TILE, U32), pltpu.SemaphoreType.DMA(())]),
        compiler_params=_params("arbitrary"),
        name="dispatch",
    )(pad_end, pos, flat(x_p), flat(x_s))


def _experts_kernel(be_ref, nu_ref, x_ref, wg_ref, wu_ref, wd_ref, y_ref):
    @pl.when(pl.program_id(0) < nu_ref[0])
    def _():
        xb = _unpack_rows(x_ref[...]).astype(BF16)
        y_ref[...] = _pack_rows(_mm(_silu(_mm(xb, wg_ref[...])) * _mm(xb, wu_ref[...]), wd_ref[...]))


def _experts(xs, block_expert, n_used, w_gate, w_up, w_down, block):
    n_rows, d = xs.shape[0], D_MODEL
    used = lambda i, nu: jnp.minimum(i, nu[0] - 1)
    rows = pl.BlockSpec((block,) + TOKEN_TILE, lambda i, be, nu: (used(i, nu), 0, 0))
    return pl.pallas_call(
        _experts_kernel,
        out_shape=jax.ShapeDtypeStruct((n_rows,) + TOKEN_TILE, U32),
        grid_spec=pltpu.PrefetchScalarGridSpec(
            num_scalar_prefetch=2,
            grid=(n_rows // block,),
            in_specs=[rows,
                      pl.BlockSpec((None, d, D_EXPERT), lambda i, be, nu: (be[used(i, nu)], 0, 0)),
                      pl.BlockSpec((None, d, D_EXPERT), lambda i, be, nu: (be[used(i, nu)], 0, 0)),
                      pl.BlockSpec((None, D_EXPERT, d), lambda i, be, nu: (be[used(i, nu)], 0, 0))],
            out_specs=rows),
        compiler_params=_params("arbitrary"),
        name="experts",
    )(block_expert, n_used, xs, w_gate, w_up, w_down)


def _combine_kernel(n_prompt_tiles, pos_ref, pos_next_ref, wkt_ref, ys_hbm, hp_ref, hq_ref, x1p_ref, x1q_ref,
                    g2p_ref, g2q_ref, wsg_ref, wsu_ref, wsd_ref, fw_ref, yp_ref, yq_ref, rows_scr, sem):
    s = pl.program_id(0)
    tm = pos_ref.shape[-1]
    slot = lax.rem(s, 2)
    in_prompt = s < n_prompt_tiles

    def row(p_ref, sl, k, t):
        return pltpu.make_async_copy(ys_hbm.at[p_ref[k, t]], rows_scr.at[sl, k, t], sem.at[sl])

    def fetch(p_ref, sl):
        @pl.loop(0, tm)
        def _(t):
            for k in range(TOP_K):
                row(p_ref, sl, k, t).start(priority=k % 2)

    @pl.when(s == 0)
    def _():
        fetch(pos_ref, 0)

    @pl.when(s + 1 < pl.num_programs(0))
    def _():
        fetch(pos_next_ref, 1 - slot)

    hb = jnp.where(in_prompt, hp_ref[...], hq_ref[...])
    acc = _mm(_silu(_mm(hb, wsg_ref[...])) * _mm(hb, wsu_ref[...]), wsd_ref[...])

    @pl.loop(0, tm)
    def _(t):
        for k in range(TOP_K):
            row(pos_ref, slot, k, t).wait()

    wkt = wkt_ref[...]
    for k in range(TOP_K):
        acc = acc + _unpack_rows(rows_scr[slot, k]) * wkt[:, k:k + 1]

    @pl.when(in_prompt)
    def _():
        yp_ref[...] = _rms(x1p_ref[...] + g2p_ref[...] * acc, fw_ref[...])

    @pl.when(jnp.logical_not(in_prompt))
    def _():
        yq_ref[...] = _rms(x1q_ref[...] + g2q_ref[...] * acc, fw_ref[...])


def _combine(ys, pos, wkt, h_p, h_s, x1_p, x1_s, mod_p, mod_s, ws_gate, ws_up, ws_down, final_w, st):
    nb, l, d = x1_p.shape
    real = x1_s.shape[1] // st.tm
    prompt = lambda w, col=0: pl.BlockSpec((None, st.tm, w), lambda s: (*st.prompt(s), col))
    sample = lambda w, n, col=0: pl.BlockSpec((None, st.tm, w), lambda s: (0, st.sample(s, n), col))
    full = lambda shape: pl.BlockSpec(shape, lambda s: (0,) * len(shape))
    pick = lambda step: pl.BlockSpec((None, TOP_K, st.tm), lambda s: (0, 0, step(s)), memory_space=pltpu.SMEM)
    return pl.pallas_call(
        functools.partial(_combine_kernel, st.n_p),
        out_shape=(jax.ShapeDtypeStruct((nb, l, d), F32), jax.ShapeDtypeStruct((1, st.n_s * st.tm, d), F32)),
        grid=(st.n,),
        in_specs=[pick(lambda s: s), pick(lambda s: jnp.minimum(s + 1, st.n - 1)),
                  pl.BlockSpec((None, st.tm, TOP_K), lambda s: (0, s, 0)), pl.BlockSpec(memory_space=pl.ANY),
                  prompt(d), sample(d, real), prompt(d), sample(d, real),
                  pl.BlockSpec((None, 1, d), lambda s: (st.prompt(s)[0], 0, 5)), sample(d, real, 5),
                  full((d, D_SHARED)), full((d, D_SHARED)), full((D_SHARED, d)), full((1, d))],
        out_specs=(prompt(d), sample(d, None)),
        scratch_shapes=[pltpu.VMEM((2, TOP_K, st.tm) + TOKEN_TILE, U32), pltpu.SemaphoreType.DMA((2,))],
        compiler_params=_params("arbitrary"),
        name="combine",
    )(pos, pos, wkt, ys, h_p, h_s, x1_p, x1_s, mod_p, mod_s, ws_gate, ws_up, ws_down, final_w)


def _moe(pre_p, pre_s, mod_p, mod_s, w):
    x1_p, h_p, hf3_p, logits_p = pre_p
    x1_s, h_s, hf3_s, logits_s = pre_s
    nb, l, _ = x1_p.shape
    ls = x1_s.shape[1]
    tr = min(l, 512)
    ls_pad = -(-ls // tr) * tr
    hf3_s = jnp.pad(hf3_s, ((0, 0), (0, ls_pad - ls), (0, 0), (0, 0)))
    logits_s = jnp.pad(logits_s, ((0, 0), (0, 0), (0, ls_pad - ls)))
    st = _Streams(nb, l, ls_pad, tr)
    st_c = _Streams(nb, l, ls_pad, min(l, ls, 128))
    n_tokens = st.n * st.tm
    n_blocks = n_tokens * TOP_K // MOE_BLOCK + N_EXPERTS
    ek, rk, wkt, counts = _route(logits_p, logits_s, w["router_bias"], st)
    pos, block_expert, pad_end, n_used = _plan(counts, ek, rk, tr, n_blocks, MOE_BLOCK)
    xs = _dispatch(hf3_p, hf3_s, pos, pad_end[:, 0], n_blocks * MOE_BLOCK, st, MOE_BLOCK)
    ys = _experts(xs, block_expert[0, :n_blocks], n_used[0, :1], w["w_gate"], w["w_up"], w["w_down"], MOE_BLOCK)
    y_p, y_s = _combine(ys, pos, wkt, h_p, h_s, x1_p, x1_s, mod_p, mod_s, w["ws_gate"], w["ws_up"], w["ws_down"],
                        w["final_norm"], st_c)
    return y_p, y_s[:, :ls]


def _trunk(x, mod, per_token, states, w):
    nb, l, _ = x.shape
    tm_proj, tn_proj, tm_post = min(l, 1024), 2048, min(l, 512)
    proj, ba, bat = _in_projection(x, mod, per_token, w["norm_mix"], w["w_main"], w["w_in"], tm_proj, tn_proj)
    if states is None:
        c0 = jnp.zeros((1, DN_CONV - 1, DN_CONV_W), F32)
        sd0 = jnp.zeros((1, DN_HEADS, DN_DK, DN_DV), F32)
        sr0 = jnp.zeros((1, RET_HEADS, RET_DK, RET_DV), F32)
        o_a, sd, cb = _delta_prefill(proj, bat, w["conv_w"], w["a_log"], w["dt_bias"], w["dn_norm"], sd0, c0)
        o_b, sr = _ret_prefill(proj, w["inv_freq"], w["gn_w"], w["gn_b"], sr0)
    else:
        c0, sd0, sr0 = states
        n = nb * l
        o_a, sd, cb = _delta_step(proj.reshape(n, 1, PROJ_W), ba.reshape(n, 1, BA_W), w["conv_w"], w["a_log"],
                                  w["dt_bias"], w["dn_norm"], sd0, c0)
        o_b, sr = _ret_step(proj.reshape(n, 1, PROJ_W), w["inv_freq"], w["gn_w"], w["gn_b"], sr0)
        o_a = o_a.reshape(nb, l, DN_VW)
        o_b = o_b.reshape(nb, l, RET_VW)
    pre_moe = _post_mixer(o_a, o_b, proj, x, mod, per_token, w["w_down_a"], w["w_down_b"], w["w_out"],
                          w["norm_ffn"], w["router_w"], tm_post)
    return pre_moe, cb, sd, sr


def kernel(x_prompt, x_sample, state_conv, state_delta, state_ret, c_prompt, c_sample, w_mod, b_mod, norm_mix_w, w_in, conv_w, a_log, dt_bias, dn_norm_w, ret_gn_w, ret_gn_b, w_down_a, w_down_b, w_out, norm_ffn_w, router_w, router_bias, w_gate, w_up, w_down, ws_gate, ws_up, ws_down, final_norm_w):
    bp, lp, _ = x_prompt.shape
    bs = x_sample.shape[0]
    half = RET_DK // 2
    w_in0 = w_in[0]
    c0, c1 = DN_CONV_W, DN_CONV_W + 2 * DN_HEADS
    w = {
        "norm_mix": norm_mix_w[0].reshape(1, D_MODEL),
        "w_main": jnp.concatenate([w_in0[:, :c0].astype(BF16), w_in0[:, c1:].astype(BF16)], axis=1),
        "w_in": w_in0,
        "conv_w": conv_w[0], "a_log": a_log[0], "dt_bias": dt_bias[0], "dn_norm": dn_norm_w[0],
        "inv_freq": (ROPE_BASE ** (-jnp.arange(half, dtype=F32) / half)).reshape(1, half),
        "gn_w": ret_gn_w[0], "gn_b": ret_gn_b[0],
        "w_down_a": w_down_a[0].astype(BF16), "w_down_b": w_down_b[0].astype(BF16), "w_out": w_out[0].astype(BF16),
        "norm_ffn": norm_ffn_w[0].reshape(1, D_MODEL), "router_w": router_w[0], "router_bias": router_bias[0],
        "w_gate": w_gate[0], "w_up": w_up[0], "w_down": w_down[0],
        "ws_gate": ws_gate[0].astype(BF16), "ws_up": ws_up[0].astype(BF16), "ws_down": ws_down[0].astype(BF16),
        "final_norm": final_norm_w.reshape(1, D_MODEL),
    }
    mod = _modulation(jnp.concatenate([c_prompt, c_sample], axis=0), w_mod[0], b_mod[0])
    mod_p = mod[:bp].reshape(bp, 1, MOD_CHUNKS * D_MODEL)
    mod_s = mod[bp:].reshape(1, bs, MOD_CHUNKS * D_MODEL)

    pre_p, conv_p, delta_p, ret_p = _trunk(x_prompt, mod_p, False, None, w)
    pre_s, conv_s, delta_s, ret_s = _trunk(x_sample.reshape(1, bs, D_MODEL), mod_s, True,
                                           (state_conv[0], state_delta[0], state_ret[0]), w)
    y_p, y_s = _moe(pre_p, pre_s, mod_p, mod_s, w)
    return (y_p, y_s.reshape(bs, 1, D_MODEL), conv_p[None], delta_p[None], ret_p[None],
            conv_s[None], delta_s[None], ret_s[None])
```

```python
import functools
import math

import jax
import jax.numpy as jnp
from jax import lax
from jax.experimental import pallas as pl
from jax.experimental.pallas import tpu as pltpu

F32 = jnp.float32
BF16 = jnp.bfloat16

D_MODEL = 1024
DN_HEADS, DN_DK, DN_DV, DN_CONV = 8, 128, 128, 4
DN_QK = DN_HEADS * DN_DK
DN_VW = DN_HEADS * DN_DV
DN_CONV_W = 2 * DN_QK + DN_VW
RET_HEADS, RET_DK, RET_DV = 4, 256, 512
RET_QK = RET_HEADS * RET_DK
RET_VW = RET_HEADS * RET_DV
ROPE_BASE = 10000.0
PAST_LEN = 16384
N_EXPERTS, TOP_K, N_GROUPS, TOPK_GROUPS = 64, 8, 8, 4
GROUP_SIZE = N_EXPERTS // N_GROUPS
D_EXPERT, D_SHARED = 256, 256
ROUTED_SCALE = 2.5
MOD_CHUNKS = 6
EPS = 1e-6

PROJ_W = 12 * D_MODEL
COL_Q_A, COL_K_A, COL_V_A, COL_Z_A = 0, 1, 2, 3
COL_Q_B, COL_K_B = 4, 5
COL_V_B, COL_G_B = 3, 4
COL_GATE_A, COL_GATE_B = 10, 11
BA_W = 128
CHUNK = 128
CONV_HALO = 8
VMEM_LIMIT = 56 * 1024 * 1024
LANES = 128
MOE_BLOCK = 512
U32 = jnp.uint32
HALF_D = D_MODEL // 2
TOKEN_TILE = (HALF_D // LANES, LANES)

NN = (((1,), (0,)), ((), ()))
NT = (((1,), (1,)), ((), ()))
TN = (((0,), (0,)), ((), ()))


def _mm(a, b, dims=NN):
    return lax.dot_general(a.astype(BF16), b.astype(BF16), dims, preferred_element_type=F32)


def _mm_hi(a, b, dims=NN):
    return lax.dot_general(a.astype(F32), b.astype(F32), dims, precision=lax.Precision.HIGHEST,
                           preferred_element_type=F32)


def _hi_lo(x):
    hi = x.astype(BF16)
    return hi, (x - hi.astype(F32)).astype(BF16)


def _split_lhs(a):
    hi, lo = _hi_lo(a)
    return jnp.concatenate([hi, lo, hi], axis=1)


def _split_rhs(b):
    hi, lo = _hi_lo(b)
    return jnp.concatenate([hi, hi, lo], axis=0)


def _mm_split(a3, b3):
    return lax.dot_general(a3, b3, NN, preferred_element_type=F32)


def _sigmoid(x):
    return jax.nn.sigmoid(x)


def _silu(x):
    return x * _sigmoid(x)


def _softplus(x):
    return jnp.maximum(x, 0.0) + jnp.log1p(jnp.exp(-jnp.abs(x)))


def _rms(x, w):
    return x * lax.rsqrt(jnp.mean(x * x, axis=-1, keepdims=True) + EPS) * w


def _pack_rows(x):
    lo = lax.bitcast_convert_type(x[:, :HALF_D].astype(BF16).astype(F32), U32) >> U32(16)
    hi = lax.bitcast_convert_type(x[:, HALF_D:].astype(BF16).astype(F32), U32) & U32(0xFFFF0000)
    return (lo | hi).reshape((x.shape[0],) + TOKEN_TILE)


def _unpack_rows(w):
    w = w.reshape(w.shape[0], HALF_D)
    return jnp.concatenate([lax.bitcast_convert_type(w << U32(16), F32),
                            lax.bitcast_convert_type(w & U32(0xFFFF0000), F32)], axis=1)


def _params(*sem):
    return pltpu.CompilerParams(dimension_semantics=sem, vmem_limit_bytes=VMEM_LIMIT)


def _mod_kernel(c_ref, w_ref, b_ref, o_ref):
    o_ref[...] = _mm_hi(_silu(c_ref[...]), w_ref[...]) + b_ref[...]


def _modulation(c, w_mod, b_mod):
    n = c.shape[0]
    tn = 1536
    return pl.pallas_call(
        _mod_kernel,
        out_shape=jax.ShapeDtypeStruct((n, MOD_CHUNKS * D_MODEL), F32),
        grid=(MOD_CHUNKS * D_MODEL // tn,),
        in_specs=[pl.BlockSpec((n, D_MODEL), lambda j: (0, 0)),
                  pl.BlockSpec((D_MODEL, tn), lambda j: (0, j)),
                  pl.BlockSpec((1, tn), lambda j: (0, j))],
        out_specs=pl.BlockSpec((n, tn), lambda j: (0, j)),
        compiler_params=_params("parallel"),
        name="modulation",
    )(c, w_mod, b_mod.reshape(1, -1))


def _mod_spec(per_token, tm, col):
    if per_token:
        return pl.BlockSpec((None, tm, D_MODEL), lambda b, i, *_: (b, i, col))
    return pl.BlockSpec((None, 1, D_MODEL), lambda b, i, *_: (b, 0, col))


def _inproj_kernel(x_ref, sh_ref, sc_ref, nw_ref, w_ref, wba_ref, out_ref, ba_ref, bat_ref, h_scr):
    @pl.when(pl.program_id(2) == 0)
    def _():
        h = _rms(x_ref[...], nw_ref[...]) * (1.0 + sc_ref[...]) + sh_ref[...]
        hb = h.astype(BF16)
        h_scr[...] = hb
        wb = wba_ref[...]
        ba_ref[...] = _mm(hb, wb)
        bat_ref[...] = _mm(wb.T, hb, NT)[0:2 * DN_HEADS]

    out_ref[...] = _mm(h_scr[...], w_ref[...]).astype(BF16)


def _in_projection(x, mod, per_token, norm_w, w_main, w_in, tm, tn):
    nb, l, _ = x.shape
    assert DN_CONV_W % BA_W == 0
    return pl.pallas_call(
        _inproj_kernel,
        out_shape=(jax.ShapeDtypeStruct((nb, l, PROJ_W), BF16),
                   jax.ShapeDtypeStruct((nb, l, BA_W), F32),
                   jax.ShapeDtypeStruct((nb, 2 * DN_HEADS, l), F32)),
        grid=(nb, l // tm, PROJ_W // tn),
        in_specs=[pl.BlockSpec((None, tm, D_MODEL), lambda b, i, j: (b, i, 0)),
                  _mod_spec(per_token, tm, 0),
                  _mod_spec(per_token, tm, 1),
                  pl.BlockSpec((1, D_MODEL), lambda b, i, j: (0, 0)),
                  pl.BlockSpec((D_MODEL, tn), lambda b, i, j: (0, j)),
                  pl.BlockSpec((D_MODEL, BA_W), lambda b, i, j: (0, DN_CONV_W // BA_W))],
        out_specs=(pl.BlockSpec((None, tm, tn), lambda b, i, j: (b, i, j)),
                   pl.BlockSpec((None, tm, BA_W), lambda b, i, j: (b, i, 0)),
                   pl.BlockSpec((None, 2 * DN_HEADS, tm), lambda b, i, j: (b, 0, i))),
        scratch_shapes=[pltpu.VMEM((tm, D_MODEL), BF16)],
        compiler_params=_params("parallel", "parallel", "arbitrary"),
        name="in_projection",
    )(x, mod, mod, norm_w, w_main, w_in)


def _delta_prefill_kernel(q_ref, k_ref, v_ref, z_ref, bat_ref, cw_ref, alog_c_ref, dtb_c_ref, nw_ref, s0_ref, c0_ref,
                          o_ref, sout_ref, cout_ref,
                          ext_scr, s_scr, qkv_scr, pow_scr, inv_scr, uw_scr, phi_scr, plo_scr):
    step = pl.program_id(1)
    c = CHUNK
    lo = CONV_HALO - (DN_CONV - 1)

    @pl.when(step == 0)
    def _():
        s_scr[...] = s0_ref[...]
        ext_scr[lo:CONV_HALO, :] = c0_ref[...]

    ext_scr[CONV_HALO:, 0:DN_QK] = q_ref[...].astype(F32)
    ext_scr[CONV_HALO:, DN_QK:2 * DN_QK] = k_ref[...].astype(F32)
    ext_scr[CONV_HALO:, 2 * DN_QK:] = v_ref[...].astype(F32)

    row = lax.broadcasted_iota(jnp.int32, (c, c), 0)
    col = lax.broadcasted_iota(jnp.int32, (c, c), 1)
    causal = row >= col
    strict = row > col
    eye = (row == col).astype(F32)
    utri = (row <= col).astype(F32)

    bat = bat_ref[...]
    g_t = -jnp.exp(alog_c_ref[...]) * _softplus(bat[DN_HEADS:, :] + dtb_c_ref[...])
    gc_t = _mm_hi(g_t, utri)
    gates = _mm_hi(jnp.concatenate([_sigmoid(bat[:DN_HEADS, :]), gc_t], axis=0), _eye(2 * DN_HEADS), TN)
    beta_tok, gc_tok = gates[:, :DN_HEADS], gates

    def conv(c0):
        acc = ext_scr[lo:lo + c, c0:c0 + DN_DK] * cw_ref[0:1, c0:c0 + DN_DK]
        for i in range(1, DN_CONV):
            acc = acc + ext_scr[lo + i:lo + i + c, c0:c0 + DN_DK] * cw_ref[i:i + 1, c0:c0 + DN_DK]
        return _silu(acc)

    def decay_of(h):
        gc = gc_tok[:, DN_HEADS + h:DN_HEADS + h + 1]
        return gc, jnp.where(causal, jnp.exp(jnp.where(causal, gc - gc_t[h:h + 1, :], 0.0)), 0.0)

    for h in range(DN_HEADS):
        q = conv(h * DN_DK)
        k = conv(DN_QK + h * DN_DK)
        q = q * lax.rsqrt(jnp.sum(q * q, axis=-1, keepdims=True) + EPS) * (DN_DK ** -0.5)
        k = k * lax.rsqrt(jnp.sum(k * k, axis=-1, keepdims=True) + EPS)
        qkv_scr[:, h * DN_DK:(h + 1) * DN_DK] = q
        qkv_scr[:, DN_QK + h * DN_DK:DN_QK + (h + 1) * DN_DK] = k
        qkv_scr[:, 2 * DN_QK + h * DN_DV:2 * DN_QK + (h + 1) * DN_DV] = conv(2 * DN_QK + h * DN_DV)
        _, decay = decay_of(h)
        a_mat = jnp.where(strict, _mm(k * beta_tok[:, h:h + 1], k, NT) * decay, 0.0)
        phi_scr[h], plo_scr[h] = _hi_lo(a_mat)
        inv_scr[h] = eye - a_mat
        pow_scr[h] = jnp.where(causal, _mm(q, k, NT) * decay, 0.0)

    for _ in range(int(math.log2(c)) - 1):
        for h in range(DN_HEADS):
            p_hi, p_lo = phi_scr[h], plo_scr[h]
            phi_scr[h], plo_scr[h] = _hi_lo(_mm_split(jnp.concatenate([p_hi, p_lo, p_hi], axis=1),
                                                      jnp.concatenate([p_hi, p_hi, p_lo], axis=0)))
        for h in range(DN_HEADS):
            inv = inv_scr[h]
            p_hi, p_lo = phi_scr[h], plo_scr[h]
            inv_scr[h] = inv + _mm_split(_split_lhs(inv), jnp.concatenate([p_hi, p_hi, p_lo], axis=0))

    def qkv_of(h):
        return (qkv_scr[:, h * DN_DK:(h + 1) * DN_DK], qkv_scr[:, DN_QK + h * DN_DK:DN_QK + (h + 1) * DN_DK],
                qkv_scr[:, 2 * DN_QK + h * DN_DV:2 * DN_QK + (h + 1) * DN_DV])

    for h in range(DN_HEADS):
        _, k, v = qkv_of(h)
        beta = beta_tok[:, h:h + 1]
        gc = gc_tok[:, DN_HEADS + h:DN_HEADS + h + 1]
        rhs = jnp.concatenate([v * beta, k * beta * jnp.exp(gc)], axis=-1)
        uw_scr[h] = _mm_split(_split_lhs(inv_scr[h]), _split_rhs(rhs))
    for h in range(DN_HEADS):
        inv_scr[h] = uw_scr[h, :, 0:DN_DV] - _mm(uw_scr[h, :, DN_DV:], s_scr[h])
    for h in range(DN_HEADS):
        q, k, _ = qkv_of(h)
        gc = gc_tok[:, DN_HEADS + h:DN_HEADS + h + 1]
        g_last = gc[c - 1:c, :]
        s = s_scr[h]
        v_new = inv_scr[h]
        o = _mm(q * jnp.exp(gc), s) + _mm(pow_scr[h], v_new)
        s_scr[h] = s * jnp.exp(g_last) + _mm(k * jnp.exp(g_last - gc), v_new, TN)
        z = z_ref[:, h * DN_DV:(h + 1) * DN_DV].astype(F32)
        o_ref[:, h * DN_DV:(h + 1) * DN_DV] = (_rms(o, nw_ref[...]) * _silu(z)).astype(BF16)

    ext_scr[0:CONV_HALO, :] = ext_scr[c:c + CONV_HALO, :]

    @pl.when(step == pl.num_programs(1) - 1)
    def _():
        sout_ref[...] = s_scr[...]
        cout_ref[...] = ext_scr[lo:CONV_HALO, :]


def _lane_pad(v, offset):
    return jnp.zeros((1, BA_W), F32).at[0, offset:offset + v.shape[0]].set(v.astype(F32))


def _delta_prefill(proj, bat, conv_w, a_log, dt_bias, dn_norm_w, s0, c0):
    nb, l, _ = proj.shape
    c = CHUNK
    col_spec = lambda j: pl.BlockSpec((None, c, D_MODEL), lambda b, i: (b, i, j))
    full = lambda shape: pl.BlockSpec(shape, lambda b, i: (0,) * len(shape))
    return pl.pallas_call(
        _delta_prefill_kernel,
        out_shape=(jax.ShapeDtypeStruct((nb, l, DN_VW), BF16),
                   jax.ShapeDtypeStruct((nb, DN_HEADS, DN_DK, DN_DV), F32),
                   jax.ShapeDtypeStruct((nb, DN_CONV - 1, DN_CONV_W), F32)),
        grid=(nb, l // c),
        in_specs=[col_spec(COL_Q_A), col_spec(COL_K_A), col_spec(COL_V_A), col_spec(COL_Z_A),
                  pl.BlockSpec((None, 2 * DN_HEADS, c), lambda b, i: (b, 0, i)),
                  full((DN_CONV, DN_CONV_W)), full((DN_HEADS, 1)), full((DN_HEADS, 1)), full((1, DN_DV)),
                  pl.BlockSpec((None, DN_HEADS, DN_DK, DN_DV), lambda b, i: (b % s0.shape[0], 0, 0, 0)),
                  pl.BlockSpec((None, DN_CONV - 1, DN_CONV_W), lambda b, i: (b % c0.shape[0], 0, 0))],
        out_specs=(pl.BlockSpec((None, c, DN_VW), lambda b, i: (b, i, 0)),
                   pl.BlockSpec((None, DN_HEADS, DN_DK, DN_DV), lambda b, i: (b, 0, 0, 0)),
                   pl.BlockSpec((None, DN_CONV - 1, DN_CONV_W), lambda b, i: (b, 0, 0))),
        scratch_shapes=[pltpu.VMEM((c + CONV_HALO, DN_CONV_W), F32),
                        pltpu.VMEM((DN_HEADS, DN_DK, DN_DV), F32),
                        pltpu.VMEM((c, DN_CONV_W), F32),
                        pltpu.VMEM((DN_HEADS, c, c), F32),
                        pltpu.VMEM((DN_HEADS, c, c), F32),
                        pltpu.VMEM((DN_HEADS, c, DN_DV + DN_DK), F32),
                        pltpu.VMEM((DN_HEADS, c, c), BF16),
                        pltpu.VMEM((DN_HEADS, c, c), BF16)],
        compiler_params=_params("parallel", "arbitrary"),
        name="delta_prefill",
    )(proj, proj, proj, proj, bat, conv_w, a_log.reshape(DN_HEADS, 1), dt_bias.reshape(DN_HEADS, 1),
      dn_norm_w.reshape(1, DN_DV), s0, c0)


def _log_gamma(h):
    return math.log(1.0 - 2.0 ** (-5.0 - h))


def _rotate(x, cos, sin):
    half = x.shape[-1] // 2
    x1, x2 = x[:, :half], x[:, half:]
    return jnp.concatenate([x1 * cos - x2 * sin, x1 * sin + x2 * cos], axis=-1)


def _group_norm_gate(o, gw, gb, gate):
    mu = jnp.mean(o, axis=-1, keepdims=True)
    var = jnp.mean(jnp.square(o - mu), axis=-1, keepdims=True)
    return ((o - mu) * lax.rsqrt(var + EPS) * gw + gb) * _silu(gate)


def _ret_prefill_kernel(q_ref, k_ref, v_ref, g_ref, inv_ref, gw_ref, gb_ref, s0_ref, o_ref, sout_ref,
                        s_scr, rope_scr, dmat_scr):
    seq, step = pl.program_id(0), pl.program_id(1)
    c = CHUNK

    @pl.when(step == 0)
    def _():
        s_scr[...] = s0_ref[...]

    idx_c = lax.broadcasted_iota(jnp.int32, (c, 1), 0)
    idx = idx_c.astype(F32)

    @pl.when(seq == 0)
    def _():
        ang = (step * c + idx_c).astype(F32) * inv_ref[...]
        rope_scr[step, 0] = jnp.cos(ang)
        rope_scr[step, 1] = jnp.sin(ang)

    @pl.when(jnp.logical_and(seq == 0, step == 0))
    def _():
        rel = (lax.broadcasted_iota(jnp.int32, (c, c), 0) - lax.broadcasted_iota(jnp.int32, (c, c), 1)).astype(F32)
        for h in range(RET_HEADS):
            dmat_scr[h] = jnp.where(rel >= 0, jnp.exp(jnp.maximum(rel, 0.0) * _log_gamma(h)), 0.0)

    cos, sin = rope_scr[step, 0], rope_scr[step, 1]

    for h in range(RET_HEADS):
        lg = _log_gamma(h)
        q = _rotate(q_ref[:, h * RET_DK:(h + 1) * RET_DK].astype(F32), cos, sin)
        k = _rotate(k_ref[:, h * RET_DK:(h + 1) * RET_DK].astype(F32), cos, sin) * (RET_DK ** -0.5)
        v = v_ref[:, h * RET_DV:(h + 1) * RET_DV]
        intra = _mm(_mm(q, k, NT) * dmat_scr[h], v)
        s = s_scr[h]
        cross = _mm(q * jnp.exp((idx + 1.0) * lg), s)
        s_scr[h] = s * math.exp(c * lg) + _mm(k * jnp.exp((c - 1.0 - idx) * lg), v, TN)
        sl = slice(h * RET_DV, (h + 1) * RET_DV)
        o_ref[:, sl] = _group_norm_gate(intra + cross, gw_ref[:, sl], gb_ref[:, sl],
                                        g_ref[:, sl].astype(F32)).astype(BF16)

    @pl.when(step == pl.num_programs(1) - 1)
    def _():
        sout_ref[...] = s_scr[...]


def _ret_prefill(proj, inv_freq, gn_w, gn_b, s0):
    nb, l, _ = proj.shape
    c = CHUNK
    full = lambda shape: pl.BlockSpec(shape, lambda b, i: (0,) * len(shape))
    state_spec = pl.BlockSpec((None, RET_HEADS, RET_DK, RET_DV), lambda b, i: (b, 0, 0, 0))
    return pl.pallas_call(
        _ret_prefill_kernel,
        out_shape=(jax.ShapeDtypeStruct((nb, l, RET_VW), BF16),
                   jax.ShapeDtypeStruct((nb, RET_HEADS, RET_DK, RET_DV), F32)),
        grid=(nb, l // c),
        in_specs=[pl.BlockSpec((None, c, RET_QK), lambda b, i: (b, i, COL_Q_B)),
                  pl.BlockSpec((None, c, RET_QK), lambda b, i: (b, i, COL_K_B)),
                  pl.BlockSpec((None, c, RET_VW), lambda b, i: (b, i, COL_V_B)),
                  pl.BlockSpec((None, c, RET_VW), lambda b, i: (b, i, COL_G_B)),
                  full((1, RET_DK // 2)), full((1, RET_VW)), full((1, RET_VW)),
                  pl.BlockSpec((None, RET_HEADS, RET_DK, RET_DV), lambda b, i: (b % s0.shape[0], 0, 0, 0))],
        out_specs=(pl.BlockSpec((None, c, RET_VW), lambda b, i: (b, i, 0)), state_spec),
        scratch_shapes=[pltpu.VMEM((RET_HEADS, RET_DK, RET_DV), F32),
                        pltpu.VMEM((l // c, 2, c, RET_DK // 2), F32),
                        pltpu.VMEM((RET_HEADS, c, c), F32)],
        compiler_params=_params("arbitrary", "arbitrary"),
        name="retention_prefill",
    )(proj, proj, proj, proj, inv_freq, gn_w.reshape(1, RET_VW), gn_b.reshape(1, RET_VW), s0)


ROWS = 8
SEQ_PER_STEP = 4
RET_SEQ_PER_STEP = 4


def _row_mm(a, s):
    return _mm(jnp.broadcast_to(a, (ROWS, a.shape[-1])), s)[0:1, :]


def _outer(a, b):
    first = lax.broadcasted_iota(jnp.int32, (ROWS, a.shape[-1]), 0) == 0
    a8 = jnp.where(first, jnp.broadcast_to(a, (ROWS, a.shape[-1])), 0.0)
    return _mm(a8, jnp.broadcast_to(b, (ROWS, b.shape[-1])), TN)


def _delta_step_kernel(qkv_ref, z_ref, ba_ref, cw_ref, alog_ref, dtb_ref, nw_ref, s_ref, c_ref,
                       o_ref, sout_ref, cout_ref):
    for j in range(qkv_ref.shape[0]):
        u_new = qkv_ref[j].astype(F32)
        buf = c_ref[j]
        acc = u_new * cw_ref[DN_CONV - 1:DN_CONV, :]
        for i in range(DN_CONV - 1):
            acc = acc + buf[i:i + 1, :] * cw_ref[i:i + 1, :]
        qkv = _silu(acc)
        cout_ref[j, 0:DN_CONV - 2, :] = buf[1:, :]
        cout_ref[j, DN_CONV - 2:, :] = u_new
        ba = ba_ref[j]
        beta_all = _sigmoid(ba)
        g_all = -jnp.exp(alog_ref[...]) * _softplus(ba + dtb_ref[...])
        for h in range(DN_HEADS):
            q = qkv[:, h * DN_DK:(h + 1) * DN_DK]
            k = qkv[:, DN_QK + h * DN_DK:DN_QK + (h + 1) * DN_DK]
            v = qkv[:, 2 * DN_QK + h * DN_DV:2 * DN_QK + (h + 1) * DN_DV]
            q = q * lax.rsqrt(jnp.sum(q * q, axis=-1, keepdims=True) + EPS) * (DN_DK ** -0.5)
            k = k * lax.rsqrt(jnp.sum(k * k, axis=-1, keepdims=True) + EPS)
            beta = beta_all[:, h:h + 1]
            eg = jnp.exp(g_all[:, DN_HEADS + h:DN_HEADS + h + 1])
            s = s_ref[j, h]
            kb = k * beta
            v_new = v * beta - _row_mm(kb * eg, s)
            o = _row_mm(q * eg, s) + jnp.sum(q * k, axis=-1, keepdims=True) * v_new
            sout_ref[j, h] = s * eg + _outer(k, v_new)
            z = z_ref[j, :, h * DN_DV:(h + 1) * DN_DV].astype(F32)
            o_ref[j, :, h * DN_DV:(h + 1) * DN_DV] = (_rms(o, nw_ref[...]) * _silu(z)).astype(BF16)


def _delta_step(proj, ba, conv_w, a_log, dt_bias, dn_norm_w, s0, c0):
    n = proj.shape[0]
    ns = math.gcd(n, SEQ_PER_STEP)
    full = lambda shape: pl.BlockSpec(shape, lambda b: (0,) * len(shape))
    state_spec = pl.BlockSpec((ns, DN_HEADS, DN_DK, DN_DV), lambda b: (b, 0, 0, 0))
    conv_spec = pl.BlockSpec((ns, DN_CONV - 1, DN_CONV_W), lambda b: (b, 0, 0))
    return pl.pallas_call(
        _delta_step_kernel,
        out_shape=(jax.ShapeDtypeStruct((n, 1, DN_VW), BF16),
                   jax.ShapeDtypeStruct((n, DN_HEADS, DN_DK, DN_DV), F32),
                   jax.ShapeDtypeStruct((n, DN_CONV - 1, DN_CONV_W), F32)),
        grid=(n // ns,),
        in_specs=[pl.BlockSpec((ns, 1, DN_CONV_W), lambda b: (b, 0, 0)),
                  pl.BlockSpec((ns, 1, DN_VW), lambda b: (b, 0, COL_Z_A)),
                  pl.BlockSpec((ns, 1, BA_W), lambda b: (b, 0, 0)),
                  full((DN_CONV, DN_CONV_W)), full((1, BA_W)), full((1, BA_W)), full((1, DN_DV)),
                  state_spec, conv_spec],
        out_specs=(pl.BlockSpec((ns, 1, DN_VW), lambda b: (b, 0, 0)), state_spec, conv_spec),
        compiler_params=_params("parallel"),
        name="delta_step",
    )(proj, proj, ba, conv_w, _lane_pad(a_log, DN_HEADS), _lane_pad(dt_bias, DN_HEADS),
      dn_norm_w.reshape(1, DN_DV), s0, c0)


def _ret_step_kernel(q_ref, k_ref, v_ref, g_ref, inv_ref, gw_ref, gb_ref, s_ref, o_ref, sout_ref):
    ang = float(PAST_LEN) * inv_ref[...]
    cos, sin = jnp.cos(ang), jnp.sin(ang)
    for j in range(q_ref.shape[0]):
        for h in range(RET_HEADS):
            gamma = math.exp(_log_gamma(h))
            q = _rotate(q_ref[j, :, h * RET_DK:(h + 1) * RET_DK].astype(F32), cos, sin)
            k = _rotate(k_ref[j, :, h * RET_DK:(h + 1) * RET_DK].astype(F32), cos, sin) * (RET_DK ** -0.5)
            v = v_ref[j, :, h * RET_DV:(h + 1) * RET_DV].astype(F32)
            s = s_ref[j, h]
            o = jnp.sum(q * k, axis=-1, keepdims=True) * v + _row_mm(q * gamma, s)
            sout_ref[j, h] = s * gamma + _outer(k, v)
            sl = slice(h * RET_DV, (h + 1) * RET_DV)
            o_ref[j, :, sl] = _group_norm_gate(o, gw_ref[:, sl], gb_ref[:, sl],
                                               g_ref[j, :, sl].astype(F32)).astype(BF16)


def _ret_step(proj, inv_freq, gn_w, gn_b, s0):
    n = proj.shape[0]
    ns = math.gcd(n, RET_SEQ_PER_STEP)
    full = lambda shape: pl.BlockSpec(shape, lambda b: (0,) * len(shape))
    state_spec = pl.BlockSpec((ns, RET_HEADS, RET_DK, RET_DV), lambda b: (b, 0, 0, 0))
    return pl.pallas_call(
        _ret_step_kernel,
        out_shape=(jax.ShapeDtypeStruct((n, 1, RET_VW), BF16),
                   jax.ShapeDtypeStruct((n, RET_HEADS, RET_DK, RET_DV), F32)),
        grid=(n // ns,),
        in_specs=[pl.BlockSpec((ns, 1, RET_QK), lambda b: (b, 0, COL_Q_B)),
                  pl.BlockSpec((ns, 1, RET_QK), lambda b: (b, 0, COL_K_B)),
                  pl.BlockSpec((ns, 1, RET_VW), lambda b: (b, 0, COL_V_B)),
                  pl.BlockSpec((ns, 1, RET_VW), lambda b: (b, 0, COL_G_B)),
                  full((1, RET_DK // 2)), full((1, RET_VW)), full((1, RET_VW)), state_spec],
        out_specs=(pl.BlockSpec((ns, 1, RET_VW), lambda b: (b, 0, 0)), state_spec),
        compiler_params=_params("parallel"),
        name="retention_step",
    )(proj, proj, proj, proj, inv_freq, gn_w.reshape(1, RET_VW), gn_b.reshape(1, RET_VW), s0)


def _post_mixer_kernel(oa_ref, ob_ref, ga_ref, gb_ref, x_ref, g1_ref, sh2_ref, sc2_ref, wda_ref, wdb_ref,
                       wout_ref, nw_ref, rw_ref, x1_ref, h_ref, hf_ref, logit_ref):
    y_a = _mm(oa_ref[...], wda_ref[...])
    y_b = _mm(ob_ref[...], wdb_ref[...])
    merged = _sigmoid(ga_ref[...].astype(F32)) * y_a + _sigmoid(gb_ref[...].astype(F32)) * y_b
    x1 = x_ref[...] + g1_ref[...] * _mm(merged, wout_ref[...])
    x1_ref[...] = x1
    hf = _rms(x1, nw_ref[...]) * (1.0 + sc2_ref[...]) + sh2_ref[...]
    h_ref[...] = hf.astype(BF16)
    hf_ref[...] = _pack_rows(hf)
    logit_ref[...] = _mm_hi(rw_ref[...], hf, NT)


def _post_mixer(o_a, o_b, proj, x, mod, per_token, w_down_a, w_down_b, w_out, norm_w, router_w, tm):
    nb, l, _ = x.shape
    tok = lambda w, j=0: pl.BlockSpec((None, tm, w), lambda b, i: (b, i, j))
    full = lambda shape: pl.BlockSpec(shape, lambda b, i: (0,) * len(shape))
    return pl.pallas_call(
        _post_mixer_kernel,
        out_shape=(jax.ShapeDtypeStruct((nb, l, D_MODEL), F32),
                   jax.ShapeDtypeStruct((nb, l, D_MODEL), BF16),
                   jax.ShapeDtypeStruct((nb, l) + TOKEN_TILE, U32),
                   jax.ShapeDtypeStruct((nb, N_EXPERTS, l), F32)),
        grid=(nb, l // tm),
        in_specs=[tok(DN_VW), tok(RET_VW), tok(D_MODEL, COL_GATE_A), tok(D_MODEL, COL_GATE_B), tok(D_MODEL),
                  _mod_spec(per_token, tm, 2), _mod_spec(per_token, tm, 3), _mod_spec(per_token, tm, 4),
                  full((DN_VW, D_MODEL)), full((RET_VW, D_MODEL)), full((D_MODEL, D_MODEL)),
                  full((1, D_MODEL)), full((N_EXPERTS, D_MODEL))],
        out_specs=(tok(D_MODEL), tok(D_MODEL),
                   pl.BlockSpec((None, tm) + TOKEN_TILE, lambda b, i: (b, i, 0, 0)),
                   pl.BlockSpec((None, N_EXPERTS, tm), lambda b, i: (b, 0, i))),
        compiler_params=_params("parallel", "parallel"),
        name="post_mixer",
    )(o_a, o_b, proj, proj, x, mod, mod, mod, w_down_a, w_down_b, w_out, norm_w, router_w)


def _first_max(x, axis, n):
    m = jnp.max(x, axis=axis, keepdims=True)
    ids = lax.broadcasted_iota(jnp.int32, x.shape, axis)
    first = jnp.min(jnp.where(x == m, ids, n), axis=axis, keepdims=True)
    return m, ids == first


def _eye(n):
    return (lax.broadcasted_iota(jnp.int32, (n, n), 0) == lax.broadcasted_iota(jnp.int32, (n, n), 1)).astype(F32)


class _Streams:
    def __init__(self, nb, l, ls, tm):
        self.tm, self.tps = tm, l // tm
        self.n_p, self.n_s = nb * (l // tm), ls // tm
        self.n = self.n_p + self.n_s

    def prompt(self, s):
        sp = jnp.minimum(s, self.n_p - 1)
        return sp // self.tps, sp % self.tps

    def sample(self, s, n_tiles=None):
        return jnp.clip(s - self.n_p, 0, (self.n_s if n_tiles is None else n_tiles) - 1)


def _route_kernel(n_prompt_tiles, lp_ref, ls_ref, bias_ref, ek_ref, rk_ref, wkt_ref, cnt_ref, run_scr):
    @pl.when(pl.program_id(0) == 0)
    def _():
        run_scr[...] = jnp.zeros_like(run_scr)

    t = lp_ref.shape[-1]
    scores = _sigmoid(jnp.where(pl.program_id(0) < n_prompt_tiles, lp_ref[...], ls_ref[...]))
    biased = scores + bias_ref[...]
    neg = jnp.float32(-jnp.inf)
    b3 = biased.reshape(N_GROUPS, GROUP_SIZE, t)
    m1, hit = _first_max(b3, 1, GROUP_SIZE)
    m2 = jnp.max(jnp.where(hit, neg, b3), axis=1, keepdims=True)
    gscore = (m1 + m2).reshape(N_GROUPS, t)
    gmask = jnp.zeros((N_GROUPS, t), jnp.bool_)
    for _ in range(TOPK_GROUPS):
        _, hit = _first_max(gscore, 0, N_GROUPS)
        gmask = jnp.logical_or(gmask, hit)
        gscore = jnp.where(hit, neg, gscore)
    emask = jnp.broadcast_to(gmask.reshape(N_GROUPS, 1, t), (N_GROUPS, GROUP_SIZE, t)).reshape(N_EXPERTS, t)
    masked = jnp.where(emask, biased, neg)
    eid = lax.broadcasted_iota(jnp.int32, (N_EXPERTS, t), 0).astype(F32)
    hits, e_rows, w_rows = [], [], []
    for _ in range(TOP_K):
        _, hit = _first_max(masked, 0, N_EXPERTS)
        hits.append(hit)
        e_rows.append(jnp.sum(jnp.where(hit, eid, 0.0), axis=0, keepdims=True))
        w_rows.append(jnp.sum(jnp.where(hit, scores, 0.0), axis=0, keepdims=True))
        masked = jnp.where(hit, neg, masked)
    sel = sum(hit.astype(F32) for hit in hits)
    before = lax.broadcasted_iota(jnp.int32, (t, t), 0) < lax.broadcasted_iota(jnp.int32, (t, t), 1)
    rank = run_scr[:, 0:1] + _mm(sel, before.astype(F32))
    rk_rows = [jnp.sum(jnp.where(hit, rank, 0.0), axis=0, keepdims=True) for hit in hits]
    run_scr[...] = run_scr[...] + jnp.sum(sel, axis=1, keepdims=True)
    wk = jnp.concatenate(w_rows, axis=0)
    wk = wk / jnp.sum(wk, axis=0, keepdims=True) * ROUTED_SCALE
    ek_ref[...] = jnp.concatenate(e_rows, axis=0).astype(jnp.int32)
    rk_ref[...] = jnp.concatenate(rk_rows, axis=0).astype(jnp.int32)
    wkt_ref[...] = _mm_hi(wk, _eye(TOP_K), TN)
    cnt_ref[...] = run_scr[...]


def _route(logits_p, logits_s, router_bias, st):
    t_all = st.n * st.tm
    pick = pl.BlockSpec((None, TOP_K, st.tm), lambda s: (0, 0, s))
    return pl.pallas_call(
        functools.partial(_route_kernel, st.n_p),
        out_shape=(jax.ShapeDtypeStruct((1, TOP_K, t_all), jnp.int32),
                   jax.ShapeDtypeStruct((1, TOP_K, t_all), jnp.int32),
                   jax.ShapeDtypeStruct((1, t_all, TOP_K), F32),
                   jax.ShapeDtypeStruct((N_EXPERTS, LANES), F32)),
        grid=(st.n,),
        in_specs=[pl.BlockSpec((None, N_EXPERTS, st.tm), lambda s: (st.prompt(s)[0], 0, st.prompt(s)[1])),
                  pl.BlockSpec((None, N_EXPERTS, st.tm), lambda s: (0, 0, st.sample(s))),
                  pl.BlockSpec((N_EXPERTS, 1), lambda s: (0, 0))],
        out_specs=(pick, pick, pl.BlockSpec((None, st.tm, TOP_K), lambda s: (0, s, 0)),
                   pl.BlockSpec((N_EXPERTS, LANES), lambda s: (0, 0))),
        scratch_shapes=[pltpu.VMEM((N_EXPERTS, LANES), F32)],
        compiler_params=_params("arbitrary"),
        name="route",
    )(logits_p, logits_s, router_bias.reshape(N_EXPERTS, 1))


def _plan_kernel(block, cnt_ref, ek_ref, rk_ref, pos_ref, be_ref, pe_ref, nu_ref, start_scr):
    @pl.when(jnp.logical_and(pl.program_id(0) == 0, pl.program_id(1) == 0))
    def _():
        padded = jnp.ceil(cnt_ref[...] * (1.0 / block)) * block
        r = lax.broadcasted_iota(jnp.int32, (N_EXPERTS, N_EXPERTS), 0)
        c = lax.broadcasted_iota(jnp.int32, (N_EXPERTS, N_EXPERTS), 1)
        pad_end = _mm_hi((r >= c).astype(F32), padded)
        start_scr[...] = pad_end - padded
        pe_ref[...] = pad_end.astype(jnp.int32)
        nu_ref[...] = (pad_end[N_EXPERTS - 1:, :] * (1.0 / block)).astype(jnp.int32)
        nbp = be_ref.shape[-1]
        first_row = lax.broadcasted_iota(jnp.int32, (1, nbp), 1).astype(F32) * block
        owner = jnp.sum((pad_end[:, 0:1] <= first_row).astype(F32), axis=0, keepdims=True)
        be_ref[...] = jnp.minimum(owner, N_EXPERTS - 1.0).astype(jnp.int32)

    t = ek_ref.shape[-1]
    eid = lax.broadcasted_iota(jnp.int32, (N_EXPERTS, t), 0)
    start = start_scr[:, 0:1]
    rows = [jnp.sum(jnp.where(eid == ek_ref[k:k + 1, :], start, 0.0), axis=0, keepdims=True)
            for k in range(TOP_K)]
    pos_ref[...] = jnp.concatenate(rows, axis=0).astype(jnp.int32) + rk_ref[...]


def _plan(counts, ek, rk, tr, n_blocks, block):
    nb, _, l = ek.shape
    nbp = -(-n_blocks // LANES) * LANES
    pick = pl.BlockSpec((None, TOP_K, tr), lambda b, i: (b, 0, i))
    const = lambda shape: pl.BlockSpec(shape, lambda b, i: (0, 0))
    return pl.pallas_call(
        functools.partial(_plan_kernel, block),
        out_shape=(jax.ShapeDtypeStruct((nb, TOP_K, l), jnp.int32),
                   jax.ShapeDtypeStruct((1, nbp), jnp.int32),
                   jax.ShapeDtypeStruct((N_EXPERTS, LANES), jnp.int32),
                   jax.ShapeDtypeStruct((1, LANES), jnp.int32)),
        grid=(nb, l // tr),
        in_specs=[const((N_EXPERTS, LANES)), pick, pick],
        out_specs=(pick, const((1, nbp)), const((N_EXPERTS, LANES)), const((1, LANES))),
        scratch_shapes=[pltpu.VMEM((N_EXPERTS, LANES), F32)],
        compiler_params=_params("arbitrary", "arbitrary"),
        name="plan",
    )(counts, ek, rk)


def _dispatch_kernel(n_prompt_tiles, pe_ref, pos_ref, xp_ref, xq_ref, xs_hbm, zero_scr, sem):
    s = pl.program_id(0)
    tm = pos_ref.shape[-1]

    @pl.when(s == 0)
    def _():
        zero_scr[...] = jnp.zeros_like(zero_scr)

        def tail_copy(e):
            block = zero_scr.shape[0]
            return pltpu.make_async_copy(zero_scr, xs_hbm.at[pl.ds(pe_ref[e] - block, block)], sem)

        def nonempty(e):
            return pe_ref[e] > jnp.where(e > 0, pe_ref[jnp.maximum(e - 1, 0)], 0)

        @pl.loop(0, N_EXPERTS)
        def _(e):
            @pl.when(nonempty(e))
            def _():
                tail_copy(e).start()

        @pl.loop(0, N_EXPERTS)
        def _(e):
            @pl.when(nonempty(e))
            def _():
                tail_copy(e).wait()

    def scatter(x_ref):
        def row(k, t):
            return pltpu.make_async_copy(x_ref.at[t], xs_hbm.at[pos_ref[k, t]], sem)

        @pl.loop(0, tm)
        def _(t):
            for k in range(TOP_K):
                row(k, t).start(priority=k % 2)

        @pl.loop(0, tm)
        def _(t):
            for k in range(TOP_K):
                row(k, t).wait()

    @pl.when(s < n_prompt_tiles)
    def _():
        scatter(xp_ref)

    @pl.when(s >= n_prompt_tiles)
    def _():
        scatter(xq_ref)


def _dispatch(x_p, x_s, pos, pad_end, n_rows, st, block):
    tile = (st.tm,) + TOKEN_TILE
    flat = lambda x: x.reshape((-1,) + TOKEN_TILE)
    return pl.pallas_call(
        functools.partial(_dispatch_kernel, st.n_p),
        out_shape=jax.ShapeDtypeStruct((n_rows,) + TOKEN_TILE, U32),
        grid_spec=pltpu.PrefetchScalarGridSpec(
            num_scalar_prefetch=1,
            grid=(st.n,),
            in_specs=[pl.BlockSpec((None, TOP_K, st.tm), lambda s, pe: (0, 0, s), memory_space=pltpu.SMEM),
                      pl.BlockSpec(tile, lambda s, pe: (jnp.minimum(s, st.n_p - 1), 0, 0)),
                      pl.BlockSpec(tile, lambda s, pe: (st.sample(s), 0, 0))],
            out_specs=pl.BlockSpec(memory_space=pl.ANY),
            scratch_shapes=[pltpu.VMEM((block,) + TOKEN_TILE, U32), pltpu.SemaphoreType.DMA(())]),
        compiler_params=_params("arbitrary"),
        name="dispatch",
    )(pad_end, pos, flat(x_p), flat(x_s))


def _experts_kernel(be_ref, nu_ref, x_ref, wg_ref, wu_ref, wd_ref, y_ref):
    @pl.when(pl.program_id(0) < nu_ref[0])
    def _():
        xb = _unpack_rows(x_ref[...]).astype(BF16)
        y_ref[...] = _pack_rows(_mm(_silu(_mm(xb, wg_ref[...])) * _mm(xb, wu_ref[...]), wd_ref[...]))


def _experts(xs, block_expert, n_used, w_gate, w_up, w_down, block):
    n_rows, d = xs.shape[0], D_MODEL
    used = lambda i, nu: jnp.minimum(i, nu[0] - 1)
    rows = pl.BlockSpec((block,) + TOKEN_TILE, lambda i, be, nu: (used(i, nu), 0, 0))
    return pl.pallas_call(
        _experts_kernel,
        out_shape=jax.ShapeDtypeStruct((n_rows,) + TOKEN_TILE, U32),
        grid_spec=pltpu.PrefetchScalarGridSpec(
            num_scalar_prefetch=2,
            grid=(n_rows // block,),
            in_specs=[rows,
                      pl.BlockSpec((None, d, D_EXPERT), lambda i, be, nu: (be[used(i, nu)], 0, 0)),
                      pl.BlockSpec((None, d, D_EXPERT), lambda i, be, nu: (be[used(i, nu)], 0, 0)),
                      pl.BlockSpec((None, D_EXPERT, d), lambda i, be, nu: (be[used(i, nu)], 0, 0))],
            out_specs=rows),
        compiler_params=_params("arbitrary"),
        name="experts",
    )(block_expert, n_used, xs, w_gate, w_up, w_down)


def _combine_kernel(n_prompt_tiles, pos_ref, pos_next_ref, wkt_ref, ys_hbm, hp_ref, hq_ref, x1p_ref, x1q_ref,
                    g2p_ref, g2q_ref, wsg_ref, wsu_ref, wsd_ref, fw_ref, yp_ref, yq_ref, rows_scr, sem):
    s = pl.program_id(0)
    tm = pos_ref.shape[-1]
    slot = lax.rem(s, 2)
    in_prompt = s < n_prompt_tiles

    def row(p_ref, sl, k, t):
        return pltpu.make_async_copy(ys_hbm.at[p_ref[k, t]], rows_scr.at[sl, k, t], sem.at[sl])

    def fetch(p_ref, sl):
        @pl.loop(0, tm)
        def _(t):
            for k in range(TOP_K):
                row(p_ref, sl, k, t).start(priority=1)

    @pl.when(s == 0)
    def _():
        fetch(pos_ref, 0)

    @pl.when(s + 1 < pl.num_programs(0))
    def _():
        fetch(pos_next_ref, 1 - slot)

    hb = jnp.where(in_prompt, hp_ref[...], hq_ref[...])
    acc = _mm(_silu(_mm(hb, wsg_ref[...])) * _mm(hb, wsu_ref[...]), wsd_ref[...])

    @pl.loop(0, tm)
    def _(t):
        for k in range(TOP_K):
            row(pos_ref, slot, k, t).wait()

    wkt = wkt_ref[...]
    for k in range(TOP_K):
        acc = acc + _unpack_rows(rows_scr[slot, k]) * wkt[:, k:k + 1]

    @pl.when(in_prompt)
    def _():
        yp_ref[...] = _rms(x1p_ref[...] + g2p_ref[...] * acc, fw_ref[...])

    @pl.when(jnp.logical_not(in_prompt))
    def _():
        yq_ref[...] = _rms(x1q_ref[...] + g2q_ref[...] * acc, fw_ref[...])


def _combine(ys, pos, wkt, h_p, h_s, x1_p, x1_s, mod_p, mod_s, ws_gate, ws_up, ws_down, final_w, st):
    nb, l, d = x1_p.shape
    real = x1_s.shape[1] // st.tm
    prompt = lambda w, col=0: pl.BlockSpec((None, st.tm, w), lambda s: (*st.prompt(s), col))
    sample = lambda w, n, col=0: pl.BlockSpec((None, st.tm, w), lambda s: (0, st.sample(s, n), col))
    full = lambda shape: pl.BlockSpec(shape, lambda s: (0,) * len(shape))
    pick = lambda step: pl.BlockSpec((None, TOP_K, st.tm), lambda s: (0, 0, step(s)), memory_space=pltpu.SMEM)
    return pl.pallas_call(
        functools.partial(_combine_kernel, st.n_p),
        out_shape=(jax.ShapeDtypeStruct((nb, l, d), F32), jax.ShapeDtypeStruct((1, st.n_s * st.tm, d), F32)),
        grid=(st.n,),
        in_specs=[pick(lambda s: s), pick(lambda s: jnp.minimum(s + 1, st.n - 1)),
                  pl.BlockSpec((None, st.tm, TOP_K), lambda s: (0, s, 0)), pl.BlockSpec(memory_space=pl.ANY),
                  prompt(d), sample(d, real), prompt(d), sample(d, real),
                  pl.BlockSpec((None, 1, d), lambda s: (st.prompt(s)[0], 0, 5)), sample(d, real, 5),
                  full((d, D_SHARED)), full((d, D_SHARED)), full((D_SHARED, d)), full((1, d))],
        out_specs=(prompt(d), sample(d, None)),
        scratch_shapes=[pltpu.VMEM((2, TOP_K, st.tm) + TOKEN_TILE, U32), pltpu.SemaphoreType.DMA((2,))],
        compiler_params=_params("arbitrary"),
        name="combine",
    )(pos, pos, wkt, ys, h_p, h_s, x1_p, x1_s, mod_p, mod_s, ws_gate, ws_up, ws_down, final_w)


def _moe(pre_p, pre_s, mod_p, mod_s, w):
    x1_p, h_p, hf3_p, logits_p = pre_p
    x1_s, h_s, hf3_s, logits_s = pre_s
    nb, l, _ = x1_p.shape
    ls = x1_s.shape[1]
    tr = min(l, 512)
    ls_pad = -(-ls // tr) * tr
    hf3_s = jnp.pad(hf3_s, ((0, 0), (0, ls_pad - ls), (0, 0), (0, 0)))
    logits_s = jnp.pad(logits_s, ((0, 0), (0, 0), (0, ls_pad - ls)))
    st = _Streams(nb, l, ls_pad, tr)
    st_c = _Streams(nb, l, ls_pad, min(l, ls, 128))
    n_tokens = st.n * st.tm
    n_blocks = n_tokens * TOP_K // MOE_BLOCK + N_EXPERTS
    ek, rk, wkt, counts = _route(logits_p, logits_s, w["router_bias"], st)
    pos, block_expert, pad_end, n_used = _plan(counts, ek, rk, tr, n_blocks, MOE_BLOCK)
    xs = _dispatch(hf3_p, hf3_s, pos, pad_end[:, 0], n_blocks * MOE_BLOCK, st, MOE_BLOCK)
    ys = _experts(xs, block_expert[0, :n_blocks], n_used[0, :1], w["w_gate"], w["w_up"], w["w_down"], MOE_BLOCK)
    y_p, y_s = _combine(ys, pos, wkt, h_p, h_s, x1_p, x1_s, mod_p, mod_s, w["ws_gate"], w["ws_up"], w["ws_down"],
                        w["final_norm"], st_c)
    return y_p, y_s[:, :ls]


def _trunk(x, mod, per_token, states, w):
    nb, l, _ = x.shape
    tm_proj, tn_proj, tm_post = min(l, 1024), 2048, min(l, 512)
    proj, ba, bat = _in_projection(x, mod, per_token, w["norm_mix"], w["w_main"], w["w_in"], tm_proj, tn_proj)
    if states is None:
        c0 = jnp.zeros((1, DN_CONV - 1, DN_CONV_W), F32)
        sd0 = jnp.zeros((1, DN_HEADS, DN_DK, DN_DV), F32)
        sr0 = jnp.zeros((1, RET_HEADS, RET_DK, RET_DV), F32)
        o_a, sd, cb = _delta_prefill(proj, bat, w["conv_w"], w["a_log"], w["dt_bias"], w["dn_norm"], sd0, c0)
        o_b, sr = _ret_prefill(proj, w["inv_freq"], w["gn_w"], w["gn_b"], sr0)
    else:
        c0, sd0, sr0 = states
        n = nb * l
        o_a, sd, cb = _delta_step(proj.reshape(n, 1, PROJ_W), ba.reshape(n, 1, BA_W), w["conv_w"], w["a_log"],
                                  w["dt_bias"], w["dn_norm"], sd0, c0)
        o_b, sr = _ret_step(proj.reshape(n, 1, PROJ_W), w["inv_freq"], w["gn_w"], w["gn_b"], sr0)
        o_a = o_a.reshape(nb, l, DN_VW)
        o_b = o_b.reshape(nb, l, RET_VW)
    pre_moe = _post_mixer(o_a, o_b, proj, x, mod, per_token, w["w_down_a"], w["w_down_b"], w["w_out"],
                          w["norm_ffn"], w["router_w"], tm_post)
    return pre_moe, cb, sd, sr


def kernel(x_prompt, x_sample, state_conv, state_delta, state_ret, c_prompt, c_sample, w_mod, b_mod, norm_mix_w, w_in, conv_w, a_log, dt_bias, dn_norm_w, ret_gn_w, ret_gn_b, w_down_a, w_down_b, w_out, norm_ffn_w, router_w, router_bias, w_gate, w_up, w_down, ws_gate, ws_up, ws_down, final_norm_w):
    bp, lp, _ = x_prompt.shape
    bs = x_sample.shape[0]
    half = RET_DK // 2
    w_in0 = w_in[0]
    c0, c1 = DN_CONV_W, DN_CONV_W + 2 * DN_HEADS
    w = {
        "norm_mix": norm_mix_w[0].reshape(1, D_MODEL),
        "w_main": jnp.concatenate([w_in0[:, :c0].astype(BF16), w_in0[:, c1:].astype(BF16)], axis=1),
        "w_in": w_in0,
        "conv_w": conv_w[0], "a_log": a_log[0], "dt_bias": dt_bias[0], "dn_norm": dn_norm_w[0],
        "inv_freq": (ROPE_BASE ** (-jnp.arange(half, dtype=F32) / half)).reshape(1, half),
        "gn_w": ret_gn_w[0], "gn_b": ret_gn_b[0],
        "w_down_a": w_down_a[0].astype(BF16), "w_down_b": w_down_b[0].astype(BF16), "w_out": w_out[0].astype(BF16),
        "norm_ffn": norm_ffn_w[0].reshape(1, D_MODEL), "router_w": router_w[0], "router_bias": router_bias[0],
        "w_gate": w_gate[0], "w_up": w_up[0], "w_down": w_down[0],
        "ws_gate": ws_gate[0].astype(BF16), "ws_up": ws_up[0].astype(BF16), "ws_down": ws_down[0].astype(BF16),
        "final_norm": final_norm_w.reshape(1, D_MODEL),
    }
    mod = _modulation(jnp.concatenate([c_prompt, c_sample], axis=0), w_mod[0], b_mod[0])
    mod_p = mod[:bp].reshape(bp, 1, MOD_CHUNKS * D_MODEL)
    mod_s = mod[bp:].reshape(1, bs, MOD_CHUNKS * D_MODEL)

    pre_p, conv_p, delta_p, ret_p = _trunk(x_prompt, mod_p, False, None, w)
    pre_s, conv_s, delta_s, ret_s = _trunk(x_sample.reshape(1, bs, D_MODEL), mod_s, True,
                                           (state_conv[0], state_delta[0], state_ret[0]), w)
    y_p, y_s = _moe(pre_p, pre_s, mod_p, mod_s, w)
    return (y_p, y_s.reshape(bs, 1, D_MODEL), conv_p[None], delta_p[None], ret_p[None],
            conv_s[None], delta_s[None], ret_s[None])
```
